```python
import math, functools
import jax, jax.numpy as jnp
from jax import lax
import numpy as np

D_MODEL = 2048
BATCH = 2
SEQ = 4096
DEPTH = 2
DEC_BATCH = 128
DEC_SEQ = 1
PAST_LEN = 2048
PAGE_SIZE = 128

ATT_HEADS = 16
ATT_DH = 64
ATT_W = ATT_HEADS * ATT_DH
MOBA_BLOCK = 256
MOBA_TOPK = 3
MOBA_QCHUNK = 32
REL_BUCKETS = 32
REL_MAX_DIST = 128
RET_HEADS = 8
RET_DK = 64
RET_DV = 128
RET_QK_W = RET_HEADS * RET_DK
RET_V_W = RET_HEADS * RET_DV
RET_CHUNK = 128
ROPE_BASE = 10000.0
S5_W = 1024
S5_GROUP = 16
S5_GROUPS = S5_W // S5_GROUP
S5_STATE = 64
N_BRANCH = 3
BRANCH_W = 1024
D_FF = ((8 * D_MODEL + 3 * 256 - 1) // (3 * 256)) * 256
NORM_EPS = 1e-6
GN_EPS = 1e-5
IN_WIDTHS = (ATT_W, ATT_W, ATT_W, RET_QK_W, RET_QK_W, RET_V_W, RET_V_W, S5_W, N_BRANCH * D_MODEL)
D_IN = 3 * ATT_W + 2 * RET_QK_W + 2 * RET_V_W + S5_W + N_BRANCH * D_MODEL

kernel_name = 'hybrid_moba_retnet_s5_step'


def _rmsnorm(x, g):
    x32 = x.astype(jnp.float32)
    y = x32 * lax.rsqrt(jnp.mean(x32 * x32, axis=-1, keepdims=True) + NORM_EPS)
    return (y * g.astype(jnp.float32)).astype(x.dtype)


def _split_in(z):
    parts, start = [], 0
    for w in IN_WIDTHS:
        parts.append(z[..., start:start + w])
        start += w
    return parts


def _t5_bucket(dist):
    n = jnp.maximum(dist, 0)
    max_exact = REL_BUCKETS // 2
    nf = jnp.maximum(n, 1).astype(jnp.float32)
    large = max_exact + (jnp.log(nf / max_exact) / math.log(REL_MAX_DIST / max_exact)
                         * (REL_BUCKETS - max_exact)).astype(jnp.int32)
    large = jnp.minimum(large, REL_BUCKETS - 1)
    return jnp.where(n < max_exact, n, large)


def _rel_bias(rel_bias, dist):
    hidx = jnp.arange(ATT_HEADS).reshape((1, 1, ATT_HEADS) + (1,) * (dist.ndim - 3))
    return rel_bias.astype(jnp.float32)[_t5_bucket(dist), hidx]


def _moba_select(q, block_mean, qpos):
    b_, t_, h_ = q.shape[:3]
    nb = block_mean.shape[1]
    kk = min(MOBA_TOPK, nb)
    cur = qpos // MOBA_BLOCK
    s = jnp.einsum('bthd,bnhd->bthn', q.astype(jnp.float32), block_mean)
    past = jnp.arange(nb)[None, :] < cur[:, None]
    s = jnp.where(past[None, :, None, :], s, -jnp.inf)
    _, top = lax.top_k(s, kk)
    own = jnp.broadcast_to(cur[None, :, None, None], (b_, t_, h_, 1)).astype(jnp.int32)
    blocks = jnp.concatenate([top.astype(jnp.int32), own], axis=-1)
    valid = jnp.concatenate([jnp.arange(kk)[None, :] < cur[:, None], jnp.ones((t_, 1), bool)], axis=-1)
    return blocks, jnp.broadcast_to(valid[None, :, None, :], blocks.shape)


def _gathered_logits(q, kg, kpos, qpos, rel_bias):
    s = jnp.einsum('bthd,bthjrd->bthjr', q, kg).astype(jnp.float32) * ATT_DH ** -0.5
    return s + _rel_bias(rel_bias, qpos[None, :, None, None, None] - kpos)


def _moba_prompt(q, k, v, rel_bias):
    b_, s_, h_, dh = q.shape
    nb = -(-s_ // MOBA_BLOCK)
    pad = ((0, 0), (0, nb * MOBA_BLOCK - s_), (0, 0), (0, 0))
    kbh = jnp.pad(k, pad).reshape(b_, nb, MOBA_BLOCK, h_, dh).transpose(0, 3, 1, 2, 4)
    vbh = jnp.pad(v, pad).reshape(b_, nb, MOBA_BLOCK, h_, dh).transpose(0, 3, 1, 2, 4)
    kmean = jnp.mean(kbh, axis=3, dtype=jnp.float32).transpose(0, 2, 1, 3)
    n_chunks = s_ // MOBA_QCHUNK
    q_chunks = q.reshape(b_, n_chunks, MOBA_QCHUNK, h_, dh).swapaxes(0, 1)
    bi = jnp.arange(b_)[:, None, None, None]
    hi = jnp.arange(h_)[None, None, :, None]
    r = jnp.arange(MOBA_BLOCK)

    def chunk(args):
        qc, ci = args
        qpos = ci * MOBA_QCHUNK + jnp.arange(MOBA_QCHUNK)
        blocks, valid = _moba_select(qc, kmean, qpos)
        kg = kbh[bi, hi, blocks]
        vg = vbh[bi, hi, blocks]
        kpos = blocks[..., None] * MOBA_BLOCK + r
        mask = valid[..., None] & (kpos <= qpos[None, :, None, None, None])
        logits = jnp.where(mask, _gathered_logits(qc, kg, kpos, qpos, rel_bias), -jnp.inf)
        p = jax.nn.softmax(logits, axis=(-2, -1))
        return jnp.einsum('bthjr,bthjrd->bthd', p.astype(vg.dtype), vg)

    out = lax.map(chunk, (q_chunks, jnp.arange(n_chunks)))
    return out.swapaxes(0, 1).reshape(b_, s_, h_ * dh)


def _moba_sample(q, k_new, v_new, pool_k, pool_v, page_table, rel_bias):
    db, t_, h_, dh = q.shape
    n_pages = page_table.shape[1]
    past = n_pages * PAGE_SIZE
    nb = -(-(past + t_) // MOBA_BLOCK)
    qpos = past + jnp.arange(t_)
    page_blk = (jnp.arange(n_pages) * PAGE_SIZE) // MOBA_BLOCK
    page_sum = jnp.sum(pool_k, axis=1, dtype=jnp.float32)[page_table]
    blk_sum = jnp.zeros((db, nb, h_, dh), jnp.float32).at[:, page_blk].add(page_sum)
    blk_sum = blk_sum.at[:, qpos // MOBA_BLOCK].add(k_new.astype(jnp.float32))
    blocks, valid = _moba_select(q, blk_sum / MOBA_BLOCK, qpos)
    kpos = blocks[..., None] * MOBA_BLOCK + jnp.arange(MOBA_BLOCK)
    phys = page_table[jnp.arange(db)[:, None, None, None, None], jnp.minimum(kpos // PAGE_SIZE, n_pages - 1)]
    off = kpos % PAGE_SIZE
    hi = jnp.arange(h_)[None, None, :, None, None]
    kg = pool_k[phys, off, hi]
    vg = pool_v[phys, off, hi]
    mask_c = valid[..., None] & (kpos < past) & (kpos <= qpos[None, :, None, None, None])
    logit_c = jnp.where(mask_c, _gathered_logits(q, kg, kpos, qpos, rel_bias), -jnp.inf)
    new_blk = qpos // MOBA_BLOCK
    in_sel = jnp.any((blocks[..., None] == new_blk) & valid[..., None], axis=3)
    mask_n = in_sel & (qpos[None, None, None, :] <= qpos[None, :, None, None])
    logit_n = (jnp.einsum('bthd,bkhd->bthk', q, k_new).astype(jnp.float32) * ATT_DH ** -0.5
               + _rel_bias(rel_bias, (qpos[:, None] - qpos[None, :])[None, :, None, :]))
    logit_n = jnp.where(mask_n, logit_n, -jnp.inf)
    n_c = blocks.shape[-1] * MOBA_BLOCK
    p = jax.nn.softmax(jnp.concatenate([logit_c.reshape(db, t_, h_, n_c), logit_n], axis=-1), axis=-1)
    p_c = p[..., :n_c].reshape(logit_c.shape).astype(vg.dtype)
    p_n = p[..., n_c:].astype(v_new.dtype)
    out = jnp.einsum('bthjr,bthjrd->bthd', p_c, vg) + jnp.einsum('bthk,bkhd->bthd', p_n, v_new)
    return out.reshape(db, t_, h_ * dh)


def _rotary(x, pos):
    half = x.shape[-1] // 2
    inv = 1.0 / (ROPE_BASE ** jnp.linspace(0.0, 1.0, half))
    ang = pos.astype(jnp.float32)[:, None] * inv[None, :]
    cos, sin = jnp.cos(ang)[None, :, None, :], jnp.sin(ang)[None, :, None, :]
    x1, x2 = x[..., :half], x[..., half:]
    return jnp.concatenate([x1 * cos - x2 * sin, x1 * sin + x2 * cos], axis=-1)


def _ret_log_decay():
    return jnp.log(1.0 - 2.0 ** (-5.0 - jnp.arange(RET_HEADS, dtype=jnp.float32)))


def _ret_qkv(rq, rk, rv, pos):
    b_, t_ = rq.shape[:2]
    q = _rotary(rq.astype(jnp.float32).reshape(b_, t_, RET_HEADS, RET_DK), pos)
    k = _rotary(rk.astype(jnp.float32).reshape(b_, t_, RET_HEADS, RET_DK), pos) * RET_DK ** -0.5
    v = rv.astype(jnp.float32).reshape(b_, t_, RET_HEADS, RET_DV)
    return q, k, v


def _retention_chunk(q, k, v, s0, log_g):
    t_ = q.shape[1]
    i = jnp.arange(t_, dtype=jnp.float32)
    diff = i[:, None] - i[None, :]
    dmask = jnp.where(diff >= 0, jnp.exp(log_g[:, None, None] * jnp.maximum(diff, 0.0)), 0.0)
    inner = jnp.einsum('bhij,bjhv->bihv', jnp.einsum('bihd,bjhd->bhij', q, k) * dmask, v)
    cross = jnp.einsum('bihd,bhdv->bihv', q, s0) * jnp.exp(log_g[None, :] * (i[:, None] + 1.0))[None, :, :, None]
    kdec = k * jnp.exp(log_g[None, :] * (t_ - 1.0 - i[:, None]))[None, :, :, None]
    s_new = jnp.exp(log_g * t_)[None, :, None, None] * s0 + jnp.einsum('bjhd,bjhv->bhdv', kdec, v)
    return inner + cross, s_new


def _retention_prompt(q, k, v):
    b_, s_ = q.shape[:2]
    nc = s_ // RET_CHUNK
    log_g = _ret_log_decay()

    def to_chunks(a):
        return a.reshape((b_, nc, RET_CHUNK) + a.shape[2:]).swapaxes(0, 1)

    def step(s, inp):
        o, s = _retention_chunk(inp[0], inp[1], inp[2], s, log_g)
        return s, o

    s0 = jnp.zeros((b_, RET_HEADS, RET_DK, RET_DV), jnp.float32)
    s_fin, o = lax.scan(step, s0, (to_chunks(q), to_chunks(k), to_chunks(v)))
    return o.swapaxes(0, 1).reshape(b_, s_, RET_HEADS, RET_DV), s_fin


def _ret_out(o, g):
    mu = jnp.mean(o, axis=-1, keepdims=True)
    var = jnp.mean((o - mu) ** 2, axis=-1, keepdims=True)
    on = ((o - mu) * lax.rsqrt(var + GN_EPS)).reshape(o.shape[0], o.shape[1], RET_V_W)
    return (jax.nn.silu(g.astype(jnp.float32)) * on).astype(g.dtype)


def _ssm_binop(e1, e2):
    a1, b1 = e1
    a2, b2 = e2
    return a2 * a1, a2 * b1 + b2


def _s5(u, x0_re, x0_im, a_re, a_im, log_dt, b_re, b_im, c_re, c_im, d, w_glu):
    b_, t_ = u.shape[:2]
    f32 = jnp.float32
    lam = lax.complex(a_re.astype(f32), a_im.astype(f32))
    lam_bar = jnp.exp(lam * jnp.exp(log_dt.astype(f32))[:, None])
    b_bar = ((lam_bar - 1.0) / lam)[..., None] * lax.complex(b_re.astype(f32), b_im.astype(f32))
    ug = u.astype(f32).reshape(b_, t_, S5_GROUPS, S5_GROUP)
    bu = jnp.einsum('gpc,btgc->btgp', b_bar, ug.astype(jnp.complex64))
    x0 = lax.complex(x0_re.astype(f32), x0_im.astype(f32))
    bu = bu.at[:, 0].add(lam_bar[None] * x0)
    _, xs = lax.associative_scan(_ssm_binop, (jnp.broadcast_to(lam_bar, bu.shape), bu), axis=1)
    c = lax.complex(c_re.astype(f32), c_im.astype(f32))
    y = jnp.einsum('gcp,btgp->btgc', c, xs).real + d.astype(f32) * ug
    z = jax.nn.gelu(y.reshape(b_, t_, S5_W))
    out = z * jax.nn.sigmoid(z @ w_glu.astype(f32))
    x_last = xs[:, -1]
    return out.astype(u.dtype), x_last.real, x_last.imag


def _mix_prompt(parts, rel_bias, s5p):
    aq, ak, av, rq, rk, rv, rg, su = parts
    b_, s_ = aq.shape[:2]
    pos = jnp.arange(s_)
    k = ak.reshape(b_, s_, ATT_HEADS, ATT_DH)
    v = av.reshape(b_, s_, ATT_HEADS, ATT_DH)
    o_att = _moba_prompt(aq.reshape(b_, s_, ATT_HEADS, ATT_DH), k, v, rel_bias)
    q_r, k_r, v_r = _ret_qkv(rq, rk, rv, pos)
    o_r, s_ret = _retention_prompt(q_r, k_r, v_r)
    o_ret = _ret_out(o_r, rg)
    zeros = jnp.zeros((b_, S5_GROUPS, S5_STATE), jnp.float32)
    o_s5, s_re, s_im = _s5(su, zeros, zeros, *s5p)
    return (o_att, o_ret, o_s5), (k, v, s_ret, s_re, s_im)


def _mix_sample(parts, rel_bias, s5p, pool_k, pool_v, page_table, ret0, s5_re0, s5_im0):
    aq, ak, av, rq, rk, rv, rg, su = parts
    db, t_ = aq.shape[:2]
    pos = page_table.shape[1] * PAGE_SIZE + jnp.arange(t_)
    k = ak.reshape(db, t_, ATT_HEADS, ATT_DH)
    v = av.reshape(db, t_, ATT_HEADS, ATT_DH)
    o_att = _moba_sample(aq.reshape(db, t_, ATT_HEADS, ATT_DH), k, v, pool_k, pool_v, page_table, rel_bias)
    q_r, k_r, v_r = _ret_qkv(rq, rk, rv, pos)
    o_r, s_ret = _retention_chunk(q_r, k_r, v_r, ret0.astype(jnp.float32), _ret_log_decay())
    o_ret = _ret_out(o_r, rg)
    o_s5, s_re, s_im = _s5(su, s5_re0, s5_im0, *s5p)
    return (o_att, o_ret, o_s5), (k, v, s_ret, s_re, s_im)


def _layer(x, c, mix, lw):
    norm1, norm2, w_ada, b_ada, w_in, w_branch, w_out, w_ffn_in, w_ffn_out = lw
    mod = (jax.nn.silu(c) @ w_ada + b_ada).reshape(c.shape[0], 6, 1, D_MODEL)
    sh1, sc1, g1, sh2, sc2, g2 = mod[:, 0], mod[:, 1], mod[:, 2], mod[:, 3], mod[:, 4], mod[:, 5]
    h = _rmsnorm(x, norm1) * (1.0 + sc1) + sh1
    parts = _split_in(h @ w_in)
    (o_att, o_ret, o_s5), new_state = mix(parts[:-1])
    o = jnp.stack([o_att, o_ret, o_s5], axis=2)
    proj = jnp.einsum('btnc,ncd->btnd', o, w_branch).astype(jnp.float32)
    gates = jax.nn.sigmoid(parts[-1].reshape(x.shape[0], x.shape[1], N_BRANCH, D_MODEL).astype(jnp.float32))
    merged = jnp.sum(gates * proj, axis=2).astype(x.dtype)
    x = x + g1 * (merged @ w_out)
    h2 = _rmsnorm(x, norm2) * (1.0 + sc2) + sh2
    up = h2 @ w_ffn_in
    x = x + g2 * ((jax.nn.silu(up[..., :D_FF]) * up[..., D_FF:]) @ w_ffn_out)
    return x, new_state


def setup_inputs(seed: int = 0) -> dict:
    key = jax.random.key(seed)
    ks = jax.random.split(key, 32)
    f32 = jnp.float32

    def nrm(i, shape, scale):
        return jax.random.normal(ks[i], shape, f32) * scale

    n_pages = PAST_LEN // PAGE_SIZE
    n_used = DEC_BATCH * n_pages
    n_pool = n_used + max(1, n_used // 4)
    page_table = jax.random.permutation(ks[0], n_pool)[:n_used].reshape(DEC_BATCH, n_pages).astype(jnp.int32)
    kv_shape = (DEPTH, n_pool, PAGE_SIZE, ATT_HEADS, ATT_DH)
    s5_state_shape = (DEPTH, DEC_BATCH, S5_GROUPS, S5_STATE)
    s5_a_shape = (DEPTH, S5_GROUPS, S5_STATE)
    a_im = jnp.pi * jnp.arange(S5_STATE, dtype=f32)
    return {
        'x_prompt': nrm(1, (BATCH, SEQ, D_MODEL), 1.0),
        'x_sample': nrm(2, (DEC_BATCH, DEC_SEQ, D_MODEL), 1.0),
        'c_prompt': nrm(3, (BATCH, D_MODEL), 1.0),
        'c_sample': nrm(4, (DEC_BATCH, D_MODEL), 1.0),
        'cache_k': nrm(5, kv_shape, 1.0),
        'cache_v': nrm(6, kv_shape, 1.0),
        'page_table': page_table,
        'state_ret': nrm(7, (DEPTH, DEC_BATCH, RET_HEADS, RET_DK, RET_DV), 1.0),
        'state_s5_re': nrm(8, s5_state_shape, 0.1),
        'state_s5_im': nrm(9, s5_state_shape, 0.1),
        'rel_bias': nrm(10, (REL_BUCKETS, ATT_HEADS), 0.3),
        'norm1_g': 1.0 + nrm(11, (DEPTH, D_MODEL), 0.05),
        'norm2_g': 1.0 + nrm(12, (DEPTH, D_MODEL), 0.05),
        'w_ada': nrm(13, (DEPTH, D_MODEL, 6 * D_MODEL), 0.5 * D_MODEL ** -0.5),
        'b_ada': nrm(14, (DEPTH, 6 * D_MODEL), 0.01),
        'w_in': nrm(15, (DEPTH, D_MODEL, D_IN), D_MODEL ** -0.5),
        's5_a_re': -0.5 * jnp.exp(nrm(16, s5_a_shape, 0.05)),
        's5_a_im': a_im + nrm(17, s5_a_shape, 0.01),
        's5_log_dt': jax.random.uniform(ks[18], (DEPTH, S5_GROUPS), f32, math.log(1e-3), math.log(1e-1)),
        's5_b_re': nrm(19, (DEPTH, S5_GROUPS, S5_STATE, S5_GROUP), (2 * S5_GROUP) ** -0.5),
        's5_b_im': nrm(20, (DEPTH, S5_GROUPS, S5_STATE, S5_GROUP), (2 * S5_GROUP) ** -0.5),
        's5_c_re': nrm(21, (DEPTH, S5_GROUPS, S5_GROUP, S5_STATE), 0.25),
        's5_c_im': nrm(22, (DEPTH, S5_GROUPS, S5_GROUP, S5_STATE), 0.25),
        's5_d': nrm(23, (DEPTH, S5_GROUPS, S5_GROUP), 1.0),
        'w_glu': nrm(24, (DEPTH, S5_W, S5_W), S5_W ** -0.5),
        'w_branch': nrm(25, (DEPTH, N_BRANCH, BRANCH_W, D_MODEL), BRANCH_W ** -0.5),
        'w_out': nrm(26, (DEPTH, D_MODEL, D_MODEL), D_MODEL ** -0.5),
        'w_ffn_in': nrm(27, (DEPTH, D_MODEL, 2 * D_FF), D_MODEL ** -0.5),
        'w_ffn_out': nrm(28, (DEPTH, D_FF, D_MODEL), D_FF ** -0.5),
        'final_g': 1.0 + nrm(29, (D_MODEL,), 0.05),
    }


def reference(x_prompt, x_sample, c_prompt, c_sample, cache_k, cache_v, page_table, state_ret,
              state_s5_re, state_s5_im, rel_bias, norm1_g, norm2_g, w_ada, b_ada, w_in,
              s5_a_re, s5_a_im, s5_log_dt, s5_b_re, s5_b_im, s5_c_re, s5_c_im, s5_d, w_glu,
              w_branch, w_out, w_ffn_in, w_ffn_out, final_g):
    xp, xs = x_prompt, x_sample
    new_p, new_s = [], []
    for l in range(DEPTH):
        lw = (norm1_g[l], norm2_g[l], w_ada[l], b_ada[l], w_in[l], w_branch[l], w_out[l],
              w_ffn_in[l], w_ffn_out[l])
        s5p = (s5_a_re[l], s5_a_im[l], s5_log_dt[l], s5_b_re[l], s5_b_im[l], s5_c_re[l],
               s5_c_im[l], s5_d[l], w_glu[l])
        xp, st_p = _layer(xp, c_prompt, functools.partial(_mix_prompt, rel_bias=rel_bias, s5p=s5p), lw)
        new_p.append(st_p)
        mix_s = functools.partial(_mix_sample, rel_bias=rel_bias, s5p=s5p, pool_k=cache_k[l],
                                  pool_v=cache_v[l], page_table=page_table, ret0=state_ret[l],
                                  s5_re0=state_s5_re[l], s5_im0=state_s5_im[l])
        xs, st_s = _layer(xs, c_sample, mix_s, lw)
        new_s.append(st_s)
    y_prompt = _rmsnorm(xp, final_g)
    y_sample = _rmsnorm(xs, final_g)
    k_prompt, v_prompt, ret_prompt, s5_re_prompt, s5_im_prompt = [jnp.stack(a) for a in zip(*new_p)]
    k_sample, v_sample, ret_sample, s5_re_sample, s5_im_sample = [jnp.stack(a) for a in zip(*new_s)]
    return (y_prompt, y_sample, k_prompt, v_prompt, ret_prompt, s5_re_prompt, s5_im_prompt,
            k_sample, v_sample, ret_sample, s5_re_sample, s5_im_sample)
```

```python
import functools
import math

import numpy as np
import jax
import jax.numpy as jnp
from jax import lax
from jax.experimental import pallas as pl
from jax.experimental.pallas import tpu as pltpu

F32 = jnp.float32
BF16 = jnp.bfloat16
HIGHEST = lax.Precision.HIGHEST

D_MODEL = 2048
PAGE_SIZE = 128
ATT_HEADS = 16
ATT_DH = 64
ATT_W = ATT_HEADS * ATT_DH
MOBA_BLOCK = 256
MOBA_TOPK = 3
REL_BUCKETS = 32
REL_MAX_DIST = 128
RET_HEADS = 8
RET_DK = 64
RET_DV = 128
RET_QK_W = RET_HEADS * RET_DK
RET_V_W = RET_HEADS * RET_DV
RET_CHUNK = 128
ROPE_BASE = 10000.0
S5_W = 1024
S5_GROUP = 16
S5_GROUPS = S5_W // S5_GROUP
S5_STATE = 64
N_BRANCH = 3
BRANCH_W = 1024
D_FF = ((8 * D_MODEL + 3 * 256 - 1) // (3 * 256)) * 256
NORM_EPS = 1e-6
GN_EPS = 1e-5
D_IN = 3 * ATT_W + 2 * RET_QK_W + 2 * RET_V_W + S5_W + N_BRANCH * D_MODEL

OFF_AQ, OFF_AK, OFF_AV = 0, ATT_W, 2 * ATT_W
OFF_RQ = 3 * ATT_W
OFF_RK = OFF_RQ + RET_QK_W
OFF_RV = OFF_RK + RET_QK_W
OFF_RG = OFF_RV + RET_V_W
OFF_SU = OFF_RG + RET_V_W
OFF_GATE = OFF_SU + S5_W

LANES = 128
SUBLANES = 8
VMEM_LIMIT = 56 * 1024 * 1024
NEG = -1e30
S5_GT = 8
S5_TILE_STATES = S5_GT * S5_STATE
S5_NT = S5_GROUPS // S5_GT


def _params(sem):
    return pltpu.CompilerParams(dimension_semantics=sem, vmem_limit_bytes=VMEM_LIMIT)


def _sigmoid(x):
    return 1.0 / (1.0 + jnp.exp(-x))


def _silu(x):
    return x * _sigmoid(x)


def _dot(a, b):
    return jnp.dot(a, b, preferred_element_type=F32)


def _dot_nt(a, b, precision=None):
    return lax.dot_general(a, b, (((1,), (1,)), ((), ())), precision=precision,
                           preferred_element_type=F32)


def _dot_tn(a, b):
    return lax.dot_general(a, b, (((0,), (0,)), ((), ())), preferred_element_type=F32)


def _mod_spec(kind, tm, tn, tiles_per_seq, col_blocked):
    if kind == "prompt":
        if col_blocked:
            return pl.BlockSpec((None, 1, tn), lambda i, j: (i // tiles_per_seq, 0, j))
        return pl.BlockSpec((None, 1, tn), lambda i, j: (i // tiles_per_seq, 0, 0))
    if col_blocked:
        return pl.BlockSpec((tm, tn), lambda i, j: (i, j))
    return pl.BlockSpec((tm, tn), lambda i, j: (i, 0))


def _ada_body(c_ref, w_ref, b_ref, o_ref):
    a = _silu(c_ref[...]).astype(BF16)
    o_ref[...] = _dot(a, w_ref[...]) + b_ref[...]


def _ada(c_all, w_bf, b):
    m, d = c_all.shape
    n = w_bf.shape[1]
    tn = 1024
    return pl.pallas_call(
        _ada_body,
        grid=(n // tn,),
        in_specs=[pl.BlockSpec((m, d), lambda j: (0, 0)),
                  pl.BlockSpec((d, tn), lambda j: (0, j)),
                  pl.BlockSpec((1, tn), lambda j: (0, j))],
        out_specs=pl.BlockSpec((m, tn), lambda j: (0, j)),
        out_shape=jax.ShapeDtypeStruct((m, n), F32),
        compiler_params=_params(("arbitrary",)),
    )(c_all, w_bf, b.reshape(1, n))


def _norm_mod(x, g, sc, sh):
    y = x * lax.rsqrt(jnp.mean(x * x, axis=-1, keepdims=True) + NORM_EPS)
    return (y * g) * (1.0 + sc) + sh


def _nmm_body(x_ref, g_ref, sc_ref, sh_ref, w_ref, o_ref, h_ref):
    @pl.when(pl.program_id(1) == 0)
    def _():
        h_ref[...] = _norm_mod(x_ref[...], g_ref[...], sc_ref[...], sh_ref[...]).astype(BF16)

    o_ref[...] = _dot(h_ref[...], w_ref[...])


def _norm_mod_matmul(x, g, sc, sh, w_bf, kind, tm, tn, tiles_per_seq):
    m, d = x.shape
    n = w_bf.shape[1]
    mod = _mod_spec(kind, tm, d, tiles_per_seq, False)
    return pl.pallas_call(
        _nmm_body,
        grid=(m // tm, n // tn),
        in_specs=[pl.BlockSpec((tm, d), lambda i, j: (i, 0)),
                  pl.BlockSpec((1, d), lambda i, j: (0, 0)),
                  mod, mod,
                  pl.BlockSpec((d, tn), lambda i, j: (0, j))],
        out_specs=pl.BlockSpec((tm, tn), lambda i, j: (i, j)),
        out_shape=jax.ShapeDtypeStruct((m, n), F32),
        scratch_shapes=[pltpu.VMEM((tm, d), BF16)],
        compiler_params=_params(("arbitrary", "arbitrary")),
    )(x, g.reshape(1, d), sc, sh, w_bf)


def _ffn_in_body(x_ref, g_ref, sc_ref, sh_ref, w1_ref, w2_ref, o_ref, h_ref):
    @pl.when(pl.program_id(1) == 0)
    def _():
        h_ref[...] = _norm_mod(x_ref[...], g_ref[...], sc_ref[...], sh_ref[...]).astype(BF16)

    h = h_ref[...]
    o_ref[...] = (_silu(_dot(h, w1_ref[...])) * _dot(h, w2_ref[...])).astype(BF16)


def _norm_mod_ffn_in(x, g, sc, sh, w_bf, kind, tm, tn, tiles_per_seq):
    m, d = x.shape
    nj = D_FF // tn
    mod = _mod_spec(kind, tm, d, tiles_per_seq, False)
    return pl.pallas_call(
        _ffn_in_body,
        grid=(m // tm, nj),
        in_specs=[pl.BlockSpec((tm, d), lambda i, j: (i, 0)),
                  pl.BlockSpec((1, d), lambda i, j: (0, 0)),
                  mod, mod,
                  pl.BlockSpec((d, tn), lambda i, j: (0, j)),
                  pl.BlockSpec((d, tn), lambda i, j: (0, j + nj))],
        out_specs=pl.BlockSpec((tm, tn), lambda i, j: (i, j)),
        out_shape=jax.ShapeDtypeStruct((m, D_FF), BF16),
        scratch_shapes=[pltpu.VMEM((tm, d), BF16)],
        compiler_params=_params(("arbitrary", "arbitrary")),
    )(x, g.reshape(1, d), sc, sh, w_bf, w_bf)


def _mmres_body(a_ref, w_ref, x_ref, g_ref, o_ref):
    o_ref[...] = x_ref[...] + g_ref[...] * _dot(a_ref[...], w_ref[...])


def _matmul_residual(a_bf, w_bf, x, gate, kind, tm, tn, tiles_per_seq):
    m, k = a_bf.shape
    n = w_bf.shape[1]
    return pl.pallas_call(
        _mmres_body,
        grid=(m // tm, n // tn),
        in_specs=[pl.BlockSpec((tm, k), lambda i, j: (i, 0)),
                  pl.BlockSpec((k, tn), lambda i, j: (0, j)),
                  pl.BlockSpec((tm, tn), lambda i, j: (i, j)),
                  _mod_spec(kind, tm, tn, tiles_per_seq, True)],
        out_specs=pl.BlockSpec((tm, tn), lambda i, j: (i, j)),
        out_shape=jax.ShapeDtypeStruct((m, n), F32),
        compiler_params=_params(("arbitrary", "arbitrary")),
    )(a_bf, w_bf, x, gate)


def _merge_body(oa_ref, or_ref, os_ref, wb_ref, ga_ref, gr_ref, gs_ref, o_ref):
    acc = None
    for n, (o_r, g_r) in enumerate(((oa_ref, ga_ref), (or_ref, gr_ref), (os_ref, gs_ref))):
        term = _sigmoid(g_r[...]) * _dot(o_r[...].astype(BF16), wb_ref[n])
        acc = term if acc is None else acc + term
    o_ref[...] = acc.astype(BF16)


def _branch_merge(o_att, o_ret, o_s5, z, wb_bf, tm, tn):
    m = z.shape[0]
    gate0 = OFF_GATE // tn
    per = D_MODEL // tn
    o_spec = pl.BlockSpec((tm, BRANCH_W), lambda i, j: (i, 0))

    def gate_spec(n):
        return pl.BlockSpec((tm, tn), lambda i, j: (i, gate0 + n * per + j))

    return pl.pallas_call(
        _merge_body,
        grid=(m // tm, per),
        in_specs=[o_spec, o_spec, o_spec,
                  pl.BlockSpec((N_BRANCH, BRANCH_W, tn), lambda i, j: (0, 0, j)),
                  gate_spec(0), gate_spec(1), gate_spec(2)],
        out_specs=pl.BlockSpec((tm, tn), lambda i, j: (i, j)),
        out_shape=jax.ShapeDtypeStruct((m, D_MODEL), BF16),
        compiler_params=_params(("arbitrary", "arbitrary")),
    )(o_att, o_ret, o_s5, wb_bf, z, z, z)


def _fnorm_body(x_ref, g_ref, o_ref):
    x = x_ref[...]
    o_ref[...] = x * lax.rsqrt(jnp.mean(x * x, axis=-1, keepdims=True) + NORM_EPS) * g_ref[...]


def _final_norm(x, g, tm):
    m, d = x.shape
    return pl.pallas_call(
        _fnorm_body,
        grid=(m // tm,),
        in_specs=[pl.BlockSpec((tm, d), lambda i: (i, 0)), pl.BlockSpec((1, d), lambda i: (0, 0))],
        out_specs=pl.BlockSpec((tm, d), lambda i: (i, 0)),
        out_shape=jax.ShapeDtypeStruct((m, d), F32),
        compiler_params=_params(("arbitrary",)),
    )(x, g.reshape(1, d))


def _bucket_np(dist):
    n = np.maximum(dist, 0)
    max_exact = REL_BUCKETS // 2
    nf = np.maximum(n, 1).astype(np.float32)
    large = max_exact + (np.log(nf / max_exact) / math.log(REL_MAX_DIST / max_exact)
                         * (REL_BUCKETS - max_exact)).astype(np.int32)
    large = np.minimum(large, REL_BUCKETS - 1)
    return np.where(n < max_exact, n, large).astype(np.int32)


def _bias_p_body(rb_ref, bk_ref, o_ref):
    h = pl.program_id(0)
    blk = MOBA_BLOCK
    for s in range(2):
        bk = bk_ref[s]
        acc = jnp.zeros((blk, blk), F32)
        for b in range(REL_BUCKETS):
            acc = jnp.where(bk == b, rb_ref[b, h], acc)
        if s == 0:
            rq = lax.broadcasted_iota(jnp.int32, (blk, blk), 0)
            rk = lax.broadcasted_iota(jnp.int32, (blk, blk), 1)
            acc = jnp.where(rk <= rq, acc, NEG)
        o_ref[s] = acc
    o_ref[2] = jnp.zeros((blk, blk), F32) + rb_ref[REL_BUCKETS - 1, h]


def _bias_tables_prompt(rel_bias):
    blk = MOBA_BLOCK
    rq = np.arange(blk)[:, None]
    rk = np.arange(blk)[None, :]
    buckets = np.stack([_bucket_np(rq - rk), _bucket_np(blk + rq - rk)])
    assert int(_bucket_np(np.array([blk + 1]))[0]) == REL_BUCKETS - 1
    return pl.pallas_call(
        _bias_p_body,
        grid=(ATT_HEADS,),
        in_specs=[pl.BlockSpec(memory_space=pltpu.SMEM),
                  pl.BlockSpec((2, blk, blk), lambda h: (0, 0, 0))],
        out_specs=pl.BlockSpec((None, 3, blk, blk), lambda h: (h, 0, 0, 0)),
        out_shape=jax.ShapeDtypeStruct((ATT_HEADS, 3, blk, blk), F32),
        compiler_params=_params(("arbitrary",)),
    )(rel_bias, jnp.asarray(buckets))


def _bias_s_body(rb_ref, bk_ref, o_ref):
    bk = bk_ref[...]
    acc = jnp.zeros(o_ref.shape, F32)
    for b in range(REL_BUCKETS):
        acc = jnp.where(bk == b, rb_ref[b:b + 1, :], acc)
    o_ref[...] = acc


def _bias_table_sample(rel_bias, past):
    buckets = _bucket_np(past - np.arange(past)).reshape(past, 1)
    return pl.pallas_call(
        _bias_s_body,
        out_shape=jax.ShapeDtypeStruct((past, ATT_HEADS), F32),
    )(rel_bias, jnp.asarray(buckets))


def _topk_select(s, allowed, idx, axis, n_cand):
    s = jnp.where(allowed, s, -jnp.inf)
    rank = jnp.zeros(s.shape, jnp.int32)
    for m in range(n_cand):
        sm = s[:, m:m + 1] if axis == 1 else s[m:m + 1, :]
        beats = (sm > s) | ((sm == s) & (idx > m))
        rank = rank + beats.astype(jnp.int32)
    return jnp.where(allowed & (rank < MOBA_TOPK), 1.0, 0.0)


def _moba_p_body(q_ref, k_ref, v_ref, tb_ref, o_ref, kbf_ref, vbf_ref, kmean_ref, *, nb):
    i = pl.program_id(2)
    blk = MOBA_BLOCK

    @pl.when(i == 0)
    def _():
        kf = k_ref[...]
        kbf_ref[...] = kf.astype(BF16)
        vbf_ref[...] = v_ref[...].astype(BF16)
        kmean_ref[...] = jnp.mean(kf.reshape(nb, blk, LANES), axis=1)

    q = q_ref[...]
    own = pl.multiple_of(i * blk, blk)
    outs = []
    for hh in range(2):
        cs = slice(hh * ATT_DH, (hh + 1) * ATT_DH)
        qh = q[:, cs]
        s = _dot_nt(qh, kmean_ref[:, cs], precision=HIGHEST)
        nidx = lax.broadcasted_iota(jnp.int32, s.shape, 1)
        sel = _topk_select(s, nidx < i, nidx, 1, nb)
        qs = (qh * ATT_DH ** -0.5).astype(BF16)

        lg = _dot_nt(qs, kbf_ref[pl.ds(own, blk), cs]) + tb_ref[hh, 0]
        m0 = jnp.max(lg, axis=-1, keepdims=True)
        p = jnp.exp(lg - m0)
        l0 = jnp.sum(p, axis=-1, keepdims=True)
        acc0 = _dot(p.astype(BF16), vbf_ref[pl.ds(own, blk), cs])

        def body(n, carry, qs=qs, sel=sel, nidx=nidx, cs=cs, hh=hh):
            m, l, acc = carry
            st = pl.multiple_of(n * blk, blk)
            bias = tb_ref[hh, jnp.where(n == i - 1, 1, 2)]
            seln = jnp.sum(jnp.where(nidx == n, sel, 0.0), axis=1, keepdims=True)
            lg = jnp.where(seln > 0.5, _dot_nt(qs, kbf_ref[pl.ds(st, blk), cs]) + bias, NEG)
            m_new = jnp.maximum(m, jnp.max(lg, axis=-1, keepdims=True))
            a = jnp.exp(m - m_new)
            p = jnp.exp(lg - m_new)
            l = a * l + jnp.sum(p, axis=-1, keepdims=True)
            acc = a * acc + _dot(p.astype(BF16), vbf_ref[pl.ds(st, blk), cs])
            return m_new, l, acc

        _, l, acc = lax.fori_loop(0, i, body, (m0, l0, acc0))
        outs.append(acc / l)
    o_ref[...] = jnp.concatenate(outs, axis=1).astype(BF16)


def _moba_prompt(z, bias_tab, n_seq, t):
    blk = MOBA_BLOCK
    nb = t // blk
    hp = ATT_HEADS // 2
    kc, vc = OFF_AK // LANES, OFF_AV // LANES
    return pl.pallas_call(
        functools.partial(_moba_p_body, nb=nb),
        grid=(n_seq, hp, nb),
        in_specs=[pl.BlockSpec((blk, LANES), lambda b, h, i: (b * nb + i, h)),
                  pl.BlockSpec((t, LANES), lambda b, h, i: (b, kc + h)),
                  pl.BlockSpec((t, LANES), lambda b, h, i: (b, vc + h)),
                  pl.BlockSpec((2, 3, blk, blk), lambda b, h, i: (h, 0, 0, 0))],
        out_specs=pl.BlockSpec((blk, LANES), lambda b, h, i: (b * nb + i, h)),
        out_shape=jax.ShapeDtypeStruct((n_seq * t, ATT_W), BF16),
        scratch_shapes=[pltpu.VMEM((t, LANES), BF16), pltpu.VMEM((t, LANES), BF16),
                        pltpu.VMEM((nb, LANES), F32)],
        compiler_params=_params(("arbitrary", "arbitrary", "arbitrary")),
    )(z, z, z, bias_tab)


PAGES_PER_STEP = 4
BLOCKS_PER_STEP = PAGES_PER_STEP * PAGE_SIZE // MOBA_BLOCK


def _head_expand(dtype):
    row = lax.broadcasted_iota(jnp.int32, (ATT_HEADS, ATT_W), 0)
    lane = lax.broadcasted_iota(jnp.int32, (ATT_HEADS, ATT_W), 1)
    return jnp.where(lane // ATT_DH == row, 1.0, 0.0).astype(dtype)


def _head_reduce():
    row = lax.broadcasted_iota(jnp.int32, (ATT_W, ATT_HEADS), 0)
    col = lax.broadcasted_iota(jnp.int32, (ATT_W, ATT_HEADS), 1)
    return jnp.where(row // ATT_DH == col, 1.0, 0.0).astype(F32)


def _moba_s_body(pt_ref, k0, k1, k2, k3, v0, v1, v2, v3, qbd_ref, bias_ref, q_ref, kn_ref, vn_ref,
                 rb_ref, o_ref, m_s, l_s, acc_s, ks_s, *, n_steps):
    del pt_ref
    g = pl.program_id(1)
    nblk = n_steps * BLOCKS_PER_STEP
    expand_bf = _head_expand(BF16)
    kpages, vpages = (k0, k1, k2, k3), (v0, v1, v2, v3)
    per_blk = MOBA_BLOCK // PAGE_SIZE
    for b in range(BLOCKS_PER_STEP):
        kb = jnp.concatenate([kpages[b * per_blk + j][...] for j in range(per_blk)], axis=0)
        vb = jnp.concatenate([vpages[b * per_blk + j][...] for j in range(per_blk)], axis=0)
        lg = _dot(kb.astype(BF16), qbd_ref[...]) + bias_ref[b * MOBA_BLOCK:(b + 1) * MOBA_BLOCK, :]
        mb = jnp.max(lg, axis=0, keepdims=True)
        p = jnp.exp(lg - mb)
        pe = _dot(p.astype(BF16), expand_bf)
        row = g * BLOCKS_PER_STEP + b
        m_s[pl.ds(row, 1), :] = mb
        l_s[pl.ds(row, 1), :] = jnp.sum(p, axis=0, keepdims=True)
        acc_s[pl.ds(row, 1), :] = jnp.sum(pe * vb, axis=0, keepdims=True)
        ks_s[pl.ds(row, 1), :] = jnp.sum(kb, axis=0, keepdims=True)

    @pl.when(g == n_steps - 1)
    def _():
        q = q_ref[...]
        reduce_f = _head_reduce()
        expand_f = _head_expand(F32)
        s = jnp.dot(ks_s[...] * q, reduce_f, precision=HIGHEST,
                    preferred_element_type=F32) * (1.0 / MOBA_BLOCK)
        nidx = lax.broadcasted_iota(jnp.int32, s.shape, 0)
        sel = _topk_select(s, nidx >= 0, nidx, 0, nblk) > 0.5
        ln = jnp.dot((q * ATT_DH ** -0.5) * kn_ref[...], reduce_f, precision=HIGHEST,
                     preferred_element_type=F32) + rb_ref[0:1, :]
        mm = jnp.where(sel, m_s[...], NEG)
        mx = jnp.maximum(jnp.max(mm, axis=0, keepdims=True), ln)
        w = jnp.where(sel, jnp.exp(mm - mx), 0.0)
        wn = jnp.exp(ln - mx)
        den = jnp.sum(w * l_s[...], axis=0, keepdims=True) + wn
        w_e = jnp.dot(w / den, expand_f, precision=HIGHEST, preferred_element_type=F32)
        wn_e = jnp.dot(wn / den, expand_f, precision=HIGHEST, preferred_element_type=F32)
        o_ref[...] = jnp.sum(w_e * acc_s[...], axis=0, keepdims=True) + wn_e * vn_ref[...]


def _moba_sample(z_s, pool_k, pool_v, page_table, bias_tab, rel_bias):
    n_seq, n_pages = page_table.shape
    n_pool = pool_k.shape[0]
    n_steps = n_pages // PAGES_PER_STEP
    nblk = n_steps * BLOCKS_PER_STEP
    kp = pool_k.reshape(n_pool, PAGE_SIZE, ATT_W)
    vp = pool_v.reshape(n_pool, PAGE_SIZE, ATT_W)
    q = z_s[:, OFF_AQ:OFF_AQ + ATT_W]
    head_of_lane = jnp.arange(ATT_W) // ATT_DH
    qbd = jnp.where(head_of_lane[None, :, None] == jnp.arange(ATT_HEADS)[None, None, :],
                    (q * ATT_DH ** -0.5)[:, :, None], 0.0).astype(BF16)
    row3 = lambda a: a.reshape(n_seq, 1, ATT_W)

    def page_spec(j):
        return pl.BlockSpec((None, PAGE_SIZE, ATT_W),
                            lambda s, g, pt: (pt[s, g * PAGES_PER_STEP + j], 0, 0))

    row_spec = pl.BlockSpec((None, 1, ATT_W), lambda s, g, pt: (s, 0, 0))
    grid_spec = pltpu.PrefetchScalarGridSpec(
        num_scalar_prefetch=1,
        grid=(n_seq, n_steps),
        in_specs=[page_spec(j) for j in range(PAGES_PER_STEP)] * 2 + [
            pl.BlockSpec((None, ATT_W, ATT_HEADS), lambda s, g, pt: (s, 0, 0)),
            pl.BlockSpec((BLOCKS_PER_STEP * MOBA_BLOCK, ATT_HEADS), lambda s, g, pt: (g, 0)),
            row_spec, row_spec, row_spec,
            pl.BlockSpec((REL_BUCKETS, ATT_HEADS), lambda s, g, pt: (0, 0))],
        out_specs=row_spec,
        scratch_shapes=[pltpu.VMEM((nblk, ATT_HEADS), F32), pltpu.VMEM((nblk, ATT_HEADS), F32),
                        pltpu.VMEM((nblk, ATT_W), F32), pltpu.VMEM((nblk, ATT_W), F32)],
    )
    out = pl.pallas_call(
        functools.partial(_moba_s_body, n_steps=n_steps),
        grid_spec=grid_spec,
        out_shape=jax.ShapeDtypeStruct((n_seq, 1, ATT_W), F32),
        compiler_params=_params(("arbitrary", "arbitrary")),
    )(page_table, kp, kp, kp, kp, vp, vp, vp, vp, qbd, bias_tab, row3(q),
      row3(z_s[:, OFF_AK:OFF_AK + ATT_W]), row3(z_s[:, OFF_AV:OFF_AV + ATT_W]), rel_bias)
    return out.reshape(n_seq, ATT_W)


def _rope_body(inv_ref, sgn_ref, cos_ref, sin_ref, *, pos0, step, rows):
    r = lax.broadcasted_iota(jnp.int32, (rows, LANES), 0) + pl.program_id(0) * rows
    ang = (pos0 + step * r).astype(F32) * inv_ref[...]
    cos_ref[...] = jnp.cos(ang)
    sin_ref[...] = jnp.sin(ang) * sgn_ref[...]


def _rope_tables(n_rows, pos0, step):
    half = RET_DK // 2
    inv = 1.0 / (ROPE_BASE ** jnp.linspace(0.0, 1.0, half))
    inv_row = jnp.tile(inv, LANES // half).reshape(1, LANES).astype(F32)
    sgn = np.where((np.arange(LANES) % RET_DK) < half, -1.0, 1.0).astype(np.float32).reshape(1, LANES)
    rows = min(n_rows, 512)
    spec = pl.BlockSpec((rows, LANES), lambda i: (i, 0))
    cst = pl.BlockSpec((1, LANES), lambda i: (0, 0))
    return pl.pallas_call(
        functools.partial(_rope_body, pos0=pos0, step=step, rows=rows),
        grid=(n_rows // rows,),
        in_specs=[cst, cst],
        out_specs=[spec, spec],
        out_shape=[jax.ShapeDtypeStruct((n_rows, LANES), F32)] * 2,
        compiler_params=_params(("arbitrary",)),
    )(inv_row, jnp.asarray(sgn))


def _rotary128(x, cos, sin_signed):
    half = RET_DK // 2
    lane = lax.broadcasted_iota(jnp.int32, x.shape, 1)
    partner = jnp.where((lane % RET_DK) < half,
                        pltpu.roll(x, LANES - half, 1), pltpu.roll(x, half, 1))
    return x * cos + partner * sin_signed


def _groupnorm_gate(o, g):
    mu = jnp.mean(o, axis=-1, keepdims=True)
    var = jnp.mean((o - mu) ** 2, axis=-1, keepdims=True)
    return _silu(g) * ((o - mu) * lax.rsqrt(var + GN_EPS))


def _ret_log_decay():
    return jnp.log(1.0 - 2.0 ** (-5.0 - jnp.arange(RET_HEADS, dtype=F32)))


def _ret_p_body(lg_ref, q_ref, k_ref, v_ref, g_ref, cos_ref, sin_ref, o_ref, st_ref, *, chunk):
    hp = pl.program_id(1)
    c = pl.program_id(2)

    @pl.when(c == 0)
    def _():
        st_ref[...] = jnp.zeros(st_ref.shape, F32)

    cos, sin = cos_ref[...], sin_ref[...]
    q = _rotary128(q_ref[...], cos, sin)
    k = _rotary128(k_ref[...], cos, sin) * RET_DK ** -0.5
    ii = lax.broadcasted_iota(jnp.int32, (chunk, chunk), 0)
    jj = lax.broadcasted_iota(jnp.int32, (chunk, chunk), 1)
    diff = (ii - jj).astype(F32)
    ri = lax.broadcasted_iota(jnp.int32, (chunk, 1), 0).astype(F32)
    for hh in range(2):
        lgh = lg_ref[hp * 2 + hh]
        dmask = jnp.where(diff >= 0, jnp.exp(lgh * jnp.maximum(diff, 0.0)), 0.0)
        qh = q[:, hh * RET_DK:(hh + 1) * RET_DK].astype(BF16)
        kh = k[:, hh * RET_DK:(hh + 1) * RET_DK]
        vh = v_ref[:, hh * RET_DV:(hh + 1) * RET_DV].astype(BF16)
        s0 = st_ref[hh]
        a = _dot_nt(qh, kh.astype(BF16)) * dmask
        inner = _dot(a.astype(BF16), vh)
        cross = _dot(qh, s0.astype(BF16)) * jnp.exp(lgh * (ri + 1.0))
        kdec = (kh * jnp.exp(lgh * (chunk - 1.0 - ri))).astype(BF16)
        st_ref[hh] = jnp.exp(lgh * chunk + jnp.zeros((1, 1), F32)) * s0 + _dot_tn(kdec, vh)
        gh = g_ref[:, hh * RET_DV:(hh + 1) * RET_DV]
        o_ref[:, hh * RET_DV:(hh + 1) * RET_DV] = _groupnorm_gate(inner + cross, gh).astype(BF16)


def _ret_prompt(z, cos_tab, sin_tab, log_decay, n_seq, t):
    chunk = RET_CHUNK
    nc = t // chunk
    hp = RET_HEADS // 2
    qc, kc = OFF_RQ // LANES, OFF_RK // LANES
    vc, gc = OFF_RV // (2 * RET_DV), OFF_RG // (2 * RET_DV)
    return pl.pallas_call(
        functools.partial(_ret_p_body, chunk=chunk),
        grid=(n_seq, hp, nc),
        in_specs=[pl.BlockSpec(memory_space=pltpu.SMEM),
                  pl.BlockSpec((chunk, LANES), lambda b, h, c: (b * nc + c, qc + h)),
                  pl.BlockSpec((chunk, LANES), lambda b, h, c: (b * nc + c, kc + h)),
                  pl.BlockSpec((chunk, 2 * RET_DV), lambda b, h, c: (b * nc + c, vc + h)),
                  pl.BlockSpec((chunk, 2 * RET_DV), lambda b, h, c: (b * nc + c, gc + h)),
                  pl.BlockSpec((chunk, LANES), lambda b, h, c: (c, 0)),
                  pl.BlockSpec((chunk, LANES), lambda b, h, c: (c, 0))],
        out_specs=[pl.BlockSpec((chunk, 2 * RET_DV), lambda b, h, c: (b * nc + c, h)),
                   pl.BlockSpec((None, 2, RET_DK, RET_DV), lambda b, h, c: (b, h, 0, 0))],
        out_shape=[jax.ShapeDtypeStruct((n_seq * t, RET_V_W), BF16),
                   jax.ShapeDtypeStruct((n_seq, RET_HEADS, RET_DK, RET_DV), F32)],
        compiler_params=_params(("arbitrary", "arbitrary", "arbitrary")),
    )(log_decay, z, z, z, z, cos_tab, sin_tab)


def _ret_s_prep_body(q_ref, k_ref, cos_ref, sin_ref, qo_ref, ko_ref):
    cos, sin = cos_ref[0:1, :], sin_ref[0:1, :]
    for j in range(RET_QK_W // LANES):
        sl = slice(j * LANES, (j + 1) * LANES)
        qo_ref[:, sl] = _rotary128(q_ref[:, sl], cos, sin)
        ko_ref[:, sl] = _rotary128(k_ref[:, sl], cos, sin) * RET_DK ** -0.5


def _ret_s_body(lg_ref, q_ref, k_ref, v_ref, g_ref, s0_ref, o_ref, sn_ref):
    gam = jnp.exp(lg_ref[pl.program_id(1)] + jnp.zeros((1, 1, LANES), F32))
    q, k, v, s0 = q_ref[...], k_ref[...], v_ref[...], s0_ref[...]
    o = jnp.sum(q * s0, axis=1, keepdims=True) * gam + jnp.sum(q * k, axis=1, keepdims=True) * v
    sn_ref[...] = gam * s0 + k * v
    o_ref[...] = _groupnorm_gate(o, g_ref[...])


def _ret_sample(z_s, state, cos_tab, sin_tab, log_decay):
    n_seq = z_s.shape[0]
    tab = pl.BlockSpec((SUBLANES, LANES), lambda i: (0, 0))
    q_rot, k_rot = pl.pallas_call(
        _ret_s_prep_body,
        grid=(1,),
        in_specs=[pl.BlockSpec((n_seq, RET_QK_W), lambda i: (0, OFF_RQ // RET_QK_W)),
                  pl.BlockSpec((n_seq, RET_QK_W), lambda i: (0, OFF_RK // RET_QK_W)), tab, tab],
        out_specs=[pl.BlockSpec((n_seq, RET_QK_W), lambda i: (0, 0))] * 2,
        out_shape=[jax.ShapeDtypeStruct((n_seq, RET_QK_W), F32)] * 2,
        compiler_params=_params(("arbitrary",)),
    )(z_s, z_s, cos_tab, sin_tab)
    col = lambda a: a.reshape(n_seq, RET_HEADS, RET_DK, 1)
    row = lambda a: a.reshape(n_seq, RET_HEADS, 1, RET_DV)
    bt = 16
    col_spec = pl.BlockSpec((bt, None, RET_DK, 1), lambda i, h: (i, h, 0, 0))
    row_spec = pl.BlockSpec((bt, None, 1, RET_DV), lambda i, h: (i, h, 0, 0))
    st_spec = pl.BlockSpec((bt, None, RET_DK, RET_DV), lambda i, h: (i, h, 0, 0))
    o, s_new = pl.pallas_call(
        _ret_s_body,
        grid=(n_seq // bt, RET_HEADS),
        in_specs=[pl.BlockSpec(memory_space=pltpu.SMEM), col_spec, col_spec, row_spec, row_spec, st_spec],
        out_specs=[row_spec, st_spec],
        out_shape=[jax.ShapeDtypeStruct((n_seq, RET_HEADS, 1, RET_DV), F32),
                   jax.ShapeDtypeStruct((n_seq, RET_HEADS, RET_DK, RET_DV), F32)],
        compiler_params=_params(("arbitrary", "arbitrary")),
    )(log_decay, col(q_rot), col(k_rot), row(z_s[:, OFF_RV:OFF_RV + RET_V_W]),
      row(z_s[:, OFF_RG:OFF_RG + RET_V_W]), state)
    return o.reshape(n_seq, RET_V_W), s_new


def _s5_prep_body(are_ref, aim_ref, ldt_ref, bre_ref, bim_ref, pwr_ref, pwi_ref, bbr_ref, bbi_ref):
    ar, ai = are_ref[...], aim_ref[...]
    dt = jnp.exp(ldt_ref[...])
    kk = (lax.broadcasted_iota(jnp.int32, pwr_ref.shape, 1) + 1).astype(F32)
    mag = jnp.exp(ar * dt * kk)
    ang = ai * dt * kk
    pwr_ref[...] = mag * jnp.cos(ang)
    pwi_ref[...] = mag * jnp.sin(ang)
    mag1 = jnp.exp(ar * dt)
    nr = mag1 * jnp.cos(ai * dt) - 1.0
    ni = mag1 * jnp.sin(ai * dt)
    den = ar * ar + ai * ai
    cr = (nr * ar + ni * ai) / den
    ci = (ni * ar - nr * ai) / den
    bre, bim = bre_ref[...], bim_ref[...]
    bbr_ref[...] = cr * bre - ci * bim
    bbi_ref[...] = cr * bim + ci * bre


def _s5_prepare(a_re, a_im, log_dt, b_re, b_im, c_re, c_im, d):
    g, p, c = S5_GROUPS, S5_STATE, S5_GROUP
    g3 = lambda a: a.reshape(g, 1, p)
    ldt = jnp.broadcast_to(log_dt.reshape(g, 1, 1), (g, 1, p))
    pwr, pwi, bbr, bbi = pl.pallas_call(
        _s5_prep_body,
        out_shape=[jax.ShapeDtypeStruct((g, SUBLANES, p), F32)] * 2
        + [jax.ShapeDtypeStruct((g, c, p), F32)] * 2,
    )(g3(a_re), g3(a_im), ldt, b_re.transpose(0, 2, 1), b_im.transpose(0, 2, 1))
    eye = jnp.eye(S5_GT, dtype=F32)
    nt = S5_NT

    def in_tile(bb):
        return jnp.einsum("tgcp,gh->tgchp", bb.reshape(nt, S5_GT, c, p), eye).reshape(nt, S5_GT * c, S5_GT * p)

    def out_tile(cc):
        return jnp.einsum("tgcp,gh->tgphc", cc.reshape(nt, S5_GT, c, p), eye).reshape(nt, S5_GT * p, S5_GT * c)

    b_tile = jnp.concatenate([in_tile(bbr), in_tile(bbi)], axis=2).astype(BF16)
    c_tile = jnp.concatenate([out_tile(c_re), out_tile(-c_im)], axis=1).astype(BF16)
    d_tile = d.reshape(nt, 1, S5_GT * c)

    def pw_tile(pw):
        return pw.reshape(nt, S5_GT, SUBLANES, p).transpose(0, 2, 1, 3).reshape(nt, SUBLANES, S5_GT * p)

    pr, pi = pw_tile(pwr), pw_tile(pwi)
    rows = jnp.arange(SUBLANES)[None, :, None]
    slabs = []
    for shift in (1, 2, 4):
        keep = rows >= shift
        slabs += [jnp.where(keep, pr[:, shift - 1:shift, :], 0.0), jnp.where(keep, pi[:, shift - 1:shift, :], 0.0)]
    scan_c = jnp.stack(slabs + [pr, pi], axis=1)
    lam1 = jnp.stack([pr[:, 0:1, :], pi[:, 0:1, :]], axis=1)
    return b_tile, c_tile, d_tile, scan_c, lam1


def _gelu_tanh(y):
    return 0.5 * y * (1.0 + jnp.tanh(math.sqrt(2.0 / math.pi) * (y + 0.044715 * (y * y * y))))


def _s5_p_body(u_ref, bt_ref, ct_ref, d_ref, sc_ref, z_ref, st_ref, x_ref, carry_ref, *, tc):
    ns = S5_TILE_STATES

    @pl.when(pl.program_id(2) == 0)
    def _():
        carry_ref[...] = jnp.zeros(carry_ref.shape, F32)

    u = u_ref[...]
    x_ref[...] = _dot(u.astype(BF16), bt_ref[...])

    def tile(t, carry):
        cr, ci = carry
        st = pl.multiple_of(t * SUBLANES, SUBLANES)
        xr = x_ref[pl.ds(st, SUBLANES), 0:ns]
        xi = x_ref[pl.ds(st, SUBLANES), ns:2 * ns]
        for s, shift in enumerate((1, 2, 4)):
            ar, ai = sc_ref[2 * s], sc_ref[2 * s + 1]
            sr, si = pltpu.roll(xr, shift, 0), pltpu.roll(xi, shift, 0)
            xr, xi = xr + ar * sr - ai * si, xi + ar * si + ai * sr
        pr, pi = sc_ref[6], sc_ref[7]
        xr, xi = xr + pr * cr - pi * ci, xi + pr * ci + pi * cr
        x_ref[pl.ds(st, SUBLANES), 0:ns] = xr
        x_ref[pl.ds(st, SUBLANES), ns:2 * ns] = xi
        return xr[SUBLANES - 1:SUBLANES, :], xi[SUBLANES - 1:SUBLANES, :]

    cr, ci = lax.fori_loop(0, tc // SUBLANES, tile, (carry_ref[:, 0:ns], carry_ref[:, ns:2 * ns]))
    last = jnp.concatenate([cr, ci], axis=1)
    carry_ref[...] = last
    st_ref[...] = last
    y = _dot(x_ref[...].astype(BF16), ct_ref[...]) + d_ref[...] * u
    z_ref[...] = _gelu_tanh(y)


def _s5_prompt(z, tiles, n_seq, t):
    b_tile, c_tile, d_tile, scan_c, _ = tiles
    tc = 256
    nc = t // tc
    uc = OFF_SU // LANES
    ns2 = 2 * S5_TILE_STATES
    zs, st = pl.pallas_call(
        functools.partial(_s5_p_body, tc=tc),
        grid=(n_seq, S5_NT, nc),
        in_specs=[pl.BlockSpec((tc, LANES), lambda b, g, c: (b * nc + c, uc + g)),
                  pl.BlockSpec((None, LANES, ns2), lambda b, g, c: (g, 0, 0)),
                  pl.BlockSpec((None, ns2, LANES), lambda b, g, c: (g, 0, 0)),
                  pl.BlockSpec((None, 1, LANES), lambda b, g, c: (g, 0, 0)),
                  pl.BlockSpec((None, 8, SUBLANES, S5_TILE_STATES), lambda b, g, c: (g, 0, 0, 0))],
        out_specs=[pl.BlockSpec((tc, LANES), lambda b, g, c: (b * nc + c, g)),
                   pl.BlockSpec((None, None, 1, ns2), lambda b, g, c: (b, g, 0, 0))],
        out_shape=[jax.ShapeDtypeStruct((n_seq * t, S5_W), F32),
                   jax.ShapeDtypeStruct((n_seq, S5_NT, 1, ns2), F32)],
        scratch_shapes=[pltpu.VMEM((tc, ns2), F32), pltpu.VMEM((1, ns2), F32)],
        compiler_params=_params(("arbitrary", "arbitrary", "arbitrary")),
    )(z, b_tile, c_tile, d_tile, scan_c)
    s_re = st[:, :, 0, :S5_TILE_STATES].reshape(n_seq, S5_GROUPS, S5_STATE)
    s_im = st[:, :, 0, S5_TILE_STATES:].reshape(n_seq, S5_GROUPS, S5_STATE)
    return zs, s_re, s_im


def _s5_s_body(u_ref, bt_ref, ct_ref, d_ref, l1_ref, x0_ref, z_ref, xn_ref):
    ns = S5_TILE_STATES
    u = u_ref[...]
    bu = _dot(u.astype(BF16), bt_ref[...])
    lr, li = l1_ref[0], l1_ref[1]
    x0r, x0i = x0_ref[:, 0:ns], x0_ref[:, ns:2 * ns]
    x = jnp.concatenate([bu[:, 0:ns] + lr * x0r - li * x0i, bu[:, ns:2 * ns] + lr * x0i + li * x0r], axis=1)
    xn_ref[...] = x
    z_ref[...] = _gelu_tanh(_dot(x.astype(BF16), ct_ref[...]) + d_ref[...] * u)


def _s5_sample(z_s, tiles, x0_re, x0_im):
    b_tile, c_tile, d_tile, _, lam1 = tiles
    n_seq = z_s.shape[0]
    uc = OFF_SU // LANES
    ns2 = 2 * S5_TILE_STATES

    def to_tiles(a):
        return a.reshape(n_seq, S5_NT, S5_TILE_STATES).transpose(1, 0, 2)

    x0 = jnp.concatenate([to_tiles(x0_re), to_tiles(x0_im)], axis=2)
    zs, xn = pl.pallas_call(
        _s5_s_body,
        grid=(S5_NT,),
        in_specs=[pl.BlockSpec((n_seq, LANES), lambda g: (0, uc + g)),
                  pl.BlockSpec((None, LANES, ns2), lambda g: (g, 0, 0)),
                  pl.BlockSpec((None, ns2, LANES), lambda g: (g, 0, 0)),
                  pl.BlockSpec((None, 1, LANES), lambda g: (g, 0, 0)),
                  pl.BlockSpec((None, 2, 1, S5_TILE_STATES), lambda g: (g, 0, 0, 0)),
                  pl.BlockSpec((None, n_seq, ns2), lambda g: (g, 0, 0))],
        out_specs=[pl.BlockSpec((n_seq, LANES), lambda g: (0, g)),
                   pl.BlockSpec((None, n_seq, ns2), lambda g: (g, 0, 0))],
        out_shape=[jax.ShapeDtypeStruct((n_seq, S5_W), F32),
                   jax.ShapeDtypeStruct((S5_NT, n_seq, ns2), F32)],
        compiler_params=_params(("arbitrary",)),
    )(z_s, b_tile, c_tile, d_tile, lam1, x0)

    def from_tiles(a):
        return a.transpose(1, 0, 2).reshape(n_seq, S5_GROUPS, S5_STATE)

    return zs, from_tiles(xn[:, :, :S5_TILE_STATES]), from_tiles(xn[:, :, S5_TILE_STATES:])


def _glu_body(zf_ref, zc_ref, w_ref, o_ref):
    o_ref[...] = (zc_ref[...] * _sigmoid(_dot(zf_ref[...].astype(BF16), w_ref[...]))).astype(BF16)


def _half_glu(zs, w_bf, tm, tn):
    m, w = zs.shape
    return pl.pallas_call(
        _glu_body,
        grid=(m // tm, w // tn),
        in_specs=[pl.BlockSpec((tm, w), lambda i, j: (i, 0)),
                  pl.BlockSpec((tm, tn), lambda i, j: (i, j)),
                  pl.BlockSpec((w, tn), lambda i, j: (0, j))],
        out_specs=pl.BlockSpec((tm, tn), lambda i, j: (i, j)),
        out_shape=jax.ShapeDtypeStruct((m, w), BF16),
        compiler_params=_params(("arbitrary", "arbitrary")),
    )(zs, zs, w_bf)


def _layer(x, mods, lw, mix, kind, tm, tiles_per_seq):
    norm1, norm2, w_in, w_branch, w_out, w_ffn_in, w_ffn_out = lw
    sh1, sc1, g1, sh2, sc2, g2 = mods
    z = _norm_mod_matmul(x, norm1, sc1, sh1, w_in, kind, tm, 512, tiles_per_seq)
    (o_att, o_ret, o_s5), state = mix(z)
    merged = _branch_merge(o_att, o_ret, o_s5, z, w_branch, tm, 512)
    x = _matmul_residual(merged, w_out, x, g1, kind, tm, 512, tiles_per_seq)
    act = _norm_mod_ffn_in(x, norm2, sc2, sh2, w_ffn_in, kind, tm, 512, tiles_per_seq)
    tm_out = min(tm, 512)
    x = _matmul_residual(act, w_ffn_out, x, g2, kind, tm_out, 512, tiles_per_seq * (tm // tm_out))
    return x, z, state


def kernel(x_prompt, x_sample, c_prompt, c_sample, cache_k, cache_v, page_table, state_ret, state_s5_re, state_s5_im, rel_bias, norm1_g, norm2_g, w_ada, b_ada, w_in, s5_a_re, s5_a_im, s5_log_dt, s5_b_re, s5_b_im, s5_c_re, s5_c_im, s5_d, w_glu, w_branch, w_out, w_ffn_in, w_ffn_out, final_g):
    n_seq, t, d = x_prompt.shape
    n_dec = x_sample.shape[0]
    depth = w_in.shape[0]
    past = page_table.shape[1] * PAGE_SIZE
    xp = x_prompt.reshape(n_seq * t, d)
    xs = x_sample.reshape(n_dec, d)
    tm_p = 1024
    tiles_per_seq = t // tm_p

    bias_p = _bias_tables_prompt(rel_bias)
    bias_s = _bias_table_sample(rel_bias, past)
    cos_p, sin_p = _rope_tables(t, 0, 1)
    cos_s, sin_s = _rope_tables(SUBLANES, past, 0)
    log_decay = _ret_log_decay()
    pad = (-(n_seq + n_dec)) % SUBLANES
    c_all = jnp.concatenate([c_prompt, c_sample, jnp.zeros((pad, d), F32)], axis=0)

    new_p, new_s = [], []
    for l in range(depth):
        mod = _ada(c_all, w_ada[l].astype(BF16), b_ada[l])
        mod_p = mod[:n_seq].reshape(n_seq, 6, 1, d)
        mod_s = mod[n_seq:n_seq + n_dec].reshape(n_dec, 6, d)
        mods_p = tuple(mod_p[:, i] for i in range(6))
        mods_s = tuple(mod_s[:, i] for i in range(6))
        lw = (norm1_g[l], norm2_g[l], w_in[l].astype(BF16), w_branch[l].astype(BF16), w_out[l].astype(BF16),
              w_ffn_in[l].astype(BF16), w_ffn_out[l].astype(BF16))
        glu_w = w_glu[l].astype(BF16)
        tiles = _s5_prepare(s5_a_re[l], s5_a_im[l], s5_log_dt[l], s5_b_re[l], s5_b_im[l],
                            s5_c_re[l], s5_c_im[l], s5_d[l])

        def mix_p(z):
            o_att = _moba_prompt(z, bias_p, n_seq, t)
            o_ret, s_ret = _ret_prompt(z, cos_p, sin_p, log_decay, n_seq, t)
            zs5, s_re, s_im = _s5_prompt(z, tiles, n_seq, t)
            return (o_att, o_ret, _half_glu(zs5, glu_w, tm_p, 512)), (s_ret, s_re, s_im)

        def mix_s(z):
            o_att = _moba_sample(z, cache_k[l], cache_v[l], page_table, bias_s, rel_bias)
            o_ret, s_ret = _ret_sample(z, state_ret[l], cos_s, sin_s, log_decay)
            zs5, s_re, s_im = _s5_sample(z, tiles, state_s5_re[l], state_s5_im[l])
            return (o_att, o_ret, _half_glu(zs5, glu_w, n_dec, 512)), (s_ret, s_re, s_im)

        xp, zp, st_p = _layer(xp, mods_p, lw, mix_p, "prompt", tm_p, tiles_per_seq)
        xs, zs, st_s = _layer(xs, mods_s, lw, mix_s, "sample", n_dec, 1)
        kv = lambda z, n, tt, off: z[:, off:off + ATT_W].reshape(n, tt, ATT_HEADS, ATT_DH)
        new_p.append((kv(zp, n_seq, t, OFF_AK), kv(zp, n_seq, t, OFF_AV)) + st_p)
        new_s.append((kv(zs, n_dec, 1, OFF_AK), kv(zs, n_dec, 1, OFF_AV)) + st_s)

    y_prompt = _final_norm(xp, final_g, tm_p).reshape(n_seq, t, d)
    y_sample = _final_norm(xs, final_g, n_dec).reshape(n_dec, 1, d)
    outs_p = [jnp.stack(a) for a in zip(*new_p)]
    outs_s = [jnp.stack(a) for a in zip(*new_s)]
    return (y_prompt, y_sample, *outs_p, *outs_s)
```

```python
import functools
import math

import numpy as np
import jax
import jax.numpy as jnp
from jax import lax
from jax.experimental import pallas as pl
from jax.experimental.pallas import tpu as pltpu

F32 = jnp.float32
BF16 = jnp.bfloat16
HIGHEST = lax.Precision.HIGHEST

D_MODEL = 2048
PAGE_SIZE = 128
ATT_HEADS = 16
ATT_DH = 64
ATT_W = ATT_HEADS * ATT_DH
MOBA_BLOCK = 256
MOBA_TOPK = 3
REL_BUCKETS = 32
REL_MAX_DIST = 128
RET_HEADS = 8
RET_DK = 64
RET_DV = 128
RET_QK_W = RET_HEADS * RET_DK
RET_V_W = RET_HEADS * RET_DV
RET_CHUNK = 128
ROPE_BASE = 10000.0
S5_W = 1024
S5_GROUP = 16
S5_GROUPS = S5_W // S5_GROUP
S5_STATE = 64
N_BRANCH = 3
BRANCH_W = 1024
D_FF = ((8 * D_MODEL + 3 * 256 - 1) // (3 * 256)) * 256
NORM_EPS = 1e-6
GN_EPS = 1e-5
D_IN = 3 * ATT_W + 2 * RET_QK_W + 2 * RET_V_W + S5_W + N_BRANCH * D_MODEL

OFF_AQ, OFF_AK, OFF_AV = 0, ATT_W, 2 * ATT_W
OFF_RQ = 3 * ATT_W
OFF_RK = OFF_RQ + RET_QK_W
OFF_RV = OFF_RK + RET_QK_W
OFF_RG = OFF_RV + RET_V_W
OFF_SU = OFF_RG + RET_V_W
OFF_GATE = OFF_SU + S5_W

LANES = 128
SUBLANES = 8
VMEM_LIMIT = 56 * 1024 * 1024
NEG = -1e30
MOBA_KEY_GROUP = 4
S5_GT = 8
S5_TILE_STATES = S5_GT * S5_STATE
S5_NT = S5_GROUPS // S5_GT


def _params(sem):
    return pltpu.CompilerParams(dimension_semantics=sem, vmem_limit_bytes=VMEM_LIMIT)


def _sigmoid(x):
    return 1.0 / (1.0 + jnp.exp(-x))


def _silu(x):
    return x * _sigmoid(x)


def _dot(a, b):
    return jnp.dot(a, b, preferred_element_type=F32)


def _dot_nt(a, b, precision=None):
    return lax.dot_general(a, b, (((1,), (1,)), ((), ())), precision=precision,
                           preferred_element_type=F32)


def _dot_tn(a, b):
    return lax.dot_general(a, b, (((0,), (0,)), ((), ())), preferred_element_type=F32)


def _mod_spec(kind, tm, tn, tiles_per_seq, col_blocked):
    if kind == "prompt":
        if col_blocked:
            return pl.BlockSpec((None, 1, tn), lambda i, j: (i // tiles_per_seq, 0, j))
        return pl.BlockSpec((None, 1, tn), lambda i, j: (i // tiles_per_seq, 0, 0))
    if col_blocked:
        return pl.BlockSpec((tm, tn), lambda i, j: (i, j))
    return pl.BlockSpec((tm, tn), lambda i, j: (i, 0))


def _ada_body(c_ref, w_ref, b_ref, o_ref):
    a = _silu(c_ref[...]).astype(BF16)
    o_ref[...] = _dot(a, w_ref[...]) + b_ref[...]


def _ada(c_all, w_bf, b):
    m, d = c_all.shape
    n = w_bf.shape[1]
    tn = 1024
    return pl.pallas_call(
        _ada_body,
        grid=(n // tn,),
        in_specs=[pl.BlockSpec((m, d), lambda j: (0, 0)),
                  pl.BlockSpec((d, tn), lambda j: (0, j)),
                  pl.BlockSpec((1, tn), lambda j: (0, j))],
        out_specs=pl.BlockSpec((m, tn), lambda j: (0, j)),
        out_shape=jax.ShapeDtypeStruct((m, n), F32),
        compiler_params=_params(("arbitrary",)),
    )(c_all, w_bf, b.reshape(1, n))


def _norm_mod(x, g, sc, sh):
    y = x * lax.rsqrt(jnp.mean(x * x, axis=-1, keepdims=True) + NORM_EPS)
    return (y * g) * (1.0 + sc) + sh


def _nmm_body(x_ref, g_ref, sc_ref, sh_ref, w_ref, o_ref, h_ref):
    @pl.when(pl.program_id(1) == 0)
    def _():
        h_ref[...] = _norm_mod(x_ref[...], g_ref[...], sc_ref[...], sh_ref[...]).astype(BF16)

    o_ref[...] = _dot(h_ref[...], w_ref[...])


def _norm_mod_matmul(x, g, sc, sh, w_bf, kind, tm, tn, tiles_per_seq):
    m, d = x.shape
    n = w_bf.shape[1]
    mod = _mod_spec(kind, tm, d, tiles_per_seq, False)
    return pl.pallas_call(
        _nmm_body,
        grid=(m // tm, n // tn),
        in_specs=[pl.BlockSpec((tm, d), lambda i, j: (i, 0)),
                  pl.BlockSpec((1, d), lambda i, j: (0, 0)),
                  mod, mod,
                  pl.BlockSpec((d, tn), lambda i, j: (0, j))],
        out_specs=pl.BlockSpec((tm, tn), lambda i, j: (i, j)),
        out_shape=jax.ShapeDtypeStruct((m, n), F32),
        scratch_shapes=[pltpu.VMEM((tm, d), BF16)],
        compiler_params=_params(("arbitrary", "arbitrary")),
    )(x, g.reshape(1, d), sc, sh, w_bf)


def _ffn_in_body(x_ref, g_ref, sc_ref, sh_ref, w1_ref, w2_ref, o_ref, h_ref):
    @pl.when(pl.program_id(1) == 0)
    def _():
        h_ref[...] = _norm_mod(x_ref[...], g_ref[...], sc_ref[...], sh_ref[...]).astype(BF16)

    h = h_ref[...]
    o_ref[...] = (_silu(_dot(h, w1_ref[...])) * _dot(h, w2_ref[...])).astype(BF16)


def _norm_mod_ffn_in(x, g, sc, sh, w_bf, kind, tm, tn, tiles_per_seq):
    m, d = x.shape
    nj = D_FF // tn
    mod = _mod_spec(kind, tm, d, tiles_per_seq, False)
    return pl.pallas_call(
        _ffn_in_body,
        grid=(m // tm, nj),
        in_specs=[pl.BlockSpec((tm, d), lambda i, j: (i, 0)),
                  pl.BlockSpec((1, d), lambda i, j: (0, 0)),
                  mod, mod,
                  pl.BlockSpec((d, tn), lambda i, j: (0, j)),
                  pl.BlockSpec((d, tn), lambda i, j: (0, j + nj))],
        out_specs=pl.BlockSpec((tm, tn), lambda i, j: (i, j)),
        out_shape=jax.ShapeDtypeStruct((m, D_FF), BF16),
        scratch_shapes=[pltpu.VMEM((tm, d), BF16)],
        compiler_params=_params(("arbitrary", "arbitrary")),
    )(x, g.reshape(1, d), sc, sh, w_bf, w_bf)


def _mmres_body(a_ref, w_ref, x_ref, g_ref, o_ref):
    o_ref[...] = x_ref[...] + g_ref[...] * _dot(a_ref[...], w_ref[...])


def _matmul_residual(a_bf, w_bf, x, gate, kind, tm, tn, tiles_per_seq):
    m, k = a_bf.shape
    n = w_bf.shape[1]
    return pl.pallas_call(
        _mmres_body,
        grid=(m // tm, n // tn),
        in_specs=[pl.BlockSpec((tm, k), lambda i, j: (i, 0)),
                  pl.BlockSpec((k, tn), lambda i, j: (0, j)),
                  pl.BlockSpec((tm, tn), lambda i, j: (i, j)),
                  _mod_spec(kind, tm, tn, tiles_per_seq, True)],
        out_specs=pl.BlockSpec((tm, tn), lambda i, j: (i, j)),
        out_shape=jax.ShapeDtypeStruct((m, n), F32),
        compiler_params=_params(("arbitrary", "arbitrary")),
    )(a_bf, w_bf, x, gate)


def _merge_body(oa_ref, or_ref, os_ref, wb_ref, ga_ref, gr_ref, gs_ref, o_ref):
    acc = None
    for n, (o_r, g_r) in enumerate(((oa_ref, ga_ref), (or_ref, gr_ref), (os_ref, gs_ref))):
        term = _sigmoid(g_r[...]) * _dot(o_r[...].astype(BF16), wb_ref[n])
        acc = term if acc is None else acc + term
    o_ref[...] = acc.astype(BF16)


def _branch_merge(o_att, o_ret, o_s5, z, wb_bf, tm, tn):
    m = z.shape[0]
    gate0 = OFF_GATE // tn
    per = D_MODEL // tn
    o_spec = pl.BlockSpec((tm, BRANCH_W), lambda i, j: (i, 0))

    def gate_spec(n):
        return pl.BlockSpec((tm, tn), lambda i, j: (i, gate0 + n * per + j))

    return pl.pallas_call(
        _merge_body,
        grid=(m // tm, per),
        in_specs=[o_spec, o_spec, o_spec,
                  pl.BlockSpec((N_BRANCH, BRANCH_W, tn), lambda i, j: (0, 0, j)),
                  gate_spec(0), gate_spec(1), gate_spec(2)],
        out_specs=pl.BlockSpec((tm, tn), lambda i, j: (i, j)),
        out_shape=jax.ShapeDtypeStruct((m, D_MODEL), BF16),
        compiler_params=_params(("arbitrary", "arbitrary")),
    )(o_att, o_ret, o_s5, wb_bf, z, z, z)


def _fnorm_body(x_ref, g_ref, o_ref):
    x = x_ref[...]
    o_ref[...] = x * lax.rsqrt(jnp.mean(x * x, axis=-1, keepdims=True) + NORM_EPS) * g_ref[...]


def _final_norm(x, g, tm):
    m, d = x.shape
    return pl.pallas_call(
        _fnorm_body,
        grid=(m // tm,),
        in_specs=[pl.BlockSpec((tm, d), lambda i: (i, 0)), pl.BlockSpec((1, d), lambda i: (0, 0))],
        out_specs=pl.BlockSpec((tm, d), lambda i: (i, 0)),
        out_shape=jax.ShapeDtypeStruct((m, d), F32),
        compiler_params=_params(("arbitrary",)),
    )(x, g.reshape(1, d))


def _bucket_np(dist):
    n = np.maximum(dist, 0)
    max_exact = REL_BUCKETS // 2
    nf = np.maximum(n, 1).astype(np.float32)
    large = max_exact + (np.log(nf / max_exact) / math.log(REL_MAX_DIST / max_exact)
                         * (REL_BUCKETS - max_exact)).astype(np.int32)
    large = np.minimum(large, REL_BUCKETS - 1)
    return np.where(n < max_exact, n, large).astype(np.int32)


def _bias_p_body(rb_ref, bk_ref, o_ref):
    h = pl.program_id(0)
    blk = MOBA_BLOCK
    for s in range(2):
        bk = bk_ref[s]
        acc = jnp.zeros((blk, blk), F32)
        for b in range(REL_BUCKETS):
            acc = jnp.where(bk == b, rb_ref[b, h], acc)
        if s == 0:
            rk = lax.broadcasted_iota(jnp.int32, (blk, blk), 0)
            rq = lax.broadcasted_iota(jnp.int32, (blk, blk), 1)
            acc = jnp.where(rk <= rq, acc, 2 * NEG)
        o_ref[s] = acc
    o_ref[2] = jnp.zeros((blk, blk), F32) + rb_ref[REL_BUCKETS - 1, h]


def _bias_tables_prompt(rel_bias):
    blk = MOBA_BLOCK
    rk = np.arange(blk)[:, None]
    rq = np.arange(blk)[None, :]
    buckets = np.stack([_bucket_np(rq - rk), _bucket_np(blk + rq - rk)])
    assert int(_bucket_np(np.array([blk + 1]))[0]) == REL_BUCKETS - 1
    return pl.pallas_call(
        _bias_p_body,
        grid=(ATT_HEADS,),
        in_specs=[pl.BlockSpec(memory_space=pltpu.SMEM),
                  pl.BlockSpec((2, blk, blk), lambda h: (0, 0, 0))],
        out_specs=pl.BlockSpec((None, 3, blk, blk), lambda h: (h, 0, 0, 0)),
        out_shape=jax.ShapeDtypeStruct((ATT_HEADS, 3, blk, blk), F32),
        compiler_params=_params(("arbitrary",)),
    )(rel_bias, jnp.asarray(buckets))


def _bias_s_body(rbt_ref, bk_ref, o_ref):
    bk = bk_ref[...]
    acc = jnp.zeros(bk.shape, F32)
    for b in range(REL_BUCKETS):
        acc = jnp.where(bk == b, rbt_ref[:, b:b + 1], acc)
    o_ref[...] = jnp.broadcast_to(acc, o_ref.shape)


def _bias_table_sample(rel_bias_t):
    dist = PAGE_SIZE - np.arange(PAGE_SIZE)
    assert int(_bucket_np(np.array([PAGE_SIZE + 1]))[0]) == REL_BUCKETS - 1
    buckets = np.broadcast_to(_bucket_np(dist).reshape(PAGE_SIZE, 1, 1), (PAGE_SIZE, ATT_HEADS, 1))
    return pl.pallas_call(
        _bias_s_body,
        out_shape=jax.ShapeDtypeStruct((PAGE_SIZE, ATT_HEADS, ATT_DH), F32),
    )(rel_bias_t, jnp.asarray(buckets))


def _topk_select(s, allowed, idx, n_cand):
    s = jnp.where(allowed, s, -jnp.inf)
    rank = jnp.zeros(s.shape, jnp.int32)
    for m in range(n_cand):
        sm = s[m:m + 1]
        beats = (sm > s) | ((sm == s) & (idx > m))
        rank = rank + beats.astype(jnp.int32)
    return jnp.where(allowed & (rank < MOBA_TOPK), 1.0, 0.0)


def _moba_p_body(q_ref, k_ref, v_ref, tb_ref, o_ref, kbf_ref, vt_ref, kmean_ref, sel_ref, *, nb):
    i = pl.program_id(2)
    blk = MOBA_BLOCK
    dh = ATT_DH

    @pl.when(i == 0)
    def _():
        kf = k_ref[...]
        for hh in range(2):
            kbf_ref[hh] = kf[:, hh * dh:(hh + 1) * dh].astype(BF16)
        vt_ref[...] = v_ref[...].T.astype(BF16)
        kmean_ref[...] = jnp.mean(kf.reshape(nb, blk, LANES), axis=1)

    q = q_ref[...]
    grp = MOBA_KEY_GROUP
    qs_all, carry0 = [], []
    for hh in range(2):
        cs = slice(hh * dh, (hh + 1) * dh)
        qh = q[:, cs]
        s = _dot_nt(kmean_ref[:, cs], qh, precision=HIGHEST)
        nidx = lax.broadcasted_iota(jnp.int32, s.shape, 0)
        sel_ref[hh] = jnp.where(nidx == i, 1.0, _topk_select(s, nidx < i, nidx, nb))
        qs_all.append((qh * dh ** -0.5).astype(BF16))
        carry0 += [jnp.full((1, blk), NEG, F32), jnp.zeros((1, blk), F32), jnp.zeros((dh, blk), F32)]

    def body(j, carry):
        base = j * grp
        st = pl.multiple_of(base * blk, grp * blk)
        out = []
        for hh in range(2):
            m, l, acc = carry[3 * hh:3 * hh + 3]
            lg_all = _dot_nt(kbf_ref[hh, pl.ds(st, grp * blk), :], qs_all[hh])
            lgs = []
            for g in range(grp):
                n = base + g
                which = jnp.where(n == i, 0, jnp.where(n == i - 1, 1, 2))
                lg = lg_all[g * blk:(g + 1) * blk] + tb_ref[hh, which]
                lgs.append(jnp.where(sel_ref[hh, pl.ds(n, 1), :] > 0.5, lg, 2 * NEG))
            m_new = m
            for lg in lgs:
                m_new = jnp.maximum(m_new, jnp.max(lg, axis=0, keepdims=True))
            a = jnp.exp(m - m_new)
            ps = [jnp.exp(lg - m_new) for lg in lgs]
            l = a * l
            for p in ps:
                l = l + jnp.sum(p, axis=0, keepdims=True)
            p_all = jnp.concatenate([p.astype(BF16) for p in ps], axis=0)
            acc = a * acc + _dot(vt_ref[hh * dh:(hh + 1) * dh, pl.ds(st, grp * blk)], p_all)
            out += [m_new, l, acc]
        return tuple(out)

    fin = lax.fori_loop(0, i // grp + 1, body, tuple(carry0))
    outs = [(fin[3 * hh + 2] / fin[3 * hh + 1]).T for hh in range(2)]
    o_ref[...] = jnp.concatenate(outs, axis=1).astype(BF16)


def _moba_prompt(z, bias_tab, n_seq, t):
    blk = MOBA_BLOCK
    nb = t // blk
    assert nb % MOBA_KEY_GROUP == 0
    hp = ATT_HEADS // 2
    kc, vc = OFF_AK // LANES, OFF_AV // LANES
    return pl.pallas_call(
        functools.partial(_moba_p_body, nb=nb),
        grid=(n_seq, hp, nb),
        in_specs=[pl.BlockSpec((blk, LANES), lambda b, h, i: (b * nb + i, h)),
                  pl.BlockSpec((t, LANES), lambda b, h, i: (b, kc + h)),
                  pl.BlockSpec((t, LANES), lambda b, h, i: (b, vc + h)),
                  pl.BlockSpec((2, 3, blk, blk), lambda b, h, i: (h, 0, 0, 0))],
        out_specs=pl.BlockSpec((blk, LANES), lambda b, h, i: (b * nb + i, h)),
        out_shape=jax.ShapeDtypeStruct((n_seq * t, ATT_W), BF16),
        scratch_shapes=[pltpu.VMEM((2, t, ATT_DH), BF16), pltpu.VMEM((LANES, t), BF16),
                        pltpu.VMEM((nb, LANES), F32), pltpu.VMEM((2, nb, blk), F32)],
        compiler_params=_params(("arbitrary", "arbitrary", "arbitrary")),
    )(z, z, z, bias_tab)


PAGES_PER_STEP = 4
PAGES_PER_BLOCK = MOBA_BLOCK // PAGE_SIZE
LOG2E = math.log2(math.e)


def _fold_rows(x, op):
    assert x.shape[0] & (x.shape[0] - 1) == 0
    while x.shape[0] > 1:
        half = x.shape[0] // 2
        x = op(x[:half], x[half:])
    return x[0]


def _moba_s_body(pt_ref, k0, k1, k2, k3, v0, v1, v2, v3, bias_ref, q_ref, kn_ref, vn_ref, rbt_ref,
                 o_ref, m_s, l_s, acc_s, ks_s, *, n_steps):
    del pt_ref
    g = pl.program_id(1)
    h, dh = ATT_HEADS, ATT_DH
    n_pages = n_steps * PAGES_PER_STEP
    q = q_ref[...]
    qs = q * (dh ** -0.5 * LOG2E)
    far_bias = jnp.broadcast_to(rbt_ref[:, REL_BUCKETS - 1:REL_BUCKETS], (h, dh)) * LOG2E
    ones = jnp.ones((dh, dh), BF16)

    def page(k_ref, v_ref, bias, row):
        kb = k_ref[...]
        prod = (kb * qs).reshape(PAGE_SIZE * h, dh)
        hi = prod.astype(BF16)
        lo = (prod - hi.astype(F32)).astype(BF16)
        lg = (_dot(hi, ones) + _dot(lo, ones)).reshape(PAGE_SIZE, h, dh)
        if bias is not None:
            lg = lg + bias
        mb = _fold_rows(lg, jnp.maximum)
        p = jnp.exp2(lg - mb)
        m_s[row] = mb + far_bias if bias is None else mb
        l_s[row] = _fold_rows(p, jnp.add)
        acc_s[row] = _fold_rows(p * v_ref[...], jnp.add)
        ks_s[row] = _fold_rows(kb, jnp.add)

    kpages, vpages = (k0, k1, k2, k3), (v0, v1, v2, v3)
    last = PAGES_PER_STEP - 1
    for j in range(last):
        page(kpages[j], vpages[j], None, g * PAGES_PER_STEP + j)

    @pl.when(g != n_steps - 1)
    def _():
        page(kpages[last], vpages[last], None, g * PAGES_PER_STEP + last)

    @pl.when(g == n_steps - 1)
    def _():
        page(kpages[last], vpages[last], bias_ref[...] * LOG2E, n_pages - 1)
        nblk = n_pages // PAGES_PER_BLOCK
        ks = jnp.sum(ks_s[...].reshape(nblk, PAGES_PER_BLOCK, h, dh), axis=1)
        s = jnp.sum(ks * q, axis=-1, keepdims=True) * (1.0 / MOBA_BLOCK)
        nidx = lax.broadcasted_iota(jnp.int32, s.shape, 0)
        sel = _topk_select(s, nidx >= 0, nidx, nblk)
        selp = jnp.broadcast_to(sel[:, None], (nblk, PAGES_PER_BLOCK, h, 1)).reshape(n_pages, h, 1) > 0.5
        ln = jnp.sum(qs * kn_ref[...], axis=-1, keepdims=True) + rbt_ref[:, 0:1] * LOG2E
        mm = jnp.where(selp, m_s[...], NEG)
        mx = jnp.maximum(jnp.max(mm, axis=0), ln)
        w = jnp.where(selp, jnp.exp2(mm - mx), 0.0)
        wn = jnp.exp2(ln - mx)
        den = jnp.sum(w * l_s[...], axis=0) + wn
        o_ref[...] = (jnp.sum(w * acc_s[...], axis=0) + wn * vn_ref[...]) / den


def _moba_sample(z_s, cache_k, cache_v, layer, page_table, bias_tab, rel_bias_t):
    n_seq, n_pages = page_table.shape
    assert n_pages % PAGES_PER_STEP == 0 and PAGES_PER_STEP % PAGES_PER_BLOCK == 0
    n_steps = n_pages // PAGES_PER_STEP
    heads = lambda off: z_s[:, off:off + ATT_W].reshape(n_seq, ATT_HEADS, ATT_DH)

    def page_spec(j):
        return pl.BlockSpec((None, None, PAGE_SIZE, ATT_HEADS, ATT_DH),
                            lambda s, g, pt: (layer, pt[s, g * PAGES_PER_STEP + j], 0, 0, 0))

    head_spec = pl.BlockSpec((None, ATT_HEADS, ATT_DH), lambda s, g, pt: (s, 0, 0))
    stat = pltpu.VMEM((n_pages, ATT_HEADS, ATT_DH), F32)
    grid_spec = pltpu.PrefetchScalarGridSpec(
        num_scalar_prefetch=1,
        grid=(n_seq, n_steps),
        in_specs=[page_spec(j) for j in range(PAGES_PER_STEP)] * 2 + [
            pl.BlockSpec((PAGE_SIZE, ATT_HEADS, ATT_DH), lambda s, g, pt: (0, 0, 0)),
            head_spec, head_spec, head_spec,
            pl.BlockSpec((ATT_HEADS, REL_BUCKETS), lambda s, g, pt: (0, 0))],
        out_specs=head_spec,
        scratch_shapes=[stat, stat, stat, stat],
    )
    out = pl.pallas_call(
        functools.partial(_moba_s_body, n_steps=n_steps),
        grid_spec=grid_spec,
        out_shape=jax.ShapeDtypeStruct((n_seq, ATT_HEADS, ATT_DH), F32),
        compiler_params=_params(("arbitrary", "arbitrary")),
    )(page_table, *([cache_k] * PAGES_PER_STEP), *([cache_v] * PAGES_PER_STEP), bias_tab,
      heads(OFF_AQ), heads(OFF_AK), heads(OFF_AV), rel_bias_t)
    return out.reshape(n_seq, ATT_W)


def _rope_body(inv_ref, sgn_ref, cos_ref, sin_ref, *, pos0, step, rows):
    r = lax.broadcasted_iota(jnp.int32, (rows, LANES), 0) + pl.program_id(0) * rows
    ang = (pos0 + step * r).astype(F32) * inv_ref[...]
    cos_ref[...] = jnp.cos(ang)
    sin_ref[...] = jnp.sin(ang) * sgn_ref[...]


def _rope_tables(n_rows, pos0, step):
    half = RET_DK // 2
    inv = 1.0 / (ROPE_BASE ** jnp.linspace(0.0, 1.0, half))
    inv_row = jnp.tile(inv, LANES // half).reshape(1, LANES).astype(F32)
    sgn = np.where((np.arange(LANES) % RET_DK) < half, -1.0, 1.0).astype(np.float32).reshape(1, LANES)
    rows = min(n_rows, 512)
    spec = pl.BlockSpec((rows, LANES), lambda i: (i, 0))
    cst = pl.BlockSpec((1, LANES), lambda i: (0, 0))
    return pl.pallas_call(
        functools.partial(_rope_body, pos0=pos0, step=step, rows=rows),
        grid=(n_rows // rows,),
        in_specs=[cst, cst],
        out_specs=[spec, spec],
        out_shape=[jax.ShapeDtypeStruct((n_rows, LANES), F32)] * 2,
        compiler_params=_params(("arbitrary",)),
    )(inv_row, jnp.asarray(sgn))


def _rotary128(x, cos, sin_signed):
    half = RET_DK // 2
    lane = lax.broadcasted_iota(jnp.int32, x.shape, 1)
    partner = jnp.where((lane % RET_DK) < half,
                        pltpu.roll(x, LANES - half, 1), pltpu.roll(x, half, 1))
    return x * cos + partner * sin_signed


def _groupnorm_gate(o, g):
    mu = jnp.mean(o, axis=-1, keepdims=True)
    var = jnp.mean((o - mu) ** 2, axis=-1, keepdims=True)
    return _silu(g) * ((o - mu) * lax.rsqrt(var + GN_EPS))


def _ret_log_decay():
    return jnp.log(1.0 - 2.0 ** (-5.0 - jnp.arange(RET_HEADS, dtype=F32)))


def _ret_p_body(lg_ref, q_ref, k_ref, v_ref, g_ref, cos_ref, sin_ref, o_ref, st_ref, *, chunk):
    hp = pl.program_id(1)
    c = pl.program_id(2)

    @pl.when(c == 0)
    def _():
        st_ref[...] = jnp.zeros(st_ref.shape, F32)

    cos, sin = cos_ref[...], sin_ref[...]
    q = _rotary128(q_ref[...], cos, sin)
    k = _rotary128(k_ref[...], cos, sin) * RET_DK ** -0.5
    ii = lax.broadcasted_iota(jnp.int32, (chunk, chunk), 0)
    jj = lax.broadcasted_iota(jnp.int32, (chunk, chunk), 1)
    diff = (ii - jj).astype(F32)
    ri = lax.broadcasted_iota(jnp.int32, (chunk, 1), 0).astype(F32)
    for hh in range(2):
        lgh = lg_ref[hp * 2 + hh]
        dmask = jnp.where(diff >= 0, jnp.exp(lgh * jnp.maximum(diff, 0.0)), 0.0)
        qh = q[:, hh * RET_DK:(hh + 1) * RET_DK].astype(BF16)
        kh = k[:, hh * RET_DK:(hh + 1) * RET_DK]
        vh = v_ref[:, hh * RET_DV:(hh + 1) * RET_DV].astype(BF16)
        s0 = st_ref[hh]
        a = _dot_nt(qh, kh.astype(BF16)) * dmask
        inner = _dot(a.astype(BF16), vh)
        cross = _dot(qh, s0.astype(BF16)) * jnp.exp(lgh * (ri + 1.0))
        kdec = (kh * jnp.exp(lgh * (chunk - 1.0 - ri))).astype(BF16)
        st_ref[hh] = jnp.exp(lgh * chunk + jnp.zeros((1, 1), F32)) * s0 + _dot_tn(kdec, vh)
        gh = g_ref[:, hh * RET_DV:(hh + 1) * RET_DV]
        o_ref[:, hh * RET_DV:(hh + 1) * RET_DV] = _groupnorm_gate(inner + cross, gh).astype(BF16)


def _ret_prompt(z, cos_tab, sin_tab, log_decay, n_seq, t):
    chunk = RET_CHUNK
    nc = t // chunk
    hp = RET_HEADS // 2
    qc, kc = OFF_RQ // LANES, OFF_RK // LANES
    vc, gc = OFF_RV // (2 * RET_DV), OFF_RG // (2 * RET_DV)
    return pl.pallas_call(
        functools.partial(_ret_p_body, chunk=chunk),
        grid=(n_seq, hp, nc),
        in_specs=[pl.BlockSpec(memory_space=pltpu.SMEM),
                  pl.BlockSpec((chunk, LANES), lambda b, h, c: (b * nc + c, qc + h)),
                  pl.BlockSpec((chunk, LANES), lambda b, h, c: (b * nc + c, kc + h)),
                  pl.BlockSpec((chunk, 2 * RET_DV), lambda b, h, c: (b * nc + c, vc + h)),
                  pl.BlockSpec((chunk, 2 * RET_DV), lambda b, h, c: (b * nc + c, gc + h)),
                  pl.BlockSpec((chunk, LANES), lambda b, h, c: (c, 0)),
                  pl.BlockSpec((chunk, LANES), lambda b, h, c: (c, 0))],
        out_specs=[pl.BlockSpec((chunk, 2 * RET_DV), lambda b, h, c: (b * nc + c, h)),
                   pl.BlockSpec((None, 2, RET_DK, RET_DV), lambda b, h, c: (b, h, 0, 0))],
        out_shape=[jax.ShapeDtypeStruct((n_seq * t, RET_V_W), BF16),
                   jax.ShapeDtypeStruct((n_seq, RET_HEADS, RET_DK, RET_DV), F32)],
        compiler_params=_params(("arbitrary", "arbitrary", "arbitrary")),
    )(log_decay, z, z, z, z, cos_tab, sin_tab)


def _ret_s_prep_body(q_ref, k_ref, cos_ref, sin_ref, qo_ref, ko_ref):
    cos, sin = cos_ref[0:1, :], sin_ref[0:1, :]
    for j in range(RET_QK_W // LANES):
        sl = slice(j * LANES, (j + 1) * LANES)
        qo_ref[:, sl] = _rotary128(q_ref[:, sl], cos, sin)
        ko_ref[:, sl] = _rotary128(k_ref[:, sl], cos, sin) * RET_DK ** -0.5


def _ret_s_body(lg_ref, q_ref, k_ref, v_ref, g_ref, s0_ref, o_ref, sn_ref):
    gam = jnp.exp(lg_ref[pl.program_id(1)] + jnp.zeros((1, 1, LANES), F32))
    q, k, v, s0 = q_ref[...], k_ref[...], v_ref[...], s0_ref[...]
    o = jnp.sum(q * s0, axis=1, keepdims=True) * gam + jnp.sum(q * k, axis=1, keepdims=True) * v
    sn_ref[...] = gam * s0 + k * v
    o_ref[...] = _groupnorm_gate(o, g_ref[...])


def _ret_sample(z_s, state, cos_tab, sin_tab, log_decay):
    n_seq = z_s.shape[0]
    tab = pl.BlockSpec((SUBLANES, LANES), lambda i: (0, 0))
    q_rot, k_rot = pl.pallas_call(
        _ret_s_prep_body,
        grid=(1,),
        in_specs=[pl.BlockSpec((n_seq, RET_QK_W), lambda i: (0, OFF_RQ // RET_QK_W)),
                  pl.BlockSpec((n_seq, RET_QK_W), lambda i: (0, OFF_RK // RET_QK_W)), tab, tab],
        out_specs=[pl.BlockSpec((n_seq, RET_QK_W), lambda i: (0, 0))] * 2,
        out_shape=[jax.ShapeDtypeStruct((n_seq, RET_QK_W), F32)] * 2,
        compiler_params=_params(("arbitrary",)),
    )(z_s, z_s, cos_tab, sin_tab)
    col = lambda a: a.reshape(n_seq, RET_HEADS, RET_DK, 1)
    row = lambda a: a.reshape(n_seq, RET_HEADS, 1, RET_DV)
    bt = 16
    col_spec = pl.BlockSpec((bt, None, RET_DK, 1), lambda i, h: (i, h, 0, 0))
    row_spec = pl.BlockSpec((bt, None, 1, RET_DV), lambda i, h: (i, h, 0, 0))
    st_spec = pl.BlockSpec((bt, None, RET_DK, RET_DV), lambda i, h: (i, h, 0, 0))
    o, s_new = pl.pallas_call(
        _ret_s_body,
        grid=(n_seq // bt, RET_HEADS),
        in_specs=[pl.BlockSpec(memory_space=pltpu.SMEM), col_spec, col_spec, row_spec, row_spec, st_spec],
        out_specs=[row_spec, st_spec],
        out_shape=[jax.ShapeDtypeStruct((n_seq, RET_HEADS, 1, RET_DV), F32),
                   jax.ShapeDtypeStruct((n_seq, RET_HEADS, RET_DK, RET_DV), F32)],
        compiler_params=_params(("arbitrary", "arbitrary")),
    )(log_decay, col(q_rot), col(k_rot), row(z_s[:, OFF_RV:OFF_RV + RET_V_W]),
      row(z_s[:, OFF_RG:OFF_RG + RET_V_W]), state)
    return o.reshape(n_seq, RET_V_W), s_new


def _s5_prep_body(are_ref, aim_ref, ldt_ref, bre_ref, bim_ref, pwr_ref, pwi_ref, bbr_ref, bbi_ref):
    ar, ai = are_ref[...], aim_ref[...]
    dt = jnp.exp(ldt_ref[...])
    kk = (lax.broadcasted_iota(jnp.int32, pwr_ref.shape, 1) + 1).astype(F32)
    mag = jnp.exp(ar * dt * kk)
    ang = ai * dt * kk
    pwr_ref[...] = mag * jnp.cos(ang)
    pwi_ref[...] = mag * jnp.sin(ang)
    mag1 = jnp.exp(ar * dt)
    nr = mag1 * jnp.cos(ai * dt) - 1.0
    ni = mag1 * jnp.sin(ai * dt)
    den = ar * ar + ai * ai
    cr = (nr * ar + ni * ai) / den
    ci = (ni * ar - nr * ai) / den
    bre, bim = bre_ref[...], bim_ref[...]
    bbr_ref[...] = cr * bre - ci * bim
    bbi_ref[...] = cr * bim + ci * bre


def _s5_prepare(a_re, a_im, log_dt, b_re, b_im, c_re, c_im, d):
    g, p, c = S5_GROUPS, S5_STATE, S5_GROUP
    g3 = lambda a: a.reshape(g, 1, p)
    ldt = jnp.broadcast_to(log_dt.reshape(g, 1, 1), (g, 1, p))
    pwr, pwi, bbr, bbi = pl.pallas_call(
        _s5_prep_body,
        out_shape=[jax.ShapeDtypeStruct((g, SUBLANES, p), F32)] * 2
        + [jax.ShapeDtypeStruct((g, c, p), F32)] * 2,
    )(g3(a_re), g3(a_im), ldt, b_re.transpose(0, 2, 1), b_im.transpose(0, 2, 1))
    eye = jnp.eye(S5_GT, dtype=F32)
    nt = S5_NT

    def in_tile(bb):
        return jnp.einsum("tgcp,gh->tgchp", bb.reshape(nt, S5_GT, c, p), eye).reshape(nt, S5_GT * c, S5_GT * p)

    def out_tile(cc):
        return jnp.einsum("tgcp,gh->tgphc", cc.reshape(nt, S5_GT, c, p), eye).reshape(nt, S5_GT * p, S5_GT * c)

    b_tile = jnp.concatenate([in_tile(bbr), in_tile(bbi)], axis=2).astype(BF16)
    c_tile = jnp.concatenate([out_tile(c_re), out_tile(-c_im)], axis=1).astype(BF16)
    d_tile = d.reshape(nt, 1, S5_GT * c)

    def pw_tile(pw):
        return pw.reshape(nt, S5_GT, SUBLANES, p).transpose(0, 2, 1, 3).reshape(nt, SUBLANES, S5_GT * p)

    pr, pi = pw_tile(pwr), pw_tile(pwi)
    rows = jnp.arange(SUBLANES)[None, :, None]
    slabs = []
    for shift in (1, 2, 4):
        keep = rows >= shift
        slabs += [jnp.where(keep, pr[:, shift - 1:shift, :], 0.0), jnp.where(keep, pi[:, shift - 1:shift, :], 0.0)]
    scan_c = jnp.stack(slabs + [pr, pi], axis=1)
    lam1 = jnp.stack([pr[:, 0:1, :], pi[:, 0:1, :]], axis=1)
    return b_tile, c_tile, d_tile, scan_c, lam1


def _gelu_tanh(y):
    return 0.5 * y * (1.0 + jnp.tanh(math.sqrt(2.0 / math.pi) * (y + 0.044715 * (y * y * y))))


def _s5_p_body(u_ref, bt_ref, ct_ref, d_ref, sc_ref, z_ref, st_ref, x_ref, carry_ref, *, tc):
    ns = S5_TILE_STATES

    @pl.when(pl.program_id(2) == 0)
    def _():
        carry_ref[...] = jnp.zeros(carry_ref.shape, F32)

    u = u_ref[...]
    x_ref[...] = _dot(u.astype(BF16), bt_ref[...])

    def tile(t, carry):
        cr, ci = carry
        st = pl.multiple_of(t * SUBLANES, SUBLANES)
        xr = x_ref[pl.ds(st, SUBLANES), 0:ns]
        xi = x_ref[pl.ds(st, SUBLANES), ns:2 * ns]
        for s, shift in enumerate((1, 2, 4)):
            ar, ai = sc_ref[2 * s], sc_ref[2 * s + 1]
            sr, si = pltpu.roll(xr, shift, 0), pltpu.roll(xi, shift, 0)
            xr, xi = xr + ar * sr - ai * si, xi + ar * si + ai * sr
        pr, pi = sc_ref[6], sc_ref[7]
        xr, xi = xr + pr * cr - pi * ci, xi + pr * ci + pi * cr
        x_ref[pl.ds(st, SUBLANES), 0:ns] = xr
        x_ref[pl.ds(st, SUBLANES), ns:2 * ns] = xi
        return xr[SUBLANES - 1:SUBLANES, :], xi[SUBLANES - 1:SUBLANES, :]

    cr, ci = lax.fori_loop(0, tc // SUBLANES, tile, (carry_ref[:, 0:ns], carry_ref[:, ns:2 * ns]))
    last = jnp.concatenate([cr, ci], axis=1)
    carry_ref[...] = last
    st_ref[...] = last
    y = _dot(x_ref[...].astype(BF16), ct_ref[...]) + d_ref[...] * u
    z_ref[...] = _gelu_tanh(y)


def _s5_prompt(z, tiles, n_seq, t):
    b_tile, c_tile, d_tile, scan_c, _ = tiles
    tc = 256
    nc = t // tc
    uc = OFF_SU // LANES
    ns2 = 2 * S5_TILE_STATES
    zs, st = pl.pallas_call(
        functools.partial(_s5_p_body, tc=tc),
        grid=(n_seq, S5_NT, nc),
        in_specs=[pl.BlockSpec((tc, LANES), lambda b, g, c: (b * nc + c, uc + g)),
                  pl.BlockSpec((None, LANES, ns2), lambda b, g, c: (g, 0, 0)),
                  pl.BlockSpec((None, ns2, LANES), lambda b, g, c: (g, 0, 0)),
                  pl.BlockSpec((None, 1, LANES), lambda b, g, c: (g, 0, 0)),
                  pl.BlockSpec((None, 8, SUBLANES, S5_TILE_STATES), lambda b, g, c: (g, 0, 0, 0))],
        out_specs=[pl.BlockSpec((tc, LANES), lambda b, g, c: (b * nc + c, g)),
                   pl.BlockSpec((None, None, 1, ns2), lambda b, g, c: (b, g, 0, 0))],
        out_shape=[jax.ShapeDtypeStruct((n_seq * t, S5_W), F32),
                   jax.ShapeDtypeStruct((n_seq, S5_NT, 1, ns2), F32)],
        scratch_shapes=[pltpu.VMEM((tc, ns2), F32), pltpu.VMEM((1, ns2), F32)],
        compiler_params=_params(("arbitrary", "arbitrary", "arbitrary")),
    )(z, b_tile, c_tile, d_tile, scan_c)
    s_re = st[:, :, 0, :S5_TILE_STATES].reshape(n_seq, S5_GROUPS, S5_STATE)
    s_im = st[:, :, 0, S5_TILE_STATES:].reshape(n_seq, S5_GROUPS, S5_STATE)
    return zs, s_re, s_im


def _s5_s_body(u_ref, bt_ref, ct_ref, d_ref, l1_ref, x0_ref, z_ref, xn_ref):
    ns = S5_TILE_STATES
    u = u_ref[...]
    bu = _dot(u.astype(BF16), bt_ref[...])
    lr, li = l1_ref[0], l1_ref[1]
    x0r, x0i = x0_ref[:, 0:ns], x0_ref[:, ns:2 * ns]
    x = jnp.concatenate([bu[:, 0:ns] + lr * x0r - li * x0i, bu[:, ns:2 * ns] + lr * x0i + li * x0r], axis=1)
    xn_ref[...] = x
    z_ref[...] = _gelu_tanh(_dot(x.astype(BF16), ct_ref[...]) + d_ref[...] * u)


def _s5_sample(z_s, tiles, x0_re, x0_im):
    b_tile, c_tile, d_tile, _, lam1 = tiles
    n_seq = z_s.shape[0]
    uc = OFF_SU // LANES
    ns2 = 2 * S5_TILE_STATES

    def to_tiles(a):
        return a.reshape(n_seq, S5_NT, S5_TILE_STATES).transpose(1, 0, 2)

    x0 = jnp.concatenate([to_tiles(x0_re), to_tiles(x0_im)], axis=2)
    zs, xn = pl.pallas_call(
        _s5_s_body,
        grid=(S5_NT,),
        in_specs=[pl.BlockSpec((n_seq, LANES), lambda g: (0, uc + g)),
                  pl.BlockSpec((None, LANES, ns2), lambda g: (g, 0, 0)),
                  pl.BlockSpec((None, ns2, LANES), lambda g: (g, 0, 0)),
                  pl.BlockSpec((None, 1, LANES), lambda g: (g, 0, 0)),
                  pl.BlockSpec((None, 2, 1, S5_TILE_STATES), lambda g: (g, 0, 0, 0)),
                  pl.BlockSpec((None, n_seq, ns2), lambda g: (g, 0, 0))],
        out_specs=[pl.BlockSpec((n_seq, LANES), lambda g: (0, g)),
                   pl.BlockSpec((None, n_seq, ns2), lambda g: (g, 0, 0))],
        out_shape=[jax.ShapeDtypeStruct((n_seq, S5_W), F32),
                   jax.ShapeDtypeStruct((S5_NT, n_seq, ns2), F32)],
        compiler_params=_params(("arbitrary",)),
    )(z_s, b_tile, c_tile, d_tile, lam1, x0)

    def from_tiles(a):
        return a.transpose(1, 0, 2).reshape(n_seq, S5_GROUPS, S5_STATE)

    return zs, from_tiles(xn[:, :, :S5_TILE_STATES]), from_tiles(xn[:, :, S5_TILE_STATES:])


def _glu_body(zf_ref, zc_ref, w_ref, o_ref):
    o_ref[...] = (zc_ref[...] * _sigmoid(_dot(zf_ref[...].astype(BF16), w_ref[...]))).astype(BF16)


def _half_glu(zs, w_bf, tm, tn):
    m, w = zs.shape
    return pl.pallas_call(
        _glu_body,
        grid=(m // tm, w // tn),
        in_specs=[pl.BlockSpec((tm, w), lambda i, j: (i, 0)),
                  pl.BlockSpec((tm, tn), lambda i, j: (i, j)),
                  pl.BlockSpec((w, tn), lambda i, j: (0, j))],
        out_specs=pl.BlockSpec((tm, tn), lambda i, j: (i, j)),
        out_shape=jax.ShapeDtypeStruct((m, w), BF16),
        compiler_params=_params(("arbitrary", "arbitrary")),
    )(zs, zs, w_bf)


def _layer(x, mods, lw, mix, kind, tm, tiles_per_seq):
    norm1, norm2, w_in, w_branch, w_out, w_ffn_in, w_ffn_out = lw
    sh1, sc1, g1, sh2, sc2, g2 = mods
    z = _norm_mod_matmul(x, norm1, sc1, sh1, w_in, kind, tm, 512, tiles_per_seq)
    (o_att, o_ret, o_s5), state = mix(z)
    merged = _branch_merge(o_att, o_ret, o_s5, z, w_branch, tm, 512)
    x = _matmul_residual(merged, w_out, x, g1, kind, tm, 512, tiles_per_seq)
    act = _norm_mod_ffn_in(x, norm2, sc2, sh2, w_ffn_in, kind, tm, 512, tiles_per_seq)
    tm_out = min(tm, 512)
    x = _matmul_residual(act, w_ffn_out, x, g2, kind, tm_out, 512, tiles_per_seq * (tm // tm_out))
    return x, z, state


def kernel(x_prompt, x_sample, c_prompt, c_sample, cache_k, cache_v, page_table, state_ret, state_s5_re, state_s5_im, rel_bias, norm1_g, norm2_g, w_ada, b_ada, w_in, s5_a_re, s5_a_im, s5_log_dt, s5_b_re, s5_b_im, s5_c_re, s5_c_im, s5_d, w_glu, w_branch, w_out, w_ffn_in, w_ffn_out, final_g):
    n_seq, t, d = x_prompt.shape
    n_dec = x_sample.shape[0]
    depth = w_in.shape[0]
    past = page_table.shape[1] * PAGE_SIZE
    xp = x_prompt.reshape(n_seq * t, d)
    xs = x_sample.reshape(n_dec, d)
    tm_p = 1024
    tiles_per_seq = t // tm_p

    bias_p = _bias_tables_prompt(rel_bias)
    rel_bias_t = rel_bias.T
    bias_s = _bias_table_sample(rel_bias_t)
    cos_p, sin_p = _rope_tables(t, 0, 1)
    cos_s, sin_s = _rope_tables(SUBLANES, past, 0)
    log_decay = _ret_log_decay()
    pad = (-(n_seq + n_dec)) % SUBLANES
    c_all = jnp.concatenate([c_prompt, c_sample, jnp.zeros((pad, d), F32)], axis=0)

    new_p, new_s = [], []
    for l in range(depth):
        mod = _ada(c_all, w_ada[l].astype(BF16), b_ada[l])
        mod_p = mod[:n_seq].reshape(n_seq, 6, 1, d)
        mod_s = mod[n_seq:n_seq + n_dec].reshape(n_dec, 6, d)
        mods_p = tuple(mod_p[:, i] for i in range(6))
        mods_s = tuple(mod_s[:, i] for i in range(6))
        lw = (norm1_g[l], norm2_g[l], w_in[l].astype(BF16), w_branch[l].astype(BF16), w_out[l].astype(BF16),
              w_ffn_in[l].astype(BF16), w_ffn_out[l].astype(BF16))
        glu_w = w_glu[l].astype(BF16)
        tiles = _s5_prepare(s5_a_re[l], s5_a_im[l], s5_log_dt[l], s5_b_re[l], s5_b_im[l],
                            s5_c_re[l], s5_c_im[l], s5_d[l])

        def mix_p(z):
            o_att = _moba_prompt(z, bias_p, n_seq, t)
            o_ret, s_ret = _ret_prompt(z, cos_p, sin_p, log_decay, n_seq, t)
            zs5, s_re, s_im = _s5_prompt(z, tiles, n_seq, t)
            return (o_att, o_ret, _half_glu(zs5, glu_w, tm_p, 512)), (s_ret, s_re, s_im)

        def mix_s(z, l=l):
            o_att = _moba_sample(z, cache_k, cache_v, l, page_table, bias_s, rel_bias_t)
            o_ret, s_ret = _ret_sample(z, state_ret[l], cos_s, sin_s, log_decay)
            zs5, s_re, s_im = _s5_sample(z, tiles, state_s5_re[l], state_s5_im[l])
            return (o_att, o_ret, _half_glu(zs5, glu_w, n_dec, 512)), (s_ret, s_re, s_im)

        xp, zp, st_p = _layer(xp, mods_p, lw, mix_p, "prompt", tm_p, tiles_per_seq)
        xs, zs, st_s = _layer(xs, mods_s, lw, mix_s, "sample", n_dec, 1)
        kv = lambda z, n, tt, off: z[:, off:off + ATT_W].reshape(n, tt, ATT_HEADS, ATT_DH)
        new_p.append((kv(zp, n_seq, t, OFF_AK), kv(zp, n_seq, t, OFF_AV)) + st_p)
        new_s.append((kv(zs, n_dec, 1, OFF_AK), kv(zs, n_dec, 1, OFF_AV)) + st_s)

    y_prompt = _final_norm(xp, final_g, tm_p).reshape(n_seq, t, d)
    y_sample = _final_norm(xs, final_g, n_dec).reshape(n_dec, 1, d)
    outs_p = [jnp.stack(a) for a in zip(*new_p)]
    outs_s = [jnp.stack(a) for a in zip(*new_s)]
    return (y_prompt, y_sample, *outs_p, *outs_s)
```

```python
import functools
import math

import numpy as np
import jax
import jax.numpy as jnp
from jax import lax
from jax.experimental import pallas as pl
from jax.experimental.pallas import tpu as pltpu

F32 = jnp.float32
BF16 = jnp.bfloat16
HIGHEST = lax.Precision.HIGHEST

D_MODEL = 2048
PAGE_SIZE = 128
ATT_HEADS = 16
ATT_DH = 64
ATT_W = ATT_HEADS * ATT_DH
MOBA_BLOCK = 256
MOBA_TOPK = 3
REL_BUCKETS = 32
REL_MAX_DIST = 128
RET_HEADS = 8
RET_DK = 64
RET_DV = 128
RET_QK_W = RET_HEADS * RET_DK
RET_V_W = RET_HEADS * RET_DV
RET_CHUNK = 256
ROPE_BASE = 10000.0
S5_W = 1024
S5_GROUP = 16
S5_GROUPS = S5_W // S5_GROUP
S5_STATE = 64
N_BRANCH = 3
BRANCH_W = 1024
D_FF = ((8 * D_MODEL + 3 * 256 - 1) // (3 * 256)) * 256
NORM_EPS = 1e-6
GN_EPS = 1e-5
D_IN = 3 * ATT_W + 2 * RET_QK_W + 2 * RET_V_W + S5_W + N_BRANCH * D_MODEL

OFF_AQ, OFF_AK, OFF_AV = 0, ATT_W, 2 * ATT_W
OFF_RQ = 3 * ATT_W
OFF_RK = OFF_RQ + RET_QK_W
OFF_RV = OFF_RK + RET_QK_W
OFF_RG = OFF_RV + RET_V_W
OFF_SU = OFF_RG + RET_V_W
OFF_GATE = OFF_SU + S5_W

LANES = 128
SUBLANES = 8
VMEM_LIMIT = 56 * 1024 * 1024
NEG = -1e30
LOG2E = math.log2(math.e)
MOBA_KEY_GROUP = 4
S5_GT = 8
S5_TILE_STATES = S5_GT * S5_STATE
S5_NT = S5_GROUPS // S5_GT


def _params(sem):
    return pltpu.CompilerParams(dimension_semantics=sem, vmem_limit_bytes=VMEM_LIMIT)


def _sigmoid(x):
    return 1.0 / (1.0 + jnp.exp(-x))


def _silu(x):
    return x * _sigmoid(x)


def _dot(a, b):
    return jnp.dot(a, b, preferred_element_type=F32)


def _dot_nt(a, b, precision=None):
    return lax.dot_general(a, b, (((1,), (1,)), ((), ())), precision=precision,
                           preferred_element_type=F32)


def _dot_tn(a, b):
    return lax.dot_general(a, b, (((0,), (0,)), ((), ())), preferred_element_type=F32)


def _mod_spec(kind, tm, tn, tiles_per_seq, col_blocked):
    if kind == "prompt":
        if col_blocked:
            return pl.BlockSpec((None, 1, tn), lambda i, j: (i // tiles_per_seq, 0, j))
        return pl.BlockSpec((None, 1, tn), lambda i, j: (i // tiles_per_seq, 0, 0))
    if col_blocked:
        return pl.BlockSpec((tm, tn), lambda i, j: (i, j))
    return pl.BlockSpec((tm, tn), lambda i, j: (i, 0))


def _ada_body(c_ref, w_ref, b_ref, o_ref):
    a = _silu(c_ref[...]).astype(BF16)
    o_ref[...] = _dot(a, w_ref[...]) + b_ref[...]


def _ada(c_all, w_bf, b):
    m, d = c_all.shape
    n = w_bf.shape[1]
    tn = 1024
    return pl.pallas_call(
        _ada_body,
        grid=(n // tn,),
        in_specs=[pl.BlockSpec((m, d), lambda j: (0, 0)),
                  pl.BlockSpec((d, tn), lambda j: (0, j)),
                  pl.BlockSpec((1, tn), lambda j: (0, j))],
        out_specs=pl.BlockSpec((m, tn), lambda j: (0, j)),
        out_shape=jax.ShapeDtypeStruct((m, n), F32),
        compiler_params=_params(("arbitrary",)),
    )(c_all, w_bf, b.reshape(1, n))


def _norm_mod(x, g, sc, sh):
    y = x * lax.rsqrt(jnp.mean(x * x, axis=-1, keepdims=True) + NORM_EPS)
    return (y * g) * (1.0 + sc) + sh


def _nmm_body(x_ref, g_ref, sc_ref, sh_ref, w_ref, o_ref, h_ref):
    @pl.when(pl.program_id(1) == 0)
    def _():
        h_ref[...] = _norm_mod(x_ref[...], g_ref[...], sc_ref[...], sh_ref[...]).astype(BF16)

    o_ref[...] = _dot(h_ref[...], w_ref[...])


def _norm_mod_matmul(x, g, sc, sh, w_bf, kind, tm, tn, tiles_per_seq):
    m, d = x.shape
    n = w_bf.shape[1]
    mod = _mod_spec(kind, tm, d, tiles_per_seq, False)
    return pl.pallas_call(
        _nmm_body,
        grid=(m // tm, n // tn),
        in_specs=[pl.BlockSpec((tm, d), lambda i, j: (i, 0)),
                  pl.BlockSpec((1, d), lambda i, j: (0, 0)),
                  mod, mod,
                  pl.BlockSpec((d, tn), lambda i, j: (0, j))],
        out_specs=pl.BlockSpec((tm, tn), lambda i, j: (i, j)),
        out_shape=jax.ShapeDtypeStruct((m, n), F32),
        scratch_shapes=[pltpu.VMEM((tm, d), BF16)],
        compiler_params=_params(("arbitrary", "arbitrary")),
    )(x, g.reshape(1, d), sc, sh, w_bf)


def _ffn_in_body(x_ref, g_ref, sc_ref, sh_ref, w1_ref, w2_ref, o_ref, h_ref):
    @pl.when(pl.program_id(1) == 0)
    def _():
        h_ref[...] = _norm_mod(x_ref[...], g_ref[...], sc_ref[...], sh_ref[...]).astype(BF16)

    h = h_ref[...]
    o_ref[...] = (_silu(_dot(h, w1_ref[...])) * _dot(h, w2_ref[...])).astype(BF16)


def _norm_mod_ffn_in(x, g, sc, sh, w_bf, kind, tm, tn, tiles_per_seq):
    m, d = x.shape
    nj = D_FF // tn
    mod = _mod_spec(kind, tm, d, tiles_per_seq, False)
    return pl.pallas_call(
        _ffn_in_body,
        grid=(m // tm, nj),
        in_specs=[pl.BlockSpec((tm, d), lambda i, j: (i, 0)),
                  pl.BlockSpec((1, d), lambda i, j: (0, 0)),
                  mod, mod,
                  pl.BlockSpec((d, tn), lambda i, j: (0, j)),
                  pl.BlockSpec((d, tn), lambda i, j: (0, j + nj))],
        out_specs=pl.BlockSpec((tm, tn), lambda i, j: (i, j)),
        out_shape=jax.ShapeDtypeStruct((m, D_FF), BF16),
        scratch_shapes=[pltpu.VMEM((tm, d), BF16)],
        compiler_params=_params(("arbitrary", "arbitrary")),
    )(x, g.reshape(1, d), sc, sh, w_bf, w_bf)


def _mmres_body(a_ref, w_ref, x_ref, g_ref, o_ref):
    o_ref[...] = x_ref[...] + g_ref[...] * _dot(a_ref[...], w_ref[...])


def _matmul_residual(a_bf, w_bf, x, gate, kind, tm, tn, tiles_per_seq):
    m, k = a_bf.shape
    n = w_bf.shape[1]
    return pl.pallas_call(
        _mmres_body,
        grid=(m // tm, n // tn),
        in_specs=[pl.BlockSpec((tm, k), lambda i, j: (i, 0)),
                  pl.BlockSpec((k, tn), lambda i, j: (0, j)),
                  pl.BlockSpec((tm, tn), lambda i, j: (i, j)),
                  _mod_spec(kind, tm, tn, tiles_per_seq, True)],
        out_specs=pl.BlockSpec((tm, tn), lambda i, j: (i, j)),
        out_shape=jax.ShapeDtypeStruct((m, n), F32),
        compiler_params=_params(("arbitrary", "arbitrary")),
    )(a_bf, w_bf, x, gate)


def _merge_body(oa_ref, or_ref, os_ref, wb_ref, ga_ref, gr_ref, gs_ref, o_ref):
    acc = None
    for n, (o_r, g_r) in enumerate(((oa_ref, ga_ref), (or_ref, gr_ref), (os_ref, gs_ref))):
        term = _sigmoid(g_r[...]) * _dot(o_r[...].astype(BF16), wb_ref[n])
        acc = term if acc is None else acc + term
    o_ref[...] = acc.astype(BF16)


def _branch_merge(o_att, o_ret, o_s5, z, wb_bf, tm, tn):
    m = z.shape[0]
    gate0 = OFF_GATE // tn
    per = D_MODEL // tn
    o_spec = pl.BlockSpec((tm, BRANCH_W), lambda i, j: (i, 0))

    def gate_spec(n):
        return pl.BlockSpec((tm, tn), lambda i, j: (i, gate0 + n * per + j))

    return pl.pallas_call(
        _merge_body,
        grid=(m // tm, per),
        in_specs=[o_spec, o_spec, o_spec,
                  pl.BlockSpec((N_BRANCH, BRANCH_W, tn), lambda i, j: (0, 0, j)),
                  gate_spec(0), gate_spec(1), gate_spec(2)],
        out_specs=pl.BlockSpec((tm, tn), lambda i, j: (i, j)),
        out_shape=jax.ShapeDtypeStruct((m, D_MODEL), BF16),
        compiler_params=_params(("arbitrary", "arbitrary")),
    )(o_att, o_ret, o_s5, wb_bf, z, z, z)


def _fnorm_body(x_ref, g_ref, o_ref):
    x = x_ref[...]
    o_ref[...] = x * lax.rsqrt(jnp.mean(x * x, axis=-1, keepdims=True) + NORM_EPS) * g_ref[...]


def _final_norm(x, g, tm):
    m, d = x.shape
    return pl.pallas_call(
        _fnorm_body,
        grid=(m // tm,),
        in_specs=[pl.BlockSpec((tm, d), lambda i: (i, 0)), pl.BlockSpec((1, d), lambda i: (0, 0))],
        out_specs=pl.BlockSpec((tm, d), lambda i: (i, 0)),
        out_shape=jax.ShapeDtypeStruct((m, d), F32),
        compiler_params=_params(("arbitrary",)),
    )(x, g.reshape(1, d))


def _bucket_np(dist):
    n = np.maximum(dist, 0)
    max_exact = REL_BUCKETS // 2
    nf = np.maximum(n, 1).astype(np.float32)
    large = max_exact + (np.log(nf / max_exact) / math.log(REL_MAX_DIST / max_exact)
                         * (REL_BUCKETS - max_exact)).astype(np.int32)
    large = np.minimum(large, REL_BUCKETS - 1)
    return np.where(n < max_exact, n, large).astype(np.int32)


def _bias_p_body(rb_ref, bk_ref, o_ref):
    h = pl.program_id(0)
    blk = MOBA_BLOCK
    far = rb_ref[REL_BUCKETS - 1, h]
    for s in range(2):
        bk = bk_ref[s]
        acc = jnp.zeros((blk, blk), F32)
        for b in range(REL_BUCKETS):
            acc = jnp.where(bk == b, rb_ref[b, h], acc)
        acc = (acc - far) * LOG2E
        if s == 0:
            rk = lax.broadcasted_iota(jnp.int32, (blk, blk), 0)
            rq = lax.broadcasted_iota(jnp.int32, (blk, blk), 1)
            acc = jnp.where(rk <= rq, acc, 2 * NEG)
        o_ref[s] = acc


def _bias_tables_prompt(rel_bias):
    blk = MOBA_BLOCK
    rk = np.arange(blk)[:, None]
    rq = np.arange(blk)[None, :]
    buckets = np.stack([_bucket_np(rq - rk), _bucket_np(blk + rq - rk)])
    assert int(_bucket_np(np.array([blk + 1]))[0]) == REL_BUCKETS - 1
    return pl.pallas_call(
        _bias_p_body,
        grid=(ATT_HEADS,),
        in_specs=[pl.BlockSpec(memory_space=pltpu.SMEM),
                  pl.BlockSpec((2, blk, blk), lambda h: (0, 0, 0))],
        out_specs=pl.BlockSpec((None, 2, blk, blk), lambda h: (h, 0, 0, 0)),
        out_shape=jax.ShapeDtypeStruct((ATT_HEADS, 2, blk, blk), F32),
        compiler_params=_params(("arbitrary",)),
    )(rel_bias, jnp.asarray(buckets))


def _topk_select(s, allowed, idx, n_cand):
    s = jnp.where(allowed, s, -jnp.inf)
    rank = jnp.zeros(s.shape, jnp.int32)
    for m in range(n_cand):
        sm = s[m:m + 1]
        beats = (sm > s) | ((sm == s) & (idx > m))
        rank = rank + beats.astype(jnp.int32)
    return jnp.where(allowed & (rank < MOBA_TOPK), 1.0, 0.0)


def _moba_p_body(q_ref, k_ref, v_ref, tb_ref, o_ref, kbf_ref, vt_ref, kmean_ref, sel_ref, *, nb):
    i = pl.program_id(2)
    blk = MOBA_BLOCK
    dh = ATT_DH

    @pl.when(i == 0)
    def _():
        kf = k_ref[...]
        for hh in range(2):
            kbf_ref[hh] = kf[:, hh * dh:(hh + 1) * dh].astype(BF16)
        vt_ref[...] = v_ref[...].T.astype(BF16)
        kmean_ref[...] = jnp.mean(kf.reshape(nb, blk, LANES), axis=1)

    q = q_ref[...]
    grp = MOBA_KEY_GROUP
    qs_all, prev_sel, carry0 = [], [], []
    for hh in range(2):
        cs = slice(hh * dh, (hh + 1) * dh)
        qh = q[:, cs]
        s = _dot_nt(kmean_ref[:, cs], qh, precision=HIGHEST)
        nidx = lax.broadcasted_iota(jnp.int32, s.shape, 0)
        sel = _topk_select(s, nidx < i, nidx, nb)
        sel_ref[hh] = jnp.where(nidx < i - 1, sel, 0.0)
        prev_sel.append(jnp.sum(jnp.where(nidx == i - 1, sel, 0.0), axis=0, keepdims=True) > 0.5)
        qs_all.append((qh * (dh ** -0.5 * LOG2E)).astype(BF16))
        carry0 += [jnp.full((1, blk), NEG, F32), jnp.zeros((1, blk), F32), jnp.zeros((dh, blk), F32)]

    def softmax_step(m, l, acc, lgs, sels, v_t):
        m_new = m
        for lg, sl in zip(lgs, sels):
            cm = jnp.max(lg, axis=0, keepdims=True)
            m_new = jnp.maximum(m_new, cm if sl is None else jnp.where(sl, cm, NEG))
        a = jnp.exp2(m - m_new)
        l = a * l
        ps = []
        for lg, sl in zip(lgs, sels):
            p = jnp.exp2(lg - (m_new if sl is None else jnp.where(sl, m_new, -NEG)))
            l = l + jnp.sum(p, axis=0, keepdims=True)
            ps.append(p.astype(BF16))
        return m_new, l, a * acc + _dot(v_t, jnp.concatenate(ps, axis=0))

    def body(j, carry):
        st = pl.multiple_of(j * (grp * blk), grp * blk)
        out = []
        for hh in range(2):
            lg_all = _dot_nt(kbf_ref[hh, pl.ds(st, grp * blk), :], qs_all[hh])
            lgs = [lg_all[g * blk:(g + 1) * blk] for g in range(grp)]
            sels = [sel_ref[hh, pl.ds(j * grp + g, 1), :] > 0.5 for g in range(grp)]
            out += softmax_step(*carry[3 * hh:3 * hh + 3], lgs, sels,
                                vt_ref[hh * dh:(hh + 1) * dh, pl.ds(st, grp * blk)])
        return tuple(out)

    fin = lax.fori_loop(0, (i + grp - 2) // grp, body, tuple(carry0))

    own = pl.multiple_of(i * blk, blk)
    prev = pl.multiple_of(jnp.maximum(i - 1, 0) * blk, blk)
    outs = []
    for hh in range(2):
        rows = slice(hh * dh, (hh + 1) * dh)
        lg_prev = _dot_nt(kbf_ref[hh, pl.ds(prev, blk), :], qs_all[hh]) + tb_ref[hh, 1]
        lg_own = _dot_nt(kbf_ref[hh, pl.ds(own, blk), :], qs_all[hh]) + tb_ref[hh, 0]
        v_t = jnp.concatenate([vt_ref[rows, pl.ds(prev, blk)], vt_ref[rows, pl.ds(own, blk)]], axis=1)
        _, l, acc = softmax_step(*fin[3 * hh:3 * hh + 3], [lg_prev, lg_own], [prev_sel[hh], None], v_t)
        outs.append((acc / l).T)
    o_ref[...] = jnp.concatenate(outs, axis=1).astype(BF16)


def _moba_prompt(z, bias_tab, n_seq, t):
    blk = MOBA_BLOCK
    nb = t // blk
    assert nb % MOBA_KEY_GROUP == 0
    hp = ATT_HEADS // 2
    kc, vc = OFF_AK // LANES, OFF_AV // LANES
    return pl.pallas_call(
        functools.partial(_moba_p_body, nb=nb),
        grid=(n_seq, hp, nb),
        in_specs=[pl.BlockSpec((blk, LANES), lambda b, h, i: (b * nb + i, h)),
                  pl.BlockSpec((t, LANES), lambda b, h, i: (b, kc + h)),
                  pl.BlockSpec((t, LANES), lambda b, h, i: (b, vc + h)),
                  pl.BlockSpec((2, 2, blk, blk), lambda b, h, i: (h, 0, 0, 0))],
        out_specs=pl.BlockSpec((blk, LANES), lambda b, h, i: (b * nb + i, h)),
        out_shape=jax.ShapeDtypeStruct((n_seq * t, ATT_W), BF16),
        scratch_shapes=[pltpu.VMEM((2, t, ATT_DH), BF16), pltpu.VMEM((LANES, t), BF16),
                        pltpu.VMEM((nb, LANES), F32), pltpu.VMEM((2, nb, blk), F32)],
        compiler_params=_params(("arbitrary", "arbitrary", "arbitrary")),
    )(z, z, z, bias_tab)


PAGES_PER_STEP = 8
PAGES_PER_BLOCK = MOBA_BLOCK // PAGE_SIZE


def _moba_s_body(pt_ref, *refs, n_steps):
    del pt_ref
    kpages, vpages = refs[:PAGES_PER_STEP], refs[PAGES_PER_STEP:2 * PAGES_PER_STEP]
    q_ref, kn_ref, vn_ref, rbt_ref, bk_ref, o_ref, m_s, l_s, s_s, acc_s = refs[2 * PAGES_PER_STEP:]
    g = pl.program_id(1)
    h, dh, w = ATT_HEADS, ATT_DH, ATT_W
    n_pages = n_steps * PAGES_PER_STEP
    row = lax.broadcasted_iota(jnp.int32, (h, w), 0)
    lane = lax.broadcasted_iota(jnp.int32, (h, w), 1)
    own_head = (lane // dh) == row
    qbd = jnp.where(own_head, q_ref[...] * (dh ** -0.5 * LOG2E), 0.0)
    q_hi = qbd.astype(BF16)
    q_lo = (qbd - q_hi.astype(F32)).astype(BF16)
    q2 = jnp.concatenate([q_hi, q_lo], axis=0)
    far_bias = rbt_ref[:, REL_BUCKETS - 1:REL_BUCKETS] * LOG2E

    def page(k_ref, v_ref, bias, idx):
        raw2 = _dot(q2, k_ref[...].reshape(w, PAGE_SIZE).astype(BF16))
        raw = raw2[:h] + raw2[h:]
        lg = raw if bias is None else raw + bias
        mb = jnp.max(lg, axis=1, keepdims=True)
        p = jnp.exp2(lg - mb)
        m_s[idx] = mb + far_bias if bias is None else mb
        l_s[idx] = jnp.sum(p, axis=1, keepdims=True)
        s_s[idx] = jnp.sum(raw, axis=1, keepdims=True)
        acc_s[idx] = _dot_nt(p.astype(BF16), v_ref[...].reshape(w, PAGE_SIZE).astype(BF16))

    last = PAGES_PER_STEP - 1
    for j in range(last):
        page(kpages[j], vpages[j], None, g * PAGES_PER_STEP + j)

    @pl.when(g != n_steps - 1)
    def _():
        page(kpages[last], vpages[last], None, g * PAGES_PER_STEP + last)

    @pl.when(g == n_steps - 1)
    def _():
        bk = bk_ref[...]
        near_bias = jnp.zeros((h, PAGE_SIZE), F32)
        for b in range(REL_BUCKETS):
            near_bias = jnp.where(bk == b, rbt_ref[:, b:b + 1], near_bias)
        page(kpages[last], vpages[last], near_bias * LOG2E, n_pages - 1)
        nblk = n_pages // PAGES_PER_BLOCK
        s = jnp.sum(s_s[...].reshape(nblk, PAGES_PER_BLOCK, h, 1), axis=1)
        nidx = lax.broadcasted_iota(jnp.int32, s.shape, 0)
        sel = _topk_select(s, nidx >= 0, nidx, nblk)
        selp = jnp.broadcast_to(sel[:, None], (nblk, PAGES_PER_BLOCK, h, 1)).reshape(n_pages, h, 1) > 0.5
        ln = jnp.sum(qbd * kn_ref[...], axis=1, keepdims=True) + rbt_ref[:, 0:1] * LOG2E
        mm = jnp.where(selp, m_s[...], NEG)
        mx = jnp.maximum(jnp.max(mm, axis=0), ln)
        wgt = jnp.where(selp, jnp.exp2(mm - mx), 0.0)
        wn = jnp.exp2(ln - mx)
        den = jnp.sum(wgt * l_s[...], axis=0) + wn
        tot = jnp.sum(wgt * acc_s[...], axis=0) + wn * vn_ref[...]
        o_ref[...] = jnp.sum(jnp.where(own_head, tot / den, 0.0), axis=0, keepdims=True)


def _moba_sample(z_s, cache_k, cache_v, layer, page_table, rel_bias_t):
    n_seq, n_pages = page_table.shape
    assert n_pages % PAGES_PER_STEP == 0 and PAGES_PER_STEP % PAGES_PER_BLOCK == 0
    n_steps = n_pages // PAGES_PER_STEP
    assert int(_bucket_np(np.array([PAGE_SIZE + 1]))[0]) == REL_BUCKETS - 1
    near_buckets = _bucket_np(PAGE_SIZE - np.arange(PAGE_SIZE)).reshape(1, PAGE_SIZE)
    rows = lambda off: z_s[:, off:off + ATT_W].reshape(n_seq, 1, ATT_W)

    def page_spec(j):
        return pl.BlockSpec((None, None, ATT_HEADS, ATT_DH, PAGE_SIZE),
                            lambda s, g, pt: (layer, pt[s, g * PAGES_PER_STEP + j], 0, 0, 0))

    row_spec = pl.BlockSpec((None, 1, ATT_W), lambda s, g, pt: (s, 0, 0))
    stat = pltpu.VMEM((n_pages, ATT_HEADS, 1), F32)
    grid_spec = pltpu.PrefetchScalarGridSpec(
        num_scalar_prefetch=1,
        grid=(n_seq, n_steps),
        in_specs=[page_spec(j) for j in range(PAGES_PER_STEP)] * 2 + [
            row_spec, row_spec, row_spec,
            pl.BlockSpec((ATT_HEADS, REL_BUCKETS), lambda s, g, pt: (0, 0)),
            pl.BlockSpec((1, PAGE_SIZE), lambda s, g, pt: (0, 0))],
        out_specs=row_spec,
        scratch_shapes=[stat, stat, stat, pltpu.VMEM((n_pages, ATT_HEADS, ATT_W), F32)],
    )
    out = pl.pallas_call(
        functools.partial(_moba_s_body, n_steps=n_steps),
        grid_spec=grid_spec,
        out_shape=jax.ShapeDtypeStruct((n_seq, 1, ATT_W), F32),
        compiler_params=_params(("arbitrary", "arbitrary")),
    )(page_table, *([cache_k] * PAGES_PER_STEP), *([cache_v] * PAGES_PER_STEP),
      rows(OFF_AQ), rows(OFF_AK), rows(OFF_AV), rel_bias_t, jnp.asarray(near_buckets))
    return out.reshape(n_seq, ATT_W)


def _rope_body(inv_ref, sgn_ref, cos_ref, sin_ref, *, pos0, step, rows):
    r = lax.broadcasted_iota(jnp.int32, (rows, LANES), 0) + pl.program_id(0) * rows
    ang = (pos0 + step * r).astype(F32) * inv_ref[...]
    cos_ref[...] = jnp.cos(ang)
    sin_ref[...] = jnp.sin(ang) * sgn_ref[...]


def _rope_tables(n_rows, pos0, step):
    half = RET_DK // 2
    inv = 1.0 / (ROPE_BASE ** jnp.linspace(0.0, 1.0, half))
    inv_row = jnp.tile(inv, LANES // half).reshape(1, LANES).astype(F32)
    sgn = np.where((np.arange(LANES) % RET_DK) < half, -1.0, 1.0).astype(np.float32).reshape(1, LANES)
    rows = min(n_rows, 512)
    spec = pl.BlockSpec((rows, LANES), lambda i: (i, 0))
    cst = pl.BlockSpec((1, LANES), lambda i: (0, 0))
    return pl.pallas_call(
        functools.partial(_rope_body, pos0=pos0, step=step, rows=rows),
        grid=(n_rows // rows,),
        in_specs=[cst, cst],
        out_specs=[spec, spec],
        out_shape=[jax.ShapeDtypeStruct((n_rows, LANES), F32)] * 2,
        compiler_params=_params(("arbitrary",)),
    )(inv_row, jnp.asarray(sgn))


def _rotary128(x, cos, sin_signed):
    half = RET_DK // 2
    lane = lax.broadcasted_iota(jnp.int32, x.shape, 1)
    partner = jnp.where((lane % RET_DK) < half,
                        pltpu.roll(x, LANES - half, 1), pltpu.roll(x, half, 1))
    return x * cos + partner * sin_signed


def _groupnorm_gate(o, g):
    mu = jnp.mean(o, axis=-1, keepdims=True)
    var = jnp.mean((o - mu) ** 2, axis=-1, keepdims=True)
    return _silu(g) * ((o - mu) * lax.rsqrt(var + GN_EPS))


def _ret_log_decay():
    return jnp.log(1.0 - 2.0 ** (-5.0 - jnp.arange(RET_HEADS, dtype=F32)))


def _ret_p_body(lg_ref, q_ref, k_ref, v_ref, g_ref, cos_ref, sin_ref, o_ref, st_ref, *, chunk):
    hp = pl.program_id(1)
    c = pl.program_id(2)

    @pl.when(c == 0)
    def _():
        st_ref[...] = jnp.zeros(st_ref.shape, F32)

    cos, sin = cos_ref[...], sin_ref[...]
    q = _rotary128(q_ref[...], cos, sin)
    k = _rotary128(k_ref[...], cos, sin) * RET_DK ** -0.5
    ii = lax.broadcasted_iota(jnp.int32, (chunk, chunk), 0)
    jj = lax.broadcasted_iota(jnp.int32, (chunk, chunk), 1)
    diff = (ii - jj).astype(F32)
    ri = lax.broadcasted_iota(jnp.int32, (chunk, 1), 0).astype(F32)
    for hh in range(2):
        lgh = lg_ref[hp * 2 + hh]
        dmask = jnp.where(diff >= 0, jnp.exp(lgh * jnp.maximum(diff, 0.0)), 0.0)
        qh = q[:, hh * RET_DK:(hh + 1) * RET_DK].astype(BF16)
        kh = k[:, hh * RET_DK:(hh + 1) * RET_DK]
        vh = v_ref[:, hh * RET_DV:(hh + 1) * RET_DV].astype(BF16)
        s0 = st_ref[hh]
        a = _dot_nt(qh, kh.astype(BF16)) * dmask
        inner = _dot(a.astype(BF16), vh)
        cross = _dot(qh, s0.astype(BF16)) * jnp.exp(lgh * (ri + 1.0))
        kdec = (kh * jnp.exp(lgh * (chunk - 1.0 - ri))).astype(BF16)
        st_ref[hh] = jnp.exp(lgh * chunk + jnp.zeros((1, 1), F32)) * s0 + _dot_tn(kdec, vh)
        gh = g_ref[:, hh * RET_DV:(hh + 1) * RET_DV]
        o_ref[:, hh * RET_DV:(hh + 1) * RET_DV] = _groupnorm_gate(inner + cross, gh).astype(BF16)


def _ret_prompt(z, cos_tab, sin_tab, log_decay, n_seq, t):
    chunk = RET_CHUNK
    nc = t // chunk
    hp = RET_HEADS // 2
    qc, kc = OFF_RQ // LANES, OFF_RK // LANES
    vc, gc = OFF_RV // (2 * RET_DV), OFF_RG // (2 * RET_DV)
    return pl.pallas_call(
        functools.partial(_ret_p_body, chunk=chunk),
        grid=(n_seq, hp, nc),
        in_specs=[pl.BlockSpec(memory_space=pltpu.SMEM),
                  pl.BlockSpec((chunk, LANES), lambda b, h, c: (b * nc + c, qc + h)),
                  pl.BlockSpec((chunk, LANES), lambda b, h, c: (b * nc + c, kc + h)),
                  pl.BlockSpec((chunk, 2 * RET_DV), lambda b, h, c: (b * nc + c, vc + h)),
                  pl.BlockSpec((chunk, 2 * RET_DV), lambda b, h, c: (b * nc + c, gc + h)),
                  pl.BlockSpec((chunk, LANES), lambda b, h, c: (c, 0)),
                  pl.BlockSpec((chunk, LANES), lambda b, h, c: (c, 0))],
        out_specs=[pl.BlockSpec((chunk, 2 * RET_DV), lambda b, h, c: (b * nc + c, h)),
                   pl.BlockSpec((None, 2, RET_DK, RET_DV), lambda b, h, c: (b, h, 0, 0))],
        out_shape=[jax.ShapeDtypeStruct((n_seq * t, RET_V_W), BF16),
                   jax.ShapeDtypeStruct((n_seq, RET_HEADS, RET_DK, RET_DV), F32)],
        compiler_params=_params(("arbitrary", "arbitrary", "arbitrary")),
    )(log_decay, z, z, z, z, cos_tab, sin_tab)


def _ret_s_prep_body(q_ref, k_ref, cos_ref, sin_ref, qo_ref, ko_ref):
    cos, sin = cos_ref[0:1, :], sin_ref[0:1, :]
    for j in range(RET_QK_W // LANES):
        sl = slice(j * LANES, (j + 1) * LANES)
        qo_ref[:, sl] = _rotary128(q_ref[:, sl], cos, sin)
        ko_ref[:, sl] = _rotary128(k_ref[:, sl], cos, sin) * RET_DK ** -0.5


def _ret_s_body(lg_ref, q_ref, k_ref, v_ref, g_ref, s0_ref, o_ref, sn_ref):
    gam = jnp.exp(lg_ref[pl.program_id(1)] + jnp.zeros((1, 1, LANES), F32))
    q, k, v, s0 = q_ref[...], k_ref[...], v_ref[...], s0_ref[...]
    o = jnp.sum(q * s0, axis=1, keepdims=True) * gam + jnp.sum(q * k, axis=1, keepdims=True) * v
    sn_ref[...] = gam * s0 + k * v
    o_ref[...] = _groupnorm_gate(o, g_ref[...])


def _ret_sample(z_s, state, cos_tab, sin_tab, log_decay):
    n_seq = z_s.shape[0]
    tab = pl.BlockSpec((SUBLANES, LANES), lambda i: (0, 0))
    q_rot, k_rot = pl.pallas_call(
        _ret_s_prep_body,
        grid=(1,),
        in_specs=[pl.BlockSpec((n_seq, RET_QK_W), lambda i: (0, OFF_RQ // RET_QK_W)),
                  pl.BlockSpec((n_seq, RET_QK_W), lambda i: (0, OFF_RK // RET_QK_W)), tab, tab],
        out_specs=[pl.BlockSpec((n_seq, RET_QK_W), lambda i: (0, 0))] * 2,
        out_shape=[jax.ShapeDtypeStruct((n_seq, RET_QK_W), F32)] * 2,
        compiler_params=_params(("arbitrary",)),
    )(z_s, z_s, cos_tab, sin_tab)
    col = lambda a: a.reshape(n_seq, RET_HEADS, RET_DK, 1)
    row = lambda a: a.reshape(n_seq, RET_HEADS, 1, RET_DV)
    bt = 16
    col_spec = pl.BlockSpec((bt, None, RET_DK, 1), lambda i, h: (i, h, 0, 0))
    row_spec = pl.BlockSpec((bt, None, 1, RET_DV), lambda i, h: (i, h, 0, 0))
    st_spec = pl.BlockSpec((bt, None, RET_DK, RET_DV), lambda i, h: (i, h, 0, 0))
    o, s_new = pl.pallas_call(
        _ret_s_body,
        grid=(n_seq // bt, RET_HEADS),
        in_specs=[pl.BlockSpec(memory_space=pltpu.SMEM), col_spec, col_spec, row_spec, row_spec, st_spec],
        out_specs=[row_spec, st_spec],
        out_shape=[jax.ShapeDtypeStruct((n_seq, RET_HEADS, 1, RET_DV), F32),
                   jax.ShapeDtypeStruct((n_seq, RET_HEADS, RET_DK, RET_DV), F32)],
        compiler_params=_params(("arbitrary", "arbitrary")),
    )(log_decay, col(q_rot), col(k_rot), row(z_s[:, OFF_RV:OFF_RV + RET_V_W]),
      row(z_s[:, OFF_RG:OFF_RG + RET_V_W]), state)
    return o.reshape(n_seq, RET_V_W), s_new


def _s5_prep_body(are_ref, aim_ref, ldt_ref, bre_ref, bim_ref, pwr_ref, pwi_ref, bbr_ref, bbi_ref):
    ar, ai = are_ref[...], aim_ref[...]
    dt = jnp.exp(ldt_ref[...])
    kk = (lax.broadcasted_iota(jnp.int32, pwr_ref.shape, 1) + 1).astype(F32)
    mag = jnp.exp(ar * dt * kk)
    ang = ai * dt * kk
    pwr_ref[...] = mag * jnp.cos(ang)
    pwi_ref[...] = mag * jnp.sin(ang)
    mag1 = jnp.exp(ar * dt)
    nr = mag1 * jnp.cos(ai * dt) - 1.0
    ni = mag1 * jnp.sin(ai * dt)
    den = ar * ar + ai * ai
    cr = (nr * ar + ni * ai) / den
    ci = (ni * ar - nr * ai) / den
    bre, bim = bre_ref[...], bim_ref[...]
    bbr_ref[...] = cr * bre - ci * bim
    bbi_ref[...] = cr * bim + ci * bre


def _s5_prepare(a_re, a_im, log_dt, b_re, b_im, c_re, c_im, d):
    g, p, c = S5_GROUPS, S5_STATE, S5_GROUP
    g3 = lambda a: a.reshape(g, 1, p)
    ldt = jnp.broadcast_to(log_dt.reshape(g, 1, 1), (g, 1, p))
    pwr, pwi, bbr, bbi = pl.pallas_call(
        _s5_prep_body,
        out_shape=[jax.ShapeDtypeStruct((g, SUBLANES, p), F32)] * 2
        + [jax.ShapeDtypeStruct((g, c, p), F32)] * 2,
    )(g3(a_re), g3(a_im), ldt, b_re.transpose(0, 2, 1), b_im.transpose(0, 2, 1))
    eye = jnp.eye(S5_GT, dtype=F32)
    nt = S5_NT

    def in_tile(bb):
        return jnp.einsum("tgcp,gh->tgchp", bb.reshape(nt, S5_GT, c, p), eye).reshape(nt, S5_GT * c, S5_GT * p)

    def out_tile(cc):
        return jnp.einsum("tgcp,gh->tgphc", cc.reshape(nt, S5_GT, c, p), eye).reshape(nt, S5_GT * p, S5_GT * c)

    b_tile = jnp.concatenate([in_tile(bbr), in_tile(bbi)], axis=2).astype(BF16)
    c_tile = jnp.concatenate([out_tile(c_re), out_tile(-c_im)], axis=1).astype(BF16)
    d_tile = d.reshape(nt, 1, S5_GT * c)

    def pw_tile(pw):
        return pw.reshape(nt, S5_GT, SUBLANES, p).transpose(0, 2, 1, 3).reshape(nt, SUBLANES, S5_GT * p)

    pr, pi = pw_tile(pwr), pw_tile(pwi)
    rows = jnp.arange(SUBLANES)[None, :, None]
    slabs = []
    for shift in (1, 2, 4):
        keep = rows >= shift
        slabs += [jnp.where(keep, pr[:, shift - 1:shift, :], 0.0), jnp.where(keep, pi[:, shift - 1:shift, :], 0.0)]
    scan_c = jnp.stack(slabs + [pr, pi], axis=1)
    lam1 = jnp.stack([pr[:, 0:1, :], pi[:, 0:1, :]], axis=1)
    return b_tile, c_tile, d_tile, scan_c, lam1


def _gelu_tanh(y):
    return 0.5 * y * (1.0 + jnp.tanh(math.sqrt(2.0 / math.pi) * (y + 0.044715 * (y * y * y))))


def _s5_p_body(u_ref, bt_ref, ct_ref, d_ref, sc_ref, z_ref, st_ref, x_ref, carry_ref, *, tc):
    ns = S5_TILE_STATES

    @pl.when(pl.program_id(2) == 0)
    def _():
        carry_ref[...] = jnp.zeros(carry_ref.shape, F32)

    u = u_ref[...]
    x_ref[...] = _dot(u.astype(BF16), bt_ref[...])

    def tile(t, carry):
        cr, ci = carry
        st = pl.multiple_of(t * SUBLANES, SUBLANES)
        xr = x_ref[pl.ds(st, SUBLANES), 0:ns]
        xi = x_ref[pl.ds(st, SUBLANES), ns:2 * ns]
        for s, shift in enumerate((1, 2, 4)):
            ar, ai = sc_ref[2 * s], sc_ref[2 * s + 1]
            sr, si = pltpu.roll(xr, shift, 0), pltpu.roll(xi, shift, 0)
            xr, xi = xr + ar * sr - ai * si, xi + ar * si + ai * sr
        pr, pi = sc_ref[6], sc_ref[7]
        xr, xi = xr + pr * cr - pi * ci, xi + pr * ci + pi * cr
        x_ref[pl.ds(st, SUBLANES), 0:ns] = xr
        x_ref[pl.ds(st, SUBLANES), ns:2 * ns] = xi
        return xr[SUBLANES - 1:SUBLANES, :], xi[SUBLANES - 1:SUBLANES, :]

    cr, ci = lax.fori_loop(0, tc // SUBLANES, tile, (carry_ref[:, 0:ns], carry_ref[:, ns:2 * ns]))
    last = jnp.concatenate([cr, ci], axis=1)
    carry_ref[...] = last
    st_ref[...] = last
    y = _dot(x_ref[...].astype(BF16), ct_ref[...]) + d_ref[...] * u
    z_ref[...] = _gelu_tanh(y)


def _s5_prompt(z, tiles, n_seq, t):
    b_tile, c_tile, d_tile, scan_c, _ = tiles
    tc = 512
    nc = t // tc
    uc = OFF_SU // LANES
    ns2 = 2 * S5_TILE_STATES
    zs, st = pl.pallas_call(
        functools.partial(_s5_p_body, tc=tc),
        grid=(n_seq, S5_NT, nc),
        in_specs=[pl.BlockSpec((tc, LANES), lambda b, g, c: (b * nc + c, uc + g)),
                  pl.BlockSpec((None, LANES, ns2), lambda b, g, c: (g, 0, 0)),
                  pl.BlockSpec((None, ns2, LANES), lambda b, g, c: (g, 0, 0)),
                  pl.BlockSpec((None, 1, LANES), lambda b, g, c: (g, 0, 0)),
                  pl.BlockSpec((None, 8, SUBLANES, S5_TILE_STATES), lambda b, g, c: (g, 0, 0, 0))],
        out_specs=[pl.BlockSpec((tc, LANES), lambda b, g, c: (b * nc + c, g)),
                   pl.BlockSpec((None, None, 1, ns2), lambda b, g, c: (b, g, 0, 0))],
        out_shape=[jax.ShapeDtypeStruct((n_seq * t, S5_W), F32),
                   jax.ShapeDtypeStruct((n_seq, S5_NT, 1, ns2), F32)],
        scratch_shapes=[pltpu.VMEM((tc, ns2), F32), pltpu.VMEM((1, ns2), F32)],
        compiler_params=_params(("arbitrary", "arbitrary", "arbitrary")),
    )(z, b_tile, c_tile, d_tile, scan_c)
    s_re = st[:, :, 0, :S5_TILE_STATES].reshape(n_seq, S5_GROUPS, S5_STATE)
    s_im = st[:, :, 0, S5_TILE_STATES:].reshape(n_seq, S5_GROUPS, S5_STATE)
    return zs, s_re, s_im


def _s5_s_body(u_ref, bt_ref, ct_ref, d_ref, l1_ref, x0_ref, z_ref, xn_ref):
    ns = S5_TILE_STATES
    u = u_ref[...]
    bu = _dot(u.astype(BF16), bt_ref[...])
    lr, li = l1_ref[0], l1_ref[1]
    x0r, x0i = x0_ref[:, 0:ns], x0_ref[:, ns:2 * ns]
    x = jnp.concatenate([bu[:, 0:ns] + lr * x0r - li * x0i, bu[:, ns:2 * ns] + lr * x0i + li * x0r], axis=1)
    xn_ref[...] = x
    z_ref[...] = _gelu_tanh(_dot(x.astype(BF16), ct_ref[...]) + d_ref[...] * u)


def _s5_sample(z_s, tiles, x0_re, x0_im):
    b_tile, c_tile, d_tile, _, lam1 = tiles
    n_seq = z_s.shape[0]
    uc = OFF_SU // LANES
    ns2 = 2 * S5_TILE_STATES

    def to_tiles(a):
        return a.reshape(n_seq, S5_NT, S5_TILE_STATES).transpose(1, 0, 2)

    x0 = jnp.concatenate([to_tiles(x0_re), to_tiles(x0_im)], axis=2)
    zs, xn = pl.pallas_call(
        _s5_s_body,
        grid=(S5_NT,),
        in_specs=[pl.BlockSpec((n_seq, LANES), lambda g: (0, uc + g)),
                  pl.BlockSpec((None, LANES, ns2), lambda g: (g, 0, 0)),
                  pl.BlockSpec((None, ns2, LANES), lambda g: (g, 0, 0)),
                  pl.BlockSpec((None, 1, LANES), lambda g: (g, 0, 0)),
                  pl.BlockSpec((None, 2, 1, S5_TILE_STATES), lambda g: (g, 0, 0, 0)),
                  pl.BlockSpec((None, n_seq, ns2), lambda g: (g, 0, 0))],
        out_specs=[pl.BlockSpec((n_seq, LANES), lambda g: (0, g)),
                   pl.BlockSpec((None, n_seq, ns2), lambda g: (g, 0, 0))],
        out_shape=[jax.ShapeDtypeStruct((n_seq, S5_W), F32),
                   jax.ShapeDtypeStruct((S5_NT, n_seq, ns2), F32)],
        compiler_params=_params(("arbitrary",)),
    )(z_s, b_tile, c_tile, d_tile, lam1, x0)

    def from_tiles(a):
        return a.transpose(1, 0, 2).reshape(n_seq, S5_GROUPS, S5_STATE)

    return zs, from_tiles(xn[:, :, :S5_TILE_STATES]), from_tiles(xn[:, :, S5_TILE_STATES:])


def _glu_body(zf_ref, zc_ref, w_ref, o_ref):
    o_ref[...] = (zc_ref[...] * _sigmoid(_dot(zf_ref[...].astype(BF16), w_ref[...]))).astype(BF16)


def _half_glu(zs, w_bf, tm, tn):
    m, w = zs.shape
    return pl.pallas_call(
        _glu_body,
        grid=(m // tm, w // tn),
        in_specs=[pl.BlockSpec((tm, w), lambda i, j: (i, 0)),
                  pl.BlockSpec((tm, tn), lambda i, j: (i, j)),
                  pl.BlockSpec((w, tn), lambda i, j: (0, j))],
        out_specs=pl.BlockSpec((tm, tn), lambda i, j: (i, j)),
        out_shape=jax.ShapeDtypeStruct((m, w), BF16),
        compiler_params=_params(("arbitrary", "arbitrary")),
    )(zs, zs, w_bf)


def _layer(x, mods, lw, mix, kind, tm, tiles_per_seq):
    norm1, norm2, w_in, w_branch, w_out, w_ffn_in, w_ffn_out = lw
    sh1, sc1, g1, sh2, sc2, g2 = mods
    z = _norm_mod_matmul(x, norm1, sc1, sh1, w_in, kind, tm, 512, tiles_per_seq)
    (o_att, o_ret, o_s5), state = mix(z)
    merged = _branch_merge(o_att, o_ret, o_s5, z, w_branch, tm, 512)
    x = _matmul_residual(merged, w_out, x, g1, kind, tm, 512, tiles_per_seq)
    act = _norm_mod_ffn_in(x, norm2, sc2, sh2, w_ffn_in, kind, tm, 512, tiles_per_seq)
    tm_out = min(tm, 512)
    x = _matmul_residual(act, w_ffn_out, x, g2, kind, tm_out, 512, tiles_per_seq * (tm // tm_out))
    return x, z, state


def kernel(x_prompt, x_sample, c_prompt, c_sample, cache_k, cache_v, page_table, state_ret, state_s5_re, state_s5_im, rel_bias, norm1_g, norm2_g, w_ada, b_ada, w_in, s5_a_re, s5_a_im, s5_log_dt, s5_b_re, s5_b_im, s5_c_re, s5_c_im, s5_d, w_glu, w_branch, w_out, w_ffn_in, w_ffn_out, final_g):
    n_seq, t, d = x_prompt.shape
    n_dec = x_sample.shape[0]
    depth = w_in.shape[0]
    past = page_table.shape[1] * PAGE_SIZE
    xp = x_prompt.reshape(n_seq * t, d)
    xs = x_sample.reshape(n_dec, d)
    tm_p = 1024
    tiles_per_seq = t // tm_p

    bias_p = _bias_tables_prompt(rel_bias)
    rel_bias_t = rel_bias.T
    cache_kt = cache_k.transpose(0, 1, 3, 4, 2)
    cache_vt = cache_v.transpose(0, 1, 3, 4, 2)
    cos_p, sin_p = _rope_tables(t, 0, 1)
    cos_s, sin_s = _rope_tables(SUBLANES, past, 0)
    log_decay = _ret_log_decay()
    pad = (-(n_seq + n_dec)) % SUBLANES
    c_all = jnp.concatenate([c_prompt, c_sample, jnp.zeros((pad, d), F32)], axis=0)

    new_p, new_s = [], []
    for l in range(depth):
        mod = _ada(c_all, w_ada[l].astype(BF16), b_ada[l])
        mod_p = mod[:n_seq].reshape(n_seq, 6, 1, d)
        mod_s = mod[n_seq:n_seq + n_dec].reshape(n_dec, 6, d)
        mods_p = tuple(mod_p[:, i] for i in range(6))
        mods_s = tuple(mod_s[:, i] for i in range(6))
        lw = (norm1_g[l], norm2_g[l], w_in[l].astype(BF16), w_branch[l].astype(BF16), w_out[l].astype(BF16),
              w_ffn_in[l].astype(BF16), w_ffn_out[l].astype(BF16))
        glu_w = w_glu[l].astype(BF16)
        tiles = _s5_prepare(s5_a_re[l], s5_a_im[l], s5_log_dt[l], s5_b_re[l], s5_b_im[l],
                            s5_c_re[l], s5_c_im[l], s5_d[l])

        def mix_p(z):
            o_att = _moba_prompt(z, bias_p, n_seq, t)
            o_ret, s_ret = _ret_prompt(z, cos_p, sin_p, log_decay, n_seq, t)
            zs5, s_re, s_im = _s5_prompt(z, tiles, n_seq, t)
            return (o_att, o_ret, _half_glu(zs5, glu_w, tm_p, 512)), (s_ret, s_re, s_im)

        def mix_s(z, l=l):
            o_att = _moba_sample(z, cache_kt, cache_vt, l, page_table, rel_bias_t)
            o_ret, s_ret = _ret_sample(z, state_ret[l], cos_s, sin_s, log_decay)
            zs5, s_re, s_im = _s5_sample(z, tiles, state_s5_re[l], state_s5_im[l])
            return (o_att, o_ret, _half_glu(zs5, glu_w, n_dec, 512)), (s_ret, s_re, s_im)

        xp, zp, st_p = _layer(xp, mods_p, lw, mix_p, "prompt", tm_p, tiles_per_seq)
        xs, zs, st_s = _layer(xs, mods_s, lw, mix_s, "sample", n_dec, 1)
        kv = lambda z, n, tt, off: z[:, off:off + ATT_W].reshape(n, tt, ATT_HEADS, ATT_DH)
        new_p.append((kv(zp, n_seq, t, OFF_AK), kv(zp, n_seq, t, OFF_AV)) + st_p)
        new_s.append((kv(zs, n_dec, 1, OFF_AK), kv(zs, n_dec, 1, OFF_AV)) + st_s)

    y_prompt = _final_norm(xp, final_g, tm_p).reshape(n_seq, t, d)
    y_sample = _final_norm(xs, final_g, n_dec).reshape(n_dec, 1, d)
    outs_p = [jnp.stack(a) for a in zip(*new_p)]
    outs_s = [jnp.stack(a) for a in zip(*new_s)]
    return (y_prompt, y_sample, *outs_p, *outs_s)
```

```python
import functools
import math

import numpy as np
import jax
import jax.numpy as jnp
from jax import lax
from jax.experimental import pallas as pl
from jax.experimental.pallas import tpu as pltpu

F32 = jnp.float32
BF16 = jnp.bfloat16
HIGHEST = lax.Precision.HIGHEST

D_MODEL = 2048
PAGE_SIZE = 128
ATT_HEADS = 16
ATT_DH = 64
ATT_W = ATT_HEADS * ATT_DH
MOBA_BLOCK = 256
MOBA_TOPK = 3
REL_BUCKETS = 32
REL_MAX_DIST = 128
RET_HEADS = 8
RET_DK = 64
RET_DV = 128
RET_QK_W = RET_HEADS * RET_DK
RET_V_W = RET_HEADS * RET_DV
RET_CHUNK = 256
ROPE_BASE = 10000.0
S5_W = 1024
S5_GROUP = 16
S5_GROUPS = S5_W // S5_GROUP
S5_STATE = 64
N_BRANCH = 3
BRANCH_W = 1024
D_FF = ((8 * D_MODEL + 3 * 256 - 1) // (3 * 256)) * 256
NORM_EPS = 1e-6
GN_EPS = 1e-5
D_IN = 3 * ATT_W + 2 * RET_QK_W + 2 * RET_V_W + S5_W + N_BRANCH * D_MODEL

OFF_AQ, OFF_AK, OFF_AV = 0, ATT_W, 2 * ATT_W
OFF_RQ = 3 * ATT_W
OFF_RK = OFF_RQ + RET_QK_W
OFF_RV = OFF_RK + RET_QK_W
OFF_RG = OFF_RV + RET_V_W
OFF_SU = OFF_RG + RET_V_W
OFF_GATE = OFF_SU + S5_W

LANES = 128
SUBLANES = 8
VMEM_LIMIT = 56 * 1024 * 1024
NEG = -1e30
LOG2E = math.log2(math.e)
MOBA_KEY_GROUP = 4
S5_GT = 8
S5_TILE_STATES = S5_GT * S5_STATE
S5_NT = S5_GROUPS // S5_GT


def _params(sem):
    return pltpu.CompilerParams(dimension_semantics=sem, vmem_limit_bytes=VMEM_LIMIT)


def _sigmoid(x):
    return 1.0 / (1.0 + jnp.exp(-x))


def _silu(x):
    return x * _sigmoid(x)


def _dot(a, b):
    return jnp.dot(a, b, preferred_element_type=F32)


def _dot_nt(a, b, precision=None):
    return lax.dot_general(a, b, (((1,), (1,)), ((), ())), precision=precision,
                           preferred_element_type=F32)


def _dot_tn(a, b):
    return lax.dot_general(a, b, (((0,), (0,)), ((), ())), preferred_element_type=F32)


def _mod_spec(kind, tm, tn, tiles_per_seq, col_blocked):
    if kind == "prompt":
        if col_blocked:
            return pl.BlockSpec((None, 1, tn), lambda i, j: (i // tiles_per_seq, 0, j))
        return pl.BlockSpec((None, 1, tn), lambda i, j: (i // tiles_per_seq, 0, 0))
    if col_blocked:
        return pl.BlockSpec((tm, tn), lambda i, j: (i, j))
    return pl.BlockSpec((tm, tn), lambda i, j: (i, 0))


def _ada_body(c_ref, w_ref, b_ref, o_ref):
    a = _silu(c_ref[...]).astype(BF16)
    o_ref[...] = _dot(a, w_ref[...].astype(BF16)) + b_ref[...]


def _ada(c_all, w_all, b, layer):
    m, d = c_all.shape
    n = w_all.shape[2]
    tn = 1024
    return pl.pallas_call(
        _ada_body,
        grid=(n // tn,),
        in_specs=[pl.BlockSpec((m, d), lambda j: (0, 0)),
                  pl.BlockSpec((None, d, tn), lambda j: (layer, 0, j)),
                  pl.BlockSpec((1, tn), lambda j: (0, j))],
        out_specs=pl.BlockSpec((m, tn), lambda j: (0, j)),
        out_shape=jax.ShapeDtypeStruct((m, n), F32),
        compiler_params=_params(("arbitrary",)),
    )(c_all, w_all, b.reshape(1, n))


def _norm_mod(x, g, sc, sh):
    y = x * lax.rsqrt(jnp.mean(x * x, axis=-1, keepdims=True) + NORM_EPS)
    return (y * g) * (1.0 + sc) + sh


def _nmm_body(x_ref, g_ref, sc_ref, sh_ref, w_ref, o_ref, h_ref):
    @pl.when(pl.program_id(1) == 0)
    def _():
        h_ref[...] = _norm_mod(x_ref[...], g_ref[...], sc_ref[...], sh_ref[...]).astype(BF16)

    o_ref[...] = _dot(h_ref[...], w_ref[...])


def _norm_mod_matmul(x, g, sc, sh, w, kind, tm, tn, tiles_per_seq):
    m, d = x.shape
    w_bf, layer = w
    n = w_bf.shape[2]
    mod = _mod_spec(kind, tm, d, tiles_per_seq, False)
    return pl.pallas_call(
        _nmm_body,
        grid=(m // tm, n // tn),
        in_specs=[pl.BlockSpec((tm, d), lambda i, j: (i, 0)),
                  pl.BlockSpec((1, d), lambda i, j: (0, 0)),
                  mod, mod,
                  pl.BlockSpec((None, d, tn), lambda i, j: (layer, 0, j))],
        out_specs=pl.BlockSpec((tm, tn), lambda i, j: (i, j)),
        out_shape=jax.ShapeDtypeStruct((m, n), F32),
        scratch_shapes=[pltpu.VMEM((tm, d), BF16)],
        compiler_params=_params(("arbitrary", "arbitrary")),
    )(x, g.reshape(1, d), sc, sh, w_bf)


def _ffn_in_body(x_ref, g_ref, sc_ref, sh_ref, w1_ref, w2_ref, o_ref, h_ref):
    @pl.when(pl.program_id(1) == 0)
    def _():
        h_ref[...] = _norm_mod(x_ref[...], g_ref[...], sc_ref[...], sh_ref[...]).astype(BF16)

    h = h_ref[...]
    o_ref[...] = (_silu(_dot(h, w1_ref[...])) * _dot(h, w2_ref[...])).astype(BF16)


def _norm_mod_ffn_in(x, g, sc, sh, w, kind, tm, tn, tiles_per_seq):
    m, d = x.shape
    w_bf, layer = w
    nj = D_FF // tn
    mod = _mod_spec(kind, tm, d, tiles_per_seq, False)
    return pl.pallas_call(
        _ffn_in_body,
        grid=(m // tm, nj),
        in_specs=[pl.BlockSpec((tm, d), lambda i, j: (i, 0)),
                  pl.BlockSpec((1, d), lambda i, j: (0, 0)),
                  mod, mod,
                  pl.BlockSpec((None, d, tn), lambda i, j: (layer, 0, j)),
                  pl.BlockSpec((None, d, tn), lambda i, j: (layer, 0, j + nj))],
        out_specs=pl.BlockSpec((tm, tn), lambda i, j: (i, j)),
        out_shape=jax.ShapeDtypeStruct((m, D_FF), BF16),
        scratch_shapes=[pltpu.VMEM((tm, d), BF16)],
        compiler_params=_params(("arbitrary", "arbitrary")),
    )(x, g.reshape(1, d), sc, sh, w_bf, w_bf)


def _mmres_body(a_ref, w_ref, x_ref, g_ref, o_ref):
    o_ref[...] = x_ref[...] + g_ref[...] * _dot(a_ref[...], w_ref[...])


def _matmul_residual(a_bf, w, x, gate, kind, tm, tn, tiles_per_seq):
    m, k = a_bf.shape
    w_bf, layer = w
    n = w_bf.shape[2]
    return pl.pallas_call(
        _mmres_body,
        grid=(m // tm, n // tn),
        in_specs=[pl.BlockSpec((tm, k), lambda i, j: (i, 0)),
                  pl.BlockSpec((None, k, tn), lambda i, j: (layer, 0, j)),
                  pl.BlockSpec((tm, tn), lambda i, j: (i, j)),
                  _mod_spec(kind, tm, tn, tiles_per_seq, True)],
        out_specs=pl.BlockSpec((tm, tn), lambda i, j: (i, j)),
        out_shape=jax.ShapeDtypeStruct((m, n), F32),
        compiler_params=_params(("arbitrary", "arbitrary")),
    )(a_bf, w_bf, x, gate)


def _merge_body(oa_ref, or_ref, os_ref, wb_ref, ga_ref, gr_ref, gs_ref, o_ref):
    acc = None
    for n, (o_r, g_r) in enumerate(((oa_ref, ga_ref), (or_ref, gr_ref), (os_ref, gs_ref))):
        term = _sigmoid(g_r[...]) * _dot(o_r[...].astype(BF16), wb_ref[n])
        acc = term if acc is None else acc + term
    o_ref[...] = acc.astype(BF16)


def _branch_merge(o_att, o_ret, o_s5, z, w, tm, tn):
    m = z.shape[0]
    wb_bf, layer = w
    gate0 = OFF_GATE // tn
    per = D_MODEL // tn
    o_spec = pl.BlockSpec((tm, BRANCH_W), lambda i, j: (i, 0))

    def gate_spec(n):
        return pl.BlockSpec((tm, tn), lambda i, j: (i, gate0 + n * per + j))

    return pl.pallas_call(
        _merge_body,
        grid=(m // tm, per),
        in_specs=[o_spec, o_spec, o_spec,
                  pl.BlockSpec((None, N_BRANCH, BRANCH_W, tn), lambda i, j: (layer, 0, 0, j)),
                  gate_spec(0), gate_spec(1), gate_spec(2)],
        out_specs=pl.BlockSpec((tm, tn), lambda i, j: (i, j)),
        out_shape=jax.ShapeDtypeStruct((m, D_MODEL), BF16),
        compiler_params=_params(("arbitrary", "arbitrary")),
    )(o_att, o_ret, o_s5, wb_bf, z, z, z)


def _fnorm_body(x_ref, g_ref, o_ref):
    x = x_ref[...]
    o_ref[...] = x * lax.rsqrt(jnp.mean(x * x, axis=-1, keepdims=True) + NORM_EPS) * g_ref[...]


def _final_norm(x, g, tm):
    m, d = x.shape
    return pl.pallas_call(
        _fnorm_body,
        grid=(m // tm,),
        in_specs=[pl.BlockSpec((tm, d), lambda i: (i, 0)), pl.BlockSpec((1, d), lambda i: (0, 0))],
        out_specs=pl.BlockSpec((tm, d), lambda i: (i, 0)),
        out_shape=jax.ShapeDtypeStruct((m, d), F32),
        compiler_params=_params(("arbitrary",)),
    )(x, g.reshape(1, d))


def _bucket_np(dist):
    n = np.maximum(dist, 0)
    max_exact = REL_BUCKETS // 2
    nf = np.maximum(n, 1).astype(np.float32)
    large = max_exact + (np.log(nf / max_exact) / math.log(REL_MAX_DIST / max_exact)
                         * (REL_BUCKETS - max_exact)).astype(np.int32)
    large = np.minimum(large, REL_BUCKETS - 1)
    return np.where(n < max_exact, n, large).astype(np.int32)


def _bias_p_body(rb_ref, bk_ref, o_ref):
    h = pl.program_id(0)
    blk = MOBA_BLOCK
    far = rb_ref[REL_BUCKETS - 1, h]
    for s in range(2):
        bk = bk_ref[s]
        acc = jnp.zeros((blk, blk), F32)
        for b in range(REL_BUCKETS):
            acc = jnp.where(bk == b, rb_ref[b, h], acc)
        acc = (acc - far) * LOG2E
        if s == 0:
            rk = lax.broadcasted_iota(jnp.int32, (blk, blk), 0)
            rq = lax.broadcasted_iota(jnp.int32, (blk, blk), 1)
            acc = jnp.where(rk <= rq, acc, 2 * NEG)
        o_ref[s] = acc


def _bias_tables_prompt(rel_bias):
    blk = MOBA_BLOCK
    rk = np.arange(blk)[:, None]
    rq = np.arange(blk)[None, :]
    buckets = np.stack([_bucket_np(rq - rk), _bucket_np(blk + rq - rk)])
    assert int(_bucket_np(np.array([blk + 1]))[0]) == REL_BUCKETS - 1
    return pl.pallas_call(
        _bias_p_body,
        grid=(ATT_HEADS,),
        in_specs=[pl.BlockSpec(memory_space=pltpu.SMEM),
                  pl.BlockSpec((2, blk, blk), lambda h: (0, 0, 0))],
        out_specs=pl.BlockSpec((None, 2, blk, blk), lambda h: (h, 0, 0, 0)),
        out_shape=jax.ShapeDtypeStruct((ATT_HEADS, 2, blk, blk), F32),
        compiler_params=_params(("arbitrary",)),
    )(rel_bias, jnp.asarray(buckets))


def _topk_select(s, allowed, idx, n_cand):
    s = jnp.where(allowed, s, -jnp.inf)
    rank = jnp.zeros(s.shape, jnp.int32)
    for m in range(n_cand):
        sm = s[m:m + 1]
        beats = (sm > s) | ((sm == s) & (idx > m))
        rank = rank + beats.astype(jnp.int32)
    return jnp.where(allowed & (rank < MOBA_TOPK), 1.0, 0.0)


def _moba_p_body(q_ref, k_ref, v_ref, tb_ref, o_ref, kbf_ref, vt_ref, kmean_ref, sel_ref, *, nb):
    i = pl.program_id(2)
    blk = MOBA_BLOCK
    dh = ATT_DH

    @pl.when(i == 0)
    def _():
        kf = k_ref[...]
        for hh in range(2):
            kbf_ref[hh] = kf[:, hh * dh:(hh + 1) * dh].astype(BF16)
        vt_ref[...] = v_ref[...].T.astype(BF16)
        kmean_ref[...] = jnp.mean(kf.reshape(nb, blk, LANES), axis=1)

    q = q_ref[...]
    grp = MOBA_KEY_GROUP
    qs_all, prev_sel, carry0 = [], [], []
    for hh in range(2):
        cs = slice(hh * dh, (hh + 1) * dh)
        qh = q[:, cs]
        s = _dot_nt(kmean_ref[:, cs], qh, precision=HIGHEST)
        nidx = lax.broadcasted_iota(jnp.int32, s.shape, 0)
        sel = _topk_select(s, nidx < i, nidx, nb)
        sel_ref[hh] = jnp.where(nidx < i - 1, sel, 0.0)
        prev_sel.append(jnp.sum(jnp.where(nidx == i - 1, sel, 0.0), axis=0, keepdims=True) > 0.5)
        qs_all.append((qh * (dh ** -0.5 * LOG2E)).astype(BF16))
        carry0 += [jnp.full((1, blk), NEG, F32), jnp.zeros((1, blk), F32), jnp.zeros((dh, blk), F32)]

    def softmax_step(carry, lgs, sels, v_ts):
        m_new = []
        for hh in range(2):
            mh = carry[3 * hh]
            for lg, sl in zip(lgs[hh], sels[hh]):
                cm = jnp.max(lg, axis=0, keepdims=True)
                mh = jnp.maximum(mh, cm if sl is None else jnp.where(sl, cm, NEG))
            m_new.append(mh)
        scale, l_new, p_all = [], [], []
        for hh in range(2):
            a = jnp.exp2(carry[3 * hh] - m_new[hh])
            l = a * carry[3 * hh + 1]
            ps = []
            for lg, sl in zip(lgs[hh], sels[hh]):
                p = jnp.exp2(lg - (m_new[hh] if sl is None else jnp.where(sl, m_new[hh], -NEG)))
                l = l + jnp.sum(p, axis=0, keepdims=True)
                ps.append(p.astype(BF16))
            scale.append(a)
            l_new.append(l)
            p_all.append(jnp.concatenate(ps, axis=0))
        out = []
        for hh in range(2):
            out += [m_new[hh], l_new[hh], scale[hh] * carry[3 * hh + 2] + _dot(v_ts[hh], p_all[hh])]
        return tuple(out)

    def body(j, carry):
        st = pl.multiple_of(j * (grp * blk), grp * blk)
        lg_all = [_dot_nt(kbf_ref[hh, pl.ds(st, grp * blk), :], qs_all[hh]) for hh in range(2)]
        lgs = [[lg_all[hh][g * blk:(g + 1) * blk] for g in range(grp)] for hh in range(2)]
        sels = [[sel_ref[hh, pl.ds(j * grp + g, 1), :] > 0.5 for g in range(grp)] for hh in range(2)]
        v_ts = [vt_ref[hh * dh:(hh + 1) * dh, pl.ds(st, grp * blk)] for hh in range(2)]
        return softmax_step(carry, lgs, sels, v_ts)

    fin = lax.fori_loop(0, (i + grp - 2) // grp, body, tuple(carry0))

    own = pl.multiple_of(i * blk, blk)
    prev = pl.multiple_of(jnp.maximum(i - 1, 0) * blk, blk)
    lgs, v_ts = [], []
    for hh in range(2):
        rows = slice(hh * dh, (hh + 1) * dh)
        lgs.append([_dot_nt(kbf_ref[hh, pl.ds(prev, blk), :], qs_all[hh]) + tb_ref[hh, 1],
                    _dot_nt(kbf_ref[hh, pl.ds(own, blk), :], qs_all[hh]) + tb_ref[hh, 0]])
        v_ts.append(jnp.concatenate([vt_ref[rows, pl.ds(prev, blk)], vt_ref[rows, pl.ds(own, blk)]], axis=1))
    fin = softmax_step(fin, lgs, [[prev_sel[hh], None] for hh in range(2)], v_ts)
    outs = [(fin[3 * hh + 2] / fin[3 * hh + 1]).T for hh in range(2)]
    o_ref[...] = jnp.concatenate(outs, axis=1).astype(BF16)


def _moba_prompt(z, bias_tab, n_seq, t):
    blk = MOBA_BLOCK
    nb = t // blk
    assert nb % MOBA_KEY_GROUP == 0
    hp = ATT_HEADS // 2
    kc, vc = OFF_AK // LANES, OFF_AV // LANES
    return pl.pallas_call(
        functools.partial(_moba_p_body, nb=nb),
        grid=(n_seq, hp, nb),
        in_specs=[pl.BlockSpec((blk, LANES), lambda b, h, i: (b * nb + i, h)),
                  pl.BlockSpec((t, LANES), lambda b, h, i: (b, kc + h)),
                  pl.BlockSpec((t, LANES), lambda b, h, i: (b, vc + h)),
                  pl.BlockSpec((2, 2, blk, blk), lambda b, h, i: (h, 0, 0, 0))],
        out_specs=pl.BlockSpec((blk, LANES), lambda b, h, i: (b * nb + i, h)),
        out_shape=jax.ShapeDtypeStruct((n_seq * t, ATT_W), BF16),
        scratch_shapes=[pltpu.VMEM((2, t, ATT_DH), BF16), pltpu.VMEM((LANES, t), BF16),
                        pltpu.VMEM((nb, LANES), F32), pltpu.VMEM((2, nb, blk), F32)],
        compiler_params=_params(("arbitrary", "arbitrary", "arbitrary")),
    )(z, z, z, bias_tab)


PAGES_PER_STEP = 8
PAGES_PER_BLOCK = MOBA_BLOCK // PAGE_SIZE


def _moba_s_body(pt_ref, *refs, n_steps):
    del pt_ref
    kpages, vpages = refs[:PAGES_PER_STEP], refs[PAGES_PER_STEP:2 * PAGES_PER_STEP]
    q_ref, kn_ref, vn_ref, rbt_ref, bk_ref, o_ref, m_s, l_s, s_s, acc_s = refs[2 * PAGES_PER_STEP:]
    g = pl.program_id(1)
    h, dh, w = ATT_HEADS, ATT_DH, ATT_W
    n_pages = n_steps * PAGES_PER_STEP
    row = lax.broadcasted_iota(jnp.int32, (h, w), 0)
    lane = lax.broadcasted_iota(jnp.int32, (h, w), 1)
    own_head = (lane // dh) == row
    qbd = jnp.where(own_head, q_ref[...] * (dh ** -0.5 * LOG2E), 0.0)
    q_hi = qbd.astype(BF16)
    q_lo = (qbd - q_hi.astype(F32)).astype(BF16)
    q2 = jnp.concatenate([q_hi, q_lo], axis=0)
    far_bias = rbt_ref[:, REL_BUCKETS - 1:REL_BUCKETS] * LOG2E

    def page(k_ref, v_ref, bias, idx):
        raw2 = _dot(q2, k_ref[...].reshape(w, PAGE_SIZE).astype(BF16))
        raw = raw2[:h] + raw2[h:]
        lg = raw if bias is None else raw + bias
        mb = jnp.max(lg, axis=1, keepdims=True)
        p = jnp.exp2(lg - mb)
        m_s[idx] = mb + far_bias if bias is None else mb
        l_s[idx] = jnp.sum(p, axis=1, keepdims=True)
        s_s[idx] = jnp.sum(raw, axis=1, keepdims=True)
        acc_s[idx] = _dot_nt(p.astype(BF16), v_ref[...].reshape(w, PAGE_SIZE).astype(BF16))

    last = PAGES_PER_STEP - 1
    for j in range(last):
        page(kpages[j], vpages[j], None, g * PAGES_PER_STEP + j)

    @pl.when(g != n_steps - 1)
    def _():
        page(kpages[last], vpages[last], None, g * PAGES_PER_STEP + last)

    @pl.when(g == n_steps - 1)
    def _():
        bk = bk_ref[...]
        near_bias = jnp.zeros((h, PAGE_SIZE), F32)
        for b in range(REL_BUCKETS):
            near_bias = jnp.where(bk == b, rbt_ref[:, b:b + 1], near_bias)
        page(kpages[last], vpages[last], near_bias * LOG2E, n_pages - 1)
        nblk = n_pages // PAGES_PER_BLOCK
        s = jnp.sum(s_s[...].reshape(nblk, PAGES_PER_BLOCK, h, 1), axis=1)
        nidx = lax.broadcasted_iota(jnp.int32, s.shape, 0)
        sel = _topk_select(s, nidx >= 0, nidx, nblk)
        selp = jnp.broadcast_to(sel[:, None], (nblk, PAGES_PER_BLOCK, h, 1)).reshape(n_pages, h, 1) > 0.5
        ln = jnp.sum(qbd * kn_ref[...], axis=1, keepdims=True) + rbt_ref[:, 0:1] * LOG2E
        mm = jnp.where(selp, m_s[...], NEG)
        mx = jnp.maximum(jnp.max(mm, axis=0), ln)
        wgt = jnp.where(selp, jnp.exp2(mm - mx), 0.0)
        wn = jnp.exp2(ln - mx)
        den = jnp.sum(wgt * l_s[...], axis=0) + wn
        tot = jnp.sum(wgt * acc_s[...], axis=0) + wn * vn_ref[...]
        o_ref[...] = jnp.sum(jnp.where(own_head, tot / den, 0.0), axis=0, keepdims=True)


def _moba_sample(z_s, cache_k, cache_v, layer, page_table, rel_bias_t):
    n_seq, n_pages = page_table.shape
    assert n_pages % PAGES_PER_STEP == 0 and PAGES_PER_STEP % PAGES_PER_BLOCK == 0
    n_steps = n_pages // PAGES_PER_STEP
    assert int(_bucket_np(np.array([PAGE_SIZE + 1]))[0]) == REL_BUCKETS - 1
    near_buckets = _bucket_np(PAGE_SIZE - np.arange(PAGE_SIZE)).reshape(1, PAGE_SIZE)
    rows = lambda off: z_s[:, off:off + ATT_W].reshape(n_seq, 1, ATT_W)

    def page_spec(j):
        return pl.BlockSpec((None, None, ATT_HEADS, ATT_DH, PAGE_SIZE),
                            lambda s, g, pt: (layer, pt[s, g * PAGES_PER_STEP + j], 0, 0, 0))

    row_spec = pl.BlockSpec((None, 1, ATT_W), lambda s, g, pt: (s, 0, 0))
    stat = pltpu.VMEM((n_pages, ATT_HEADS, 1), F32)
    grid_spec = pltpu.PrefetchScalarGridSpec(
        num_scalar_prefetch=1,
        grid=(n_seq, n_steps),
        in_specs=[page_spec(j) for j in range(PAGES_PER_STEP)] * 2 + [
            row_spec, row_spec, row_spec,
            pl.BlockSpec((ATT_HEADS, REL_BUCKETS), lambda s, g, pt: (0, 0)),
            pl.BlockSpec((1, PAGE_SIZE), lambda s, g, pt: (0, 0))],
        out_specs=row_spec,
        scratch_shapes=[stat, stat, stat, pltpu.VMEM((n_pages, ATT_HEADS, ATT_W), F32)],
    )
    out = pl.pallas_call(
        functools.partial(_moba_s_body, n_steps=n_steps),
        grid_spec=grid_spec,
        out_shape=jax.ShapeDtypeStruct((n_seq, 1, ATT_W), F32),
        compiler_params=_params(("arbitrary", "arbitrary")),
    )(page_table, *([cache_k] * PAGES_PER_STEP), *([cache_v] * PAGES_PER_STEP),
      rows(OFF_AQ), rows(OFF_AK), rows(OFF_AV), rel_bias_t, jnp.asarray(near_buckets))
    return out.reshape(n_seq, ATT_W)


def _rope_body(inv_ref, sgn_ref, cos_ref, sin_ref, *, pos0, step, rows):
    r = lax.broadcasted_iota(jnp.int32, (rows, LANES), 0) + pl.program_id(0) * rows
    ang = (pos0 + step * r).astype(F32) * inv_ref[...]
    cos_ref[...] = jnp.cos(ang)
    sin_ref[...] = jnp.sin(ang) * sgn_ref[...]


def _rope_tables(n_rows, pos0, step):
    half = RET_DK // 2
    inv = 1.0 / (ROPE_BASE ** jnp.linspace(0.0, 1.0, half))
    inv_row = jnp.tile(inv, LANES // half).reshape(1, LANES).astype(F32)
    sgn = np.where((np.arange(LANES) % RET_DK) < half, -1.0, 1.0).astype(np.float32).reshape(1, LANES)
    rows = min(n_rows, 512)
    spec = pl.BlockSpec((rows, LANES), lambda i: (i, 0))
    cst = pl.BlockSpec((1, LANES), lambda i: (0, 0))
    return pl.pallas_call(
        functools.partial(_rope_body, pos0=pos0, step=step, rows=rows),
        grid=(n_rows // rows,),
        in_specs=[cst, cst],
        out_specs=[spec, spec],
        out_shape=[jax.ShapeDtypeStruct((n_rows, LANES), F32)] * 2,
        compiler_params=_params(("arbitrary",)),
    )(inv_row, jnp.asarray(sgn))


def _rotary128(x, cos, sin_signed):
    half = RET_DK // 2
    lane = lax.broadcasted_iota(jnp.int32, x.shape, 1)
    partner = jnp.where((lane % RET_DK) < half,
                        pltpu.roll(x, LANES - half, 1), pltpu.roll(x, half, 1))
    return x * cos + partner * sin_signed


def _groupnorm_gate(o, g):
    mu = jnp.mean(o, axis=-1, keepdims=True)
    var = jnp.mean((o - mu) ** 2, axis=-1, keepdims=True)
    return _silu(g) * ((o - mu) * lax.rsqrt(var + GN_EPS))


def _ret_log_decay():
    return jnp.log(1.0 - 2.0 ** (-5.0 - jnp.arange(RET_HEADS, dtype=F32)))


def _ret_p_body(lg_ref, q_ref, k_ref, v_ref, g_ref, cos_ref, sin_ref, o_ref, st_ref, *, chunk):
    hp = pl.program_id(1)
    c = pl.program_id(2)

    @pl.when(c == 0)
    def _():
        st_ref[...] = jnp.zeros(st_ref.shape, F32)

    cos, sin = cos_ref[...], sin_ref[...]
    q = _rotary128(q_ref[...], cos, sin)
    k = _rotary128(k_ref[...], cos, sin) * RET_DK ** -0.5
    ii = lax.broadcasted_iota(jnp.int32, (chunk, chunk), 0)
    jj = lax.broadcasted_iota(jnp.int32, (chunk, chunk), 1)
    diff = (ii - jj).astype(F32)
    ri = lax.broadcasted_iota(jnp.int32, (chunk, 1), 0).astype(F32)
    for hh in range(2):
        lgh = lg_ref[hp * 2 + hh]
        dmask = jnp.where(diff >= 0, jnp.exp(lgh * jnp.maximum(diff, 0.0)), 0.0)
        qh = q[:, hh * RET_DK:(hh + 1) * RET_DK].astype(BF16)
        kh = k[:, hh * RET_DK:(hh + 1) * RET_DK]
        vh = v_ref[:, hh * RET_DV:(hh + 1) * RET_DV].astype(BF16)
        s0 = st_ref[hh]
        a = _dot_nt(qh, kh.astype(BF16)) * dmask
        inner = _dot(a.astype(BF16), vh)
        cross = _dot(qh, s0.astype(BF16)) * jnp.exp(lgh * (ri + 1.0))
        kdec = (kh * jnp.exp(lgh * (chunk - 1.0 - ri))).astype(BF16)
        st_ref[hh] = jnp.exp(lgh * chunk + jnp.zeros((1, 1), F32)) * s0 + _dot_tn(kdec, vh)
        gh = g_ref[:, hh * RET_DV:(hh + 1) * RET_DV]
        o_ref[:, hh * RET_DV:(hh + 1) * RET_DV] = _groupnorm_gate(inner + cross, gh).astype(BF16)


def _ret_prompt(z, cos_tab, sin_tab, log_decay, n_seq, t):
    chunk = RET_CHUNK
    nc = t // chunk
    hp = RET_HEADS // 2
    qc, kc = OFF_RQ // LANES, OFF_RK // LANES
    vc, gc = OFF_RV // (2 * RET_DV), OFF_RG // (2 * RET_DV)
    return pl.pallas_call(
        functools.partial(_ret_p_body, chunk=chunk),
        grid=(n_seq, hp, nc),
        in_specs=[pl.BlockSpec(memory_space=pltpu.SMEM),
                  pl.BlockSpec((chunk, LANES), lambda b, h, c: (b * nc + c, qc + h)),
                  pl.BlockSpec((chunk, LANES), lambda b, h, c: (b * nc + c, kc + h)),
                  pl.BlockSpec((chunk, 2 * RET_DV), lambda b, h, c: (b * nc + c, vc + h)),
                  pl.BlockSpec((chunk, 2 * RET_DV), lambda b, h, c: (b * nc + c, gc + h)),
                  pl.BlockSpec((chunk, LANES), lambda b, h, c: (c, 0)),
                  pl.BlockSpec((chunk, LANES), lambda b, h, c: (c, 0))],
        out_specs=[pl.BlockSpec((chunk, 2 * RET_DV), lambda b, h, c: (b * nc + c, h)),
                   pl.BlockSpec((None, 2, RET_DK, RET_DV), lambda b, h, c: (b, h, 0, 0))],
        out_shape=[jax.ShapeDtypeStruct((n_seq * t, RET_V_W), BF16),
                   jax.ShapeDtypeStruct((n_seq, RET_HEADS, RET_DK, RET_DV), F32)],
        compiler_params=_params(("arbitrary", "arbitrary", "arbitrary")),
    )(log_decay, z, z, z, z, cos_tab, sin_tab)


def _ret_s_prep_body(q_ref, k_ref, cos_ref, sin_ref, qo_ref, ko_ref):
    cos, sin = cos_ref[0:1, :], sin_ref[0:1, :]
    for j in range(RET_QK_W // LANES):
        sl = slice(j * LANES, (j + 1) * LANES)
        qo_ref[:, sl] = _rotary128(q_ref[:, sl], cos, sin)
        ko_ref[:, sl] = _rotary128(k_ref[:, sl], cos, sin) * RET_DK ** -0.5


def _ret_s_body(lg_ref, q_ref, k_ref, v_ref, g_ref, s0_ref, o_ref, sn_ref):
    gam = jnp.exp(lg_ref[pl.program_id(1)] + jnp.zeros((1, 1, LANES), F32))
    q, k, v, s0 = q_ref[...], k_ref[...], v_ref[...], s0_ref[...]
    o = jnp.sum(q * s0, axis=1, keepdims=True) * gam + jnp.sum(q * k, axis=1, keepdims=True) * v
    sn_ref[...] = gam * s0 + k * v
    o_ref[...] = _groupnorm_gate(o, g_ref[...])


def _ret_sample(z_s, state, cos_tab, sin_tab, log_decay):
    n_seq = z_s.shape[0]
    tab = pl.BlockSpec((SUBLANES, LANES), lambda i: (0, 0))
    q_rot, k_rot = pl.pallas_call(
        _ret_s_prep_body,
        grid=(1,),
        in_specs=[pl.BlockSpec((n_seq, RET_QK_W), lambda i: (0, OFF_RQ // RET_QK_W)),
                  pl.BlockSpec((n_seq, RET_QK_W), lambda i: (0, OFF_RK // RET_QK_W)), tab, tab],
        out_specs=[pl.BlockSpec((n_seq, RET_QK_W), lambda i: (0, 0))] * 2,
        out_shape=[jax.ShapeDtypeStruct((n_seq, RET_QK_W), F32)] * 2,
        compiler_params=_params(("arbitrary",)),
    )(z_s, z_s, cos_tab, sin_tab)
    col = lambda a: a.reshape(n_seq, RET_HEADS, RET_DK, 1)
    row = lambda a: a.reshape(n_seq, RET_HEADS, 1, RET_DV)
    bt = 16
    col_spec = pl.BlockSpec((bt, None, RET_DK, 1), lambda i, h: (i, h, 0, 0))
    row_spec = pl.BlockSpec((bt, None, 1, RET_DV), lambda i, h: (i, h, 0, 0))
    st_spec = pl.BlockSpec((bt, None, RET_DK, RET_DV), lambda i, h: (i, h, 0, 0))
    state_all, layer = state
    st_in_spec = pl.BlockSpec((None, bt, None, RET_DK, RET_DV), lambda i, h: (layer, i, h, 0, 0))
    o, s_new = pl.pallas_call(
        _ret_s_body,
        grid=(n_seq // bt, RET_HEADS),
        in_specs=[pl.BlockSpec(memory_space=pltpu.SMEM), col_spec, col_spec, row_spec, row_spec, st_in_spec],
        out_specs=[row_spec, st_spec],
        out_shape=[jax.ShapeDtypeStruct((n_seq, RET_HEADS, 1, RET_DV), F32),
                   jax.ShapeDtypeStruct((n_seq, RET_HEADS, RET_DK, RET_DV), F32)],
        compiler_params=_params(("arbitrary", "arbitrary")),
    )(log_decay, col(q_rot), col(k_rot), row(z_s[:, OFF_RV:OFF_RV + RET_V_W]),
      row(z_s[:, OFF_RG:OFF_RG + RET_V_W]), state_all)
    return o.reshape(n_seq, RET_V_W), s_new


def _s5_prep_body(are_ref, aim_ref, ldt_ref, bre_ref, bim_ref, pwr_ref, pwi_ref, bbr_ref, bbi_ref):
    ar, ai = are_ref[...], aim_ref[...]
    dt = jnp.exp(ldt_ref[...])
    kk = (lax.broadcasted_iota(jnp.int32, pwr_ref.shape, 1) + 1).astype(F32)
    mag = jnp.exp(ar * dt * kk)
    ang = ai * dt * kk
    pwr_ref[...] = mag * jnp.cos(ang)
    pwi_ref[...] = mag * jnp.sin(ang)
    mag1 = jnp.exp(ar * dt)
    nr = mag1 * jnp.cos(ai * dt) - 1.0
    ni = mag1 * jnp.sin(ai * dt)
    den = ar * ar + ai * ai
    cr = (nr * ar + ni * ai) / den
    ci = (ni * ar - nr * ai) / den
    bre, bim = bre_ref[...], bim_ref[...]
    bbr_ref[...] = cr * bre - ci * bim
    bbi_ref[...] = cr * bim + ci * bre


def _s5_prepare(a_re, a_im, log_dt, b_re, b_im, c_re, c_im, d):
    g, p, c = S5_GROUPS, S5_STATE, S5_GROUP
    g3 = lambda a: a.reshape(g, 1, p)
    ldt = jnp.broadcast_to(log_dt.reshape(g, 1, 1), (g, 1, p))
    pwr, pwi, bbr, bbi = pl.pallas_call(
        _s5_prep_body,
        out_shape=[jax.ShapeDtypeStruct((g, SUBLANES, p), F32)] * 2
        + [jax.ShapeDtypeStruct((g, c, p), F32)] * 2,
    )(g3(a_re), g3(a_im), ldt, b_re.transpose(0, 2, 1), b_im.transpose(0, 2, 1))
    eye = jnp.eye(S5_GT, dtype=F32)
    nt = S5_NT

    def in_tile(bb):
        return jnp.einsum("tgcp,gh->tgchp", bb.reshape(nt, S5_GT, c, p), eye).reshape(nt, S5_GT * c, S5_GT * p)

    def out_tile(cc):
        return jnp.einsum("tgcp,gh->tgphc", cc.reshape(nt, S5_GT, c, p), eye).reshape(nt, S5_GT * p, S5_GT * c)

    b_tile = jnp.concatenate([in_tile(bbr), in_tile(bbi)], axis=2).astype(BF16)
    c_tile = jnp.concatenate([out_tile(c_re), out_tile(-c_im)], axis=1).astype(BF16)
    d_tile = d.reshape(nt, 1, S5_GT * c)

    def pw_tile(pw):
        return pw.reshape(nt, S5_GT, SUBLANES, p).transpose(0, 2, 1, 3).reshape(nt, SUBLANES, S5_GT * p)

    pr, pi = pw_tile(pwr), pw_tile(pwi)
    rows = jnp.arange(SUBLANES)[None, :, None]
    slabs = []
    for shift in (1, 2, 4):
        keep = rows >= shift
        slabs += [jnp.where(keep, pr[:, shift - 1:shift, :], 0.0), jnp.where(keep, pi[:, shift - 1:shift, :], 0.0)]
    scan_c = jnp.stack(slabs + [pr, pi], axis=1)
    lam1 = jnp.stack([pr[:, 0:1, :], pi[:, 0:1, :]], axis=1)
    return b_tile, c_tile, d_tile, scan_c, lam1


def _gelu_tanh(y):
    return 0.5 * y * (1.0 + jnp.tanh(math.sqrt(2.0 / math.pi) * (y + 0.044715 * (y * y * y))))


def _s5_p_body(u_ref, bt_ref, ct_ref, d_ref, sc_ref, z_ref, st_ref, x_ref, carry_ref, *, tc):
    ns = S5_TILE_STATES

    @pl.when(pl.program_id(2) == 0)
    def _():
        carry_ref[...] = jnp.zeros(carry_ref.shape, F32)

    u = u_ref[...]
    x_ref[...] = _dot(u.astype(BF16), bt_ref[...])

    def tile(t, carry):
        cr, ci = carry
        st = pl.multiple_of(t * SUBLANES, SUBLANES)
        xr = x_ref[pl.ds(st, SUBLANES), 0:ns]
        xi = x_ref[pl.ds(st, SUBLANES), ns:2 * ns]
        for s, shift in enumerate((1, 2, 4)):
            ar, ai = sc_ref[2 * s], sc_ref[2 * s + 1]
            sr, si = pltpu.roll(xr, shift, 0), pltpu.roll(xi, shift, 0)
            xr, xi = xr + ar * sr - ai * si, xi + ar * si + ai * sr
        pr, pi = sc_ref[6], sc_ref[7]
        xr, xi = xr + pr * cr - pi * ci, xi + pr * ci + pi * cr
        x_ref[pl.ds(st, SUBLANES), 0:ns] = xr
        x_ref[pl.ds(st, SUBLANES), ns:2 * ns] = xi
        return xr[SUBLANES - 1:SUBLANES, :], xi[SUBLANES - 1:SUBLANES, :]

    cr, ci = lax.fori_loop(0, tc // SUBLANES, tile, (carry_ref[:, 0:ns], carry_ref[:, ns:2 * ns]))
    last = jnp.concatenate([cr, ci], axis=1)
    carry_ref[...] = last
    st_ref[...] = last
    y = _dot(x_ref[...].astype(BF16), ct_ref[...]) + d_ref[...] * u
    z_ref[...] = _gelu_tanh(y)


def _s5_prompt(z, tiles, n_seq, t):
    b_tile, c_tile, d_tile, scan_c, _ = tiles
    tc = 512
    nc = t // tc
    uc = OFF_SU // LANES
    ns2 = 2 * S5_TILE_STATES
    zs, st = pl.pallas_call(
        functools.partial(_s5_p_body, tc=tc),
        grid=(n_seq, S5_NT, nc),
        in_specs=[pl.BlockSpec((tc, LANES), lambda b, g, c: (b * nc + c, uc + g)),
                  pl.BlockSpec((None, LANES, ns2), lambda b, g, c: (g, 0, 0)),
                  pl.BlockSpec((None, ns2, LANES), lambda b, g, c: (g, 0, 0)),
                  pl.BlockSpec((None, 1, LANES), lambda b, g, c: (g, 0, 0)),
                  pl.BlockSpec((None, 8, SUBLANES, S5_TILE_STATES), lambda b, g, c: (g, 0, 0, 0))],
        out_specs=[pl.BlockSpec((tc, LANES), lambda b, g, c: (b * nc + c, g)),
                   pl.BlockSpec((None, None, 1, ns2), lambda b, g, c: (b, g, 0, 0))],
        out_shape=[jax.ShapeDtypeStruct((n_seq * t, S5_W), F32),
                   jax.ShapeDtypeStruct((n_seq, S5_NT, 1, ns2), F32)],
        scratch_shapes=[pltpu.VMEM((tc, ns2), F32), pltpu.VMEM((1, ns2), F32)],
        compiler_params=_params(("arbitrary", "arbitrary", "arbitrary")),
    )(z, b_tile, c_tile, d_tile, scan_c)
    s_re = st[:, :, 0, :S5_TILE_STATES].reshape(n_seq, S5_GROUPS, S5_STATE)
    s_im = st[:, :, 0, S5_TILE_STATES:].reshape(n_seq, S5_GROUPS, S5_STATE)
    return zs, s_re, s_im


def _s5_s_body(u_ref, bt_ref, ct_ref, d_ref, l1_ref, x0_ref, z_ref, xn_ref):
    ns = S5_TILE_STATES
    u = u_ref[...]
    bu = _dot(u.astype(BF16), bt_ref[...])
    lr, li = l1_ref[0], l1_ref[1]
    x0r, x0i = x0_ref[:, 0:ns], x0_ref[:, ns:2 * ns]
    x = jnp.concatenate([bu[:, 0:ns] + lr * x0r - li * x0i, bu[:, ns:2 * ns] + lr * x0i + li * x0r], axis=1)
    xn_ref[...] = x
    z_ref[...] = _gelu_tanh(_dot(x.astype(BF16), ct_ref[...]) + d_ref[...] * u)


def _s5_sample(z_s, tiles, x0_re, x0_im):
    b_tile, c_tile, d_tile, _, lam1 = tiles
    n_seq = z_s.shape[0]
    uc = OFF_SU // LANES
    ns2 = 2 * S5_TILE_STATES

    def to_tiles(a):
        return a.reshape(n_seq, S5_NT, S5_TILE_STATES).transpose(1, 0, 2)

    x0 = jnp.concatenate([to_tiles(x0_re), to_tiles(x0_im)], axis=2)
    zs, xn = pl.pallas_call(
        _s5_s_body,
        grid=(S5_NT,),
        in_specs=[pl.BlockSpec((n_seq, LANES), lambda g: (0, uc + g)),
                  pl.BlockSpec((None, LANES, ns2), lambda g: (g, 0, 0)),
                  pl.BlockSpec((None, ns2, LANES), lambda g: (g, 0, 0)),
                  pl.BlockSpec((None, 1, LANES), lambda g: (g, 0, 0)),
                  pl.BlockSpec((None, 2, 1, S5_TILE_STATES), lambda g: (g, 0, 0, 0)),
                  pl.BlockSpec((None, n_seq, ns2), lambda g: (g, 0, 0))],
        out_specs=[pl.BlockSpec((n_seq, LANES), lambda g: (0, g)),
                   pl.BlockSpec((None, n_seq, ns2), lambda g: (g, 0, 0))],
        out_shape=[jax.ShapeDtypeStruct((n_seq, S5_W), F32),
                   jax.ShapeDtypeStruct((S5_NT, n_seq, ns2), F32)],
        compiler_params=_params(("arbitrary",)),
    )(z_s, b_tile, c_tile, d_tile, lam1, x0)

    def from_tiles(a):
        return a.transpose(1, 0, 2).reshape(n_seq, S5_GROUPS, S5_STATE)

    return zs, from_tiles(xn[:, :, :S5_TILE_STATES]), from_tiles(xn[:, :, S5_TILE_STATES:])


def _glu_body(zf_ref, zc_ref, w_ref, o_ref):
    o_ref[...] = (zc_ref[...] * _sigmoid(_dot(zf_ref[...].astype(BF16), w_ref[...]))).astype(BF16)


def _half_glu(zs, w_glu, tm, tn):
    m, w = zs.shape
    w_bf, layer = w_glu
    return pl.pallas_call(
        _glu_body,
        grid=(m // tm, w // tn),
        in_specs=[pl.BlockSpec((tm, w), lambda i, j: (i, 0)),
                  pl.BlockSpec((tm, tn), lambda i, j: (i, j)),
                  pl.BlockSpec((None, w, tn), lambda i, j: (layer, 0, j))],
        out_specs=pl.BlockSpec((tm, tn), lambda i, j: (i, j)),
        out_shape=jax.ShapeDtypeStruct((m, w), BF16),
        compiler_params=_params(("arbitrary", "arbitrary")),
    )(zs, zs, w_bf)


def _layer(x, mods, lw, mix, kind, tm, tiles_per_seq):
    norm1, norm2, w_in, w_branch, w_out, w_ffn_in, w_ffn_out = lw
    sh1, sc1, g1, sh2, sc2, g2 = mods
    z = _norm_mod_matmul(x, norm1, sc1, sh1, w_in, kind, tm, 512, tiles_per_seq)
    (o_att, o_ret, o_s5), state = mix(z)
    merged = _branch_merge(o_att, o_ret, o_s5, z, w_branch, tm, 512)
    x = _matmul_residual(merged, w_out, x, g1, kind, tm, 512, tiles_per_seq)
    act = _norm_mod_ffn_in(x, norm2, sc2, sh2, w_ffn_in, kind, tm, 512, tiles_per_seq)
    tm_out = min(tm, 512)
    x = _matmul_residual(act, w_ffn_out, x, g2, kind, tm_out, 512, tiles_per_seq * (tm // tm_out))
    return x, z, state


def kernel(x_prompt, x_sample, c_prompt, c_sample, cache_k, cache_v, page_table, state_ret, state_s5_re, state_s5_im, rel_bias, norm1_g, norm2_g, w_ada, b_ada, w_in, s5_a_re, s5_a_im, s5_log_dt, s5_b_re, s5_b_im, s5_c_re, s5_c_im, s5_d, w_glu, w_branch, w_out, w_ffn_in, w_ffn_out, final_g):
    n_seq, t, d = x_prompt.shape
    n_dec = x_sample.shape[0]
    depth = w_in.shape[0]
    past = page_table.shape[1] * PAGE_SIZE
    xp = x_prompt.reshape(n_seq * t, d)
    xs = x_sample.reshape(n_dec, d)
    tm_p = 1024
    tiles_per_seq = t // tm_p

    bias_p = _bias_tables_prompt(rel_bias)
    rel_bias_t = rel_bias.T
    cache_kt = cache_k.transpose(0, 1, 3, 4, 2)
    cache_vt = cache_v.transpose(0, 1, 3, 4, 2)
    cos_p, sin_p = _rope_tables(t, 0, 1)
    cos_s, sin_s = _rope_tables(SUBLANES, past, 0)
    log_decay = _ret_log_decay()
    pad = (-(n_seq + n_dec)) % SUBLANES
    c_all = jnp.concatenate([c_prompt, c_sample, jnp.zeros((pad, d), F32)], axis=0)

    w_in_bf, w_branch_bf, w_out_bf = w_in.astype(BF16), w_branch.astype(BF16), w_out.astype(BF16)
    w_ffn_in_bf, w_ffn_out_bf, w_glu_bf = w_ffn_in.astype(BF16), w_ffn_out.astype(BF16), w_glu.astype(BF16)

    new_p, new_s = [], []
    for l in range(depth):
        mod = _ada(c_all, w_ada, b_ada[l], l)
        mod_p = mod[:n_seq].reshape(n_seq, 6, 1, d)
        mod_s = mod[n_seq:n_seq + n_dec].reshape(n_dec, 6, d)
        mods_p = tuple(mod_p[:, i] for i in range(6))
        mods_s = tuple(mod_s[:, i] for i in range(6))
        lw = (norm1_g[l], norm2_g[l], (w_in_bf, l), (w_branch_bf, l), (w_out_bf, l),
              (w_ffn_in_bf, l), (w_ffn_out_bf, l))
        glu_w = (w_glu_bf, l)
        tiles = _s5_prepare(s5_a_re[l], s5_a_im[l], s5_log_dt[l], s5_b_re[l], s5_b_im[l],
                            s5_c_re[l], s5_c_im[l], s5_d[l])

        def mix_p(z):
            o_att = _moba_prompt(z, bias_p, n_seq, t)
            o_ret, s_ret = _ret_prompt(z, cos_p, sin_p, log_decay, n_seq, t)
            zs5, s_re, s_im = _s5_prompt(z, tiles, n_seq, t)
            return (o_att, o_ret, _half_glu(zs5, glu_w, tm_p, 512)), (s_ret, s_re, s_im)

        def mix_s(z, l=l):
            o_att = _moba_sample(z, cache_kt, cache_vt, l, page_table, rel_bias_t)
            o_ret, s_ret = _ret_sample(z, (state_ret, l), cos_s, sin_s, log_decay)
            zs5, s_re, s_im = _s5_sample(z, tiles, state_s5_re[l], state_s5_im[l])
            return (o_att, o_ret, _half_glu(zs5, glu_w, n_dec, 512)), (s_ret, s_re, s_im)

        xp, zp, st_p = _layer(xp, mods_p, lw, mix_p, "prompt", tm_p, tiles_per_seq)
        xs, zs, st_s = _layer(xs, mods_s, lw, mix_s, "sample", n_dec, 1)
        kv = lambda z, n, tt, off: z[:, off:off + ATT_W].reshape(n, tt, ATT_HEADS, ATT_DH)
        new_p.append((kv(zp, n_seq, t, OFF_AK), kv(zp, n_seq, t, OFF_AV)) + st_p)
        new_s.append((kv(zs, n_dec, 1, OFF_AK), kv(zs, n_dec, 1, OFF_AV)) + st_s)

    y_prompt = _final_norm(xp, final_g, tm_p).reshape(n_seq, t, d)
    y_sample = _final_norm(xs, final_g, n_dec).reshape(n_dec, 1, d)
    outs_p = [jnp.stack(a) for a in zip(*new_p)]
    outs_s = [jnp.stack(a) for a in zip(*new_s)]
    return (y_prompt, y_sample, *outs_p, *outs_s)
```

```python
import functools
import math

import numpy as np
import jax
import jax.numpy as jnp
from jax import lax
from jax.experimental import pallas as pl
from jax.experimental.pallas import tpu as pltpu

F32 = jnp.float32
BF16 = jnp.bfloat16
HIGHEST = lax.Precision.HIGHEST

D_MODEL = 2048
PAGE_SIZE = 128
ATT_HEADS = 16
ATT_DH = 64
ATT_W = ATT_HEADS * ATT_DH
MOBA_BLOCK = 256
MOBA_TOPK = 3
REL_BUCKETS = 32
REL_MAX_DIST = 128
RET_HEADS = 8
RET_DK = 64
RET_DV = 128
RET_QK_W = RET_HEADS * RET_DK
RET_V_W = RET_HEADS * RET_DV
RET_CHUNK = 256
ROPE_BASE = 10000.0
S5_W = 1024
S5_GROUP = 16
S5_GROUPS = S5_W // S5_GROUP
S5_STATE = 64
N_BRANCH = 3
BRANCH_W = 1024
D_FF = ((8 * D_MODEL + 3 * 256 - 1) // (3 * 256)) * 256
NORM_EPS = 1e-6
GN_EPS = 1e-5
D_IN = 3 * ATT_W + 2 * RET_QK_W + 2 * RET_V_W + S5_W + N_BRANCH * D_MODEL

OFF_AQ, OFF_AK, OFF_AV = 0, ATT_W, 2 * ATT_W
OFF_RQ = 3 * ATT_W
OFF_RK = OFF_RQ + RET_QK_W
OFF_RV = OFF_RK + RET_QK_W
OFF_RG = OFF_RV + RET_V_W
OFF_SU = OFF_RG + RET_V_W
OFF_GATE = OFF_SU + S5_W

LANES = 128
SUBLANES = 8
VMEM_LIMIT = 56 * 1024 * 1024
NEG = -1e30
LOG2E = math.log2(math.e)
MOBA_KEY_GROUP = 4
S5_GT = 8
S5_TILE_STATES = S5_GT * S5_STATE
S5_NT = S5_GROUPS // S5_GT


def _params(sem):
    return pltpu.CompilerParams(dimension_semantics=sem, vmem_limit_bytes=VMEM_LIMIT)


def _sigmoid(x):
    return 1.0 / (1.0 + jnp.exp(-x))


def _silu(x):
    return x * _sigmoid(x)


def _dot(a, b):
    return jnp.dot(a, b, preferred_element_type=F32)


def _dot_nt(a, b, precision=None):
    return lax.dot_general(a, b, (((1,), (1,)), ((), ())), precision=precision,
                           preferred_element_type=F32)


def _dot_tn(a, b):
    return lax.dot_general(a, b, (((0,), (0,)), ((), ())), preferred_element_type=F32)


def _mod_spec(kind, tm, tn, tiles_per_seq, col_blocked):
    if kind == "prompt":
        if col_blocked:
            return pl.BlockSpec((None, 1, tn), lambda i, j: (i // tiles_per_seq, 0, j))
        return pl.BlockSpec((None, 1, tn), lambda i, j: (i // tiles_per_seq, 0, 0))
    if col_blocked:
        return pl.BlockSpec((tm, tn), lambda i, j: (i, j))
    return pl.BlockSpec((tm, tn), lambda i, j: (i, 0))


def _ada_body(c_ref, w_ref, b_ref, o_ref):
    a = _silu(c_ref[...]).astype(BF16)
    o_ref[...] = _dot(a, w_ref[...].astype(BF16)) + b_ref[...]


def _ada(c_all, w_all, b, layer):
    m, d = c_all.shape
    n = w_all.shape[2]
    tn = 1024
    return pl.pallas_call(
        _ada_body,
        grid=(n // tn,),
        in_specs=[pl.BlockSpec((m, d), lambda j: (0, 0)),
                  pl.BlockSpec((None, d, tn), lambda j: (layer, 0, j)),
                  pl.BlockSpec((1, tn), lambda j: (0, j))],
        out_specs=pl.BlockSpec((m, tn), lambda j: (0, j)),
        out_shape=jax.ShapeDtypeStruct((m, n), F32),
        compiler_params=_params(("arbitrary",)),
    )(c_all, w_all, b.reshape(1, n))


def _norm_mod(x, g, sc, sh):
    y = x * lax.rsqrt(jnp.mean(x * x, axis=-1, keepdims=True) + NORM_EPS)
    return (y * g) * (1.0 + sc) + sh


def _nmm_body(x_ref, g_ref, sc_ref, sh_ref, w_ref, o_ref, h_ref):
    @pl.when(pl.program_id(1) == 0)
    def _():
        h_ref[...] = _norm_mod(x_ref[...], g_ref[...], sc_ref[...], sh_ref[...]).astype(BF16)

    o_ref[...] = _dot(h_ref[...], w_ref[...])


def _norm_mod_matmul(x, g, sc, sh, w, kind, tm, tn, tiles_per_seq):
    m, d = x.shape
    w_bf, layer = w
    n = w_bf.shape[2]
    mod = _mod_spec(kind, tm, d, tiles_per_seq, False)
    return pl.pallas_call(
        _nmm_body,
        grid=(m // tm, n // tn),
        in_specs=[pl.BlockSpec((tm, d), lambda i, j: (i, 0)),
                  pl.BlockSpec((1, d), lambda i, j: (0, 0)),
                  mod, mod,
                  pl.BlockSpec((None, d, tn), lambda i, j: (layer, 0, j))],
        out_specs=pl.BlockSpec((tm, tn), lambda i, j: (i, j)),
        out_shape=jax.ShapeDtypeStruct((m, n), F32),
        scratch_shapes=[pltpu.VMEM((tm, d), BF16)],
        compiler_params=_params(("arbitrary", "arbitrary")),
    )(x, g.reshape(1, d), sc, sh, w_bf)


def _ffn_in_body(x_ref, g_ref, sc_ref, sh_ref, w1_ref, w2_ref, o_ref, h_ref):
    @pl.when(pl.program_id(1) == 0)
    def _():
        h_ref[...] = _norm_mod(x_ref[...], g_ref[...], sc_ref[...], sh_ref[...]).astype(BF16)

    h = h_ref[...]
    o_ref[...] = (_silu(_dot(h, w1_ref[...])) * _dot(h, w2_ref[...])).astype(BF16)


def _norm_mod_ffn_in(x, g, sc, sh, w, kind, tm, tn, tiles_per_seq):
    m, d = x.shape
    w_bf, layer = w
    nj = D_FF // tn
    mod = _mod_spec(kind, tm, d, tiles_per_seq, False)
    return pl.pallas_call(
        _ffn_in_body,
        grid=(m // tm, nj),
        in_specs=[pl.BlockSpec((tm, d), lambda i, j: (i, 0)),
                  pl.BlockSpec((1, d), lambda i, j: (0, 0)),
                  mod, mod,
                  pl.BlockSpec((None, d, tn), lambda i, j: (layer, 0, j)),
                  pl.BlockSpec((None, d, tn), lambda i, j: (layer, 0, j + nj))],
        out_specs=pl.BlockSpec((tm, tn), lambda i, j: (i, j)),
        out_shape=jax.ShapeDtypeStruct((m, D_FF), BF16),
        scratch_shapes=[pltpu.VMEM((tm, d), BF16)],
        compiler_params=_params(("arbitrary", "arbitrary")),
    )(x, g.reshape(1, d), sc, sh, w_bf, w_bf)


def _mmres_body(a_ref, w_ref, x_ref, g_ref, o_ref):
    o_ref[...] = x_ref[...] + g_ref[...] * _dot(a_ref[...], w_ref[...])


def _matmul_residual(a_bf, w, x, gate, kind, tm, tn, tiles_per_seq):
    m, k = a_bf.shape
    w_bf, layer = w
    n = w_bf.shape[2]
    return pl.pallas_call(
        _mmres_body,
        grid=(m // tm, n // tn),
        in_specs=[pl.BlockSpec((tm, k), lambda i, j: (i, 0)),
                  pl.BlockSpec((None, k, tn), lambda i, j: (layer, 0, j)),
                  pl.BlockSpec((tm, tn), lambda i, j: (i, j)),
                  _mod_spec(kind, tm, tn, tiles_per_seq, True)],
        out_specs=pl.BlockSpec((tm, tn), lambda i, j: (i, j)),
        out_shape=jax.ShapeDtypeStruct((m, n), F32),
        compiler_params=_params(("arbitrary", "arbitrary")),
    )(a_bf, w_bf, x, gate)


def _merge_body(oa_ref, or_ref, os_ref, wb_ref, ga_ref, gr_ref, gs_ref, o_ref):
    acc = None
    for n, (o_r, g_r) in enumerate(((oa_ref, ga_ref), (or_ref, gr_ref), (os_ref, gs_ref))):
        term = _sigmoid(g_r[...]) * _dot(o_r[...].astype(BF16), wb_ref[n])
        acc = term if acc is None else acc + term
    o_ref[...] = acc.astype(BF16)


def _branch_merge(o_att, o_ret, o_s5, z, w, tm, tn):
    m = z.shape[0]
    wb_bf, layer = w
    gate0 = OFF_GATE // tn
    per = D_MODEL // tn
    o_spec = pl.BlockSpec((tm, BRANCH_W), lambda i, j: (i, 0))

    def gate_spec(n):
        return pl.BlockSpec((tm, tn), lambda i, j: (i, gate0 + n * per + j))

    return pl.pallas_call(
        _merge_body,
        grid=(m // tm, per),
        in_specs=[o_spec, o_spec, o_spec,
                  pl.BlockSpec((None, N_BRANCH, BRANCH_W, tn), lambda i, j: (layer, 0, 0, j)),
                  gate_spec(0), gate_spec(1), gate_spec(2)],
        out_specs=pl.BlockSpec((tm, tn), lambda i, j: (i, j)),
        out_shape=jax.ShapeDtypeStruct((m, D_MODEL), BF16),
        compiler_params=_params(("arbitrary", "arbitrary")),
    )(o_att, o_ret, o_s5, wb_bf, z, z, z)


def _fnorm_body(x_ref, g_ref, o_ref):
    x = x_ref[...]
    o_ref[...] = x * lax.rsqrt(jnp.mean(x * x, axis=-1, keepdims=True) + NORM_EPS) * g_ref[...]


def _final_norm(x, g, tm):
    m, d = x.shape
    return pl.pallas_call(
        _fnorm_body,
        grid=(m // tm,),
        in_specs=[pl.BlockSpec((tm, d), lambda i: (i, 0)), pl.BlockSpec((1, d), lambda i: (0, 0))],
        out_specs=pl.BlockSpec((tm, d), lambda i: (i, 0)),
        out_shape=jax.ShapeDtypeStruct((m, d), F32),
        compiler_params=_params(("arbitrary",)),
    )(x, g.reshape(1, d))


def _bucket_np(dist):
    n = np.maximum(dist, 0)
    max_exact = REL_BUCKETS // 2
    nf = np.maximum(n, 1).astype(np.float32)
    large = max_exact + (np.log(nf / max_exact) / math.log(REL_MAX_DIST / max_exact)
                         * (REL_BUCKETS - max_exact)).astype(np.int32)
    large = np.minimum(large, REL_BUCKETS - 1)
    return np.where(n < max_exact, n, large).astype(np.int32)


def _bias_p_body(rb_ref, bk_ref, o_ref):
    h = pl.program_id(0)
    blk = MOBA_BLOCK
    far = rb_ref[REL_BUCKETS - 1, h]
    for s in range(2):
        bk = bk_ref[s]
        acc = jnp.zeros((blk, blk), F32)
        for b in range(REL_BUCKETS):
            acc = jnp.where(bk == b, rb_ref[b, h], acc)
        acc = (acc - far) * LOG2E
        if s == 0:
            rk = lax.broadcasted_iota(jnp.int32, (blk, blk), 0)
            rq = lax.broadcasted_iota(jnp.int32, (blk, blk), 1)
            acc = jnp.where(rk <= rq, acc, 2 * NEG)
        o_ref[s] = acc


def _bias_tables_prompt(rel_bias):
    blk = MOBA_BLOCK
    rk = np.arange(blk)[:, None]
    rq = np.arange(blk)[None, :]
    buckets = np.stack([_bucket_np(rq - rk), _bucket_np(blk + rq - rk)])
    assert int(_bucket_np(np.array([blk + 1]))[0]) == REL_BUCKETS - 1
    return pl.pallas_call(
        _bias_p_body,
        grid=(ATT_HEADS,),
        in_specs=[pl.BlockSpec(memory_space=pltpu.SMEM),
                  pl.BlockSpec((2, blk, blk), lambda h: (0, 0, 0))],
        out_specs=pl.BlockSpec((None, 2, blk, blk), lambda h: (h, 0, 0, 0)),
        out_shape=jax.ShapeDtypeStruct((ATT_HEADS, 2, blk, blk), F32),
        compiler_params=_params(("arbitrary",)),
    )(rel_bias, jnp.asarray(buckets))


def _topk_select(s, allowed, idx, n_cand):
    s = jnp.where(allowed, s, -jnp.inf)
    rank = jnp.zeros(s.shape, jnp.int32)
    for m in range(n_cand):
        sm = s[m:m + 1]
        beats = (sm > s) | ((sm == s) & (idx > m))
        rank = rank + beats.astype(jnp.int32)
    return jnp.where(allowed & (rank < MOBA_TOPK), 1.0, 0.0)


def _moba_p_body(q_ref, k_ref, v_ref, tb_ref, o_ref, kbf_ref, vt_ref, kmean_ref, sel_ref, *, nb):
    i = pl.program_id(2)
    blk = MOBA_BLOCK
    dh = ATT_DH

    @pl.when(i == 0)
    def _():
        kf = k_ref[...]
        for hh in range(2):
            kbf_ref[hh] = kf[:, hh * dh:(hh + 1) * dh].astype(BF16)
        vt_ref[...] = v_ref[...].T.astype(BF16)
        kmean_ref[...] = jnp.mean(kf.reshape(nb, blk, LANES), axis=1)

    q = q_ref[...]
    grp = MOBA_KEY_GROUP
    qs_all, prev_sel, carry0 = [], [], []
    for hh in range(2):
        cs = slice(hh * dh, (hh + 1) * dh)
        qh = q[:, cs]
        s = _dot_nt(kmean_ref[:, cs], qh, precision=HIGHEST)
        nidx = lax.broadcasted_iota(jnp.int32, s.shape, 0)
        sel = _topk_select(s, nidx < i, nidx, nb)
        sel_ref[hh] = jnp.where(nidx < i - 1, sel, 0.0)
        prev_sel.append(jnp.sum(jnp.where(nidx == i - 1, sel, 0.0), axis=0, keepdims=True) > 0.5)
        qs_all.append((qh * (dh ** -0.5 * LOG2E)).astype(BF16))
        carry0 += [jnp.full((1, blk), NEG, F32), jnp.zeros((1, blk), F32), jnp.zeros((dh, blk), F32)]

    def softmax_step(carry, lgs, sels, v_ts):
        m_new = []
        for hh in range(2):
            mh = carry[3 * hh]
            for lg, sl in zip(lgs[hh], sels[hh]):
                cm = jnp.max(lg, axis=0, keepdims=True)
                mh = jnp.maximum(mh, cm if sl is None else jnp.where(sl, cm, NEG))
            m_new.append(mh)
        scale, l_new, p_all = [], [], []
        for hh in range(2):
            a = jnp.exp2(carry[3 * hh] - m_new[hh])
            l = a * carry[3 * hh + 1]
            ps = []
            for lg, sl in zip(lgs[hh], sels[hh]):
                p = jnp.exp2(lg - (m_new[hh] if sl is None else jnp.where(sl, m_new[hh], -NEG)))
                l = l + jnp.sum(p, axis=0, keepdims=True)
                ps.append(p.astype(BF16))
            scale.append(a)
            l_new.append(l)
            p_all.append(jnp.concatenate(ps, axis=0))
        out = []
        for hh in range(2):
            out += [m_new[hh], l_new[hh], scale[hh] * carry[3 * hh + 2] + _dot(v_ts[hh], p_all[hh])]
        return tuple(out)

    def body(j, carry):
        st = pl.multiple_of(j * (grp * blk), grp * blk)
        lg_all = [_dot_nt(kbf_ref[hh, pl.ds(st, grp * blk), :], qs_all[hh]) for hh in range(2)]
        lgs = [[lg_all[hh][g * blk:(g + 1) * blk] for g in range(grp)] for hh in range(2)]
        sels = [[sel_ref[hh, pl.ds(j * grp + g, 1), :] > 0.5 for g in range(grp)] for hh in range(2)]
        v_ts = [vt_ref[hh * dh:(hh + 1) * dh, pl.ds(st, grp * blk)] for hh in range(2)]
        return softmax_step(carry, lgs, sels, v_ts)

    fin = lax.fori_loop(0, (i + grp - 2) // grp, body, tuple(carry0))

    own = pl.multiple_of(i * blk, blk)
    prev = pl.multiple_of(jnp.maximum(i - 1, 0) * blk, blk)
    lgs, v_ts = [], []
    for hh in range(2):
        rows = slice(hh * dh, (hh + 1) * dh)
        lgs.append([_dot_nt(kbf_ref[hh, pl.ds(prev, blk), :], qs_all[hh]) + tb_ref[hh, 1],
                    _dot_nt(kbf_ref[hh, pl.ds(own, blk), :], qs_all[hh]) + tb_ref[hh, 0]])
        v_ts.append(jnp.concatenate([vt_ref[rows, pl.ds(prev, blk)], vt_ref[rows, pl.ds(own, blk)]], axis=1))
    fin = softmax_step(fin, lgs, [[prev_sel[hh], None] for hh in range(2)], v_ts)
    outs = [(fin[3 * hh + 2] / fin[3 * hh + 1]).T for hh in range(2)]
    o_ref[...] = jnp.concatenate(outs, axis=1).astype(BF16)


def _moba_prompt(z, bias_tab, n_seq, t):
    blk = MOBA_BLOCK
    nb = t // blk
    assert nb % MOBA_KEY_GROUP == 0
    hp = ATT_HEADS // 2
    kc, vc = OFF_AK // LANES, OFF_AV // LANES
    return pl.pallas_call(
        functools.partial(_moba_p_body, nb=nb),
        grid=(n_seq, hp, nb),
        in_specs=[pl.BlockSpec((blk, LANES), lambda b, h, i: (b * nb + i, h)),
                  pl.BlockSpec((t, LANES), lambda b, h, i: (b, kc + h)),
                  pl.BlockSpec((t, LANES), lambda b, h, i: (b, vc + h)),
                  pl.BlockSpec((2, 2, blk, blk), lambda b, h, i: (h, 0, 0, 0))],
        out_specs=pl.BlockSpec((blk, LANES), lambda b, h, i: (b * nb + i, h)),
        out_shape=jax.ShapeDtypeStruct((n_seq * t, ATT_W), BF16),
        scratch_shapes=[pltpu.VMEM((2, t, ATT_DH), BF16), pltpu.VMEM((LANES, t), BF16),
                        pltpu.VMEM((nb, LANES), F32), pltpu.VMEM((2, nb, blk), F32)],
        compiler_params=_params(("arbitrary", "arbitrary", "arbitrary")),
    )(z, z, z, bias_tab)


PAGES_PER_STEP = 8
PAGES_PER_BLOCK = MOBA_BLOCK // PAGE_SIZE


def _moba_s_body(pt_ref, *refs, n_steps):
    del pt_ref
    kpages, vpages = refs[:PAGES_PER_STEP], refs[PAGES_PER_STEP:2 * PAGES_PER_STEP]
    q_ref, kn_ref, vn_ref, rbt_ref, bk_ref, o_ref, m_s, l_s, s_s, acc_s = refs[2 * PAGES_PER_STEP:]
    g = pl.program_id(1)
    h, dh, w = ATT_HEADS, ATT_DH, ATT_W
    n_pages = n_steps * PAGES_PER_STEP
    row = lax.broadcasted_iota(jnp.int32, (h, w), 0)
    lane = lax.broadcasted_iota(jnp.int32, (h, w), 1)
    own_head = (lane // dh) == row
    qbd = jnp.where(own_head, q_ref[...] * (dh ** -0.5 * LOG2E), 0.0)
    q_hi = qbd.astype(BF16)
    q_lo = (qbd - q_hi.astype(F32)).astype(BF16)
    q2 = jnp.concatenate([q_hi, q_lo], axis=0)
    far_bias = rbt_ref[:, REL_BUCKETS - 1:REL_BUCKETS] * LOG2E
    bk = bk_ref[...]
    near_bias = jnp.zeros((h, PAGE_SIZE), F32)
    for b in range(REL_BUCKETS):
        near_bias = jnp.where(bk == b, rbt_ref[:, b:b + 1], near_bias)
    is_last = (g == n_steps - 1).astype(F32)
    last_bias = far_bias + is_last * (near_bias * LOG2E - far_bias)

    raws = []
    for j in range(PAGES_PER_STEP):
        raw2 = _dot(q2, kpages[j][...].reshape(w, PAGE_SIZE).astype(BF16))
        raws.append(raw2[:h] + raw2[h:])
    stats = []
    for j, raw in enumerate(raws):
        lg = raw + (last_bias if j == PAGES_PER_STEP - 1 else far_bias)
        mb = jnp.max(lg, axis=1, keepdims=True)
        p = jnp.exp2(lg - mb)
        stats.append((mb, jnp.sum(p, axis=1, keepdims=True), jnp.sum(raw, axis=1, keepdims=True), p.astype(BF16)))
    for j, (mb, lb, sb, p) in enumerate(stats):
        idx = g * PAGES_PER_STEP + j
        m_s[idx] = mb
        l_s[idx] = lb
        s_s[idx] = sb
        acc_s[idx] = _dot_nt(p, vpages[j][...].reshape(w, PAGE_SIZE).astype(BF16))

    @pl.when(g == n_steps - 1)
    def _():
        nblk = n_pages // PAGES_PER_BLOCK
        s = jnp.sum(s_s[...].reshape(nblk, PAGES_PER_BLOCK, h, 1), axis=1)
        nidx = lax.broadcasted_iota(jnp.int32, s.shape, 0)
        sel = _topk_select(s, nidx >= 0, nidx, nblk)
        selp = jnp.broadcast_to(sel[:, None], (nblk, PAGES_PER_BLOCK, h, 1)).reshape(n_pages, h, 1) > 0.5
        ln = jnp.sum(qbd * kn_ref[...], axis=1, keepdims=True) + rbt_ref[:, 0:1] * LOG2E
        mm = jnp.where(selp, m_s[...], NEG)
        mx = jnp.maximum(jnp.max(mm, axis=0), ln)
        wgt = jnp.where(selp, jnp.exp2(mm - mx), 0.0)
        wn = jnp.exp2(ln - mx)
        den = jnp.sum(wgt * l_s[...], axis=0) + wn
        tot = jnp.sum(wgt * acc_s[...], axis=0) + wn * vn_ref[...]
        o_ref[...] = jnp.sum(jnp.where(own_head, tot / den, 0.0), axis=0, keepdims=True)


def _moba_sample(z_s, cache_k, cache_v, layer, page_table, rel_bias_t):
    n_seq, n_pages = page_table.shape
    assert n_pages % PAGES_PER_STEP == 0 and PAGES_PER_STEP % PAGES_PER_BLOCK == 0
    n_steps = n_pages // PAGES_PER_STEP
    assert int(_bucket_np(np.array([PAGE_SIZE + 1]))[0]) == REL_BUCKETS - 1
    near_buckets = _bucket_np(PAGE_SIZE - np.arange(PAGE_SIZE)).reshape(1, PAGE_SIZE)
    rows = lambda off: z_s[:, off:off + ATT_W].reshape(n_seq, 1, ATT_W)

    def page_spec(j):
        return pl.BlockSpec((None, None, ATT_HEADS, ATT_DH, PAGE_SIZE),
                            lambda s, g, pt: (layer, pt[s, g * PAGES_PER_STEP + j], 0, 0, 0))

    row_spec = pl.BlockSpec((None, 1, ATT_W), lambda s, g, pt: (s, 0, 0))
    stat = pltpu.VMEM((n_pages, ATT_HEADS, 1), F32)
    grid_spec = pltpu.PrefetchScalarGridSpec(
        num_scalar_prefetch=1,
        grid=(n_seq, n_steps),
        in_specs=[page_spec(j) for j in range(PAGES_PER_STEP)] * 2 + [
            row_spec, row_spec, row_spec,
            pl.BlockSpec((ATT_HEADS, REL_BUCKETS), lambda s, g, pt: (0, 0)),
            pl.BlockSpec((1, PAGE_SIZE), lambda s, g, pt: (0, 0))],
        out_specs=row_spec,
        scratch_shapes=[stat, stat, stat, pltpu.VMEM((n_pages, ATT_HEADS, ATT_W), F32)],
    )
    out = pl.pallas_call(
        functools.partial(_moba_s_body, n_steps=n_steps),
        grid_spec=grid_spec,
        out_shape=jax.ShapeDtypeStruct((n_seq, 1, ATT_W), F32),
        compiler_params=_params(("arbitrary", "arbitrary")),
    )(page_table, *([cache_k] * PAGES_PER_STEP), *([cache_v] * PAGES_PER_STEP),
      rows(OFF_AQ), rows(OFF_AK), rows(OFF_AV), rel_bias_t, jnp.asarray(near_buckets))
    return out.reshape(n_seq, ATT_W)


def _rope_body(inv_ref, sgn_ref, cos_ref, sin_ref, *, pos0, step, rows):
    r = lax.broadcasted_iota(jnp.int32, (rows, LANES), 0) + pl.program_id(0) * rows
    ang = (pos0 + step * r).astype(F32) * inv_ref[...]
    cos_ref[...] = jnp.cos(ang)
    sin_ref[...] = jnp.sin(ang) * sgn_ref[...]


def _rope_tables(n_rows, pos0, step):
    half = RET_DK // 2
    inv = 1.0 / (ROPE_BASE ** jnp.linspace(0.0, 1.0, half))
    inv_row = jnp.tile(inv, LANES // half).reshape(1, LANES).astype(F32)
    sgn = np.where((np.arange(LANES) % RET_DK) < half, -1.0, 1.0).astype(np.float32).reshape(1, LANES)
    rows = min(n_rows, 512)
    spec = pl.BlockSpec((rows, LANES), lambda i: (i, 0))
    cst = pl.BlockSpec((1, LANES), lambda i: (0, 0))
    return pl.pallas_call(
        functools.partial(_rope_body, pos0=pos0, step=step, rows=rows),
        grid=(n_rows // rows,),
        in_specs=[cst, cst],
        out_specs=[spec, spec],
        out_shape=[jax.ShapeDtypeStruct((n_rows, LANES), F32)] * 2,
        compiler_params=_params(("arbitrary",)),
    )(inv_row, jnp.asarray(sgn))


def _rotary128(x, cos, sin_signed):
    half = RET_DK // 2
    lane = lax.broadcasted_iota(jnp.int32, x.shape, 1)
    partner = jnp.where((lane % RET_DK) < half,
                        pltpu.roll(x, LANES - half, 1), pltpu.roll(x, half, 1))
    return x * cos + partner * sin_signed


def _groupnorm_gate(o, g):
    mu = jnp.mean(o, axis=-1, keepdims=True)
    var = jnp.mean((o - mu) ** 2, axis=-1, keepdims=True)
    return _silu(g) * ((o - mu) * lax.rsqrt(var + GN_EPS))


def _ret_log_decay():
    return jnp.log(1.0 - 2.0 ** (-5.0 - jnp.arange(RET_HEADS, dtype=F32)))


def _ret_p_body(lg_ref, q_ref, k_ref, v_ref, g_ref, cos_ref, sin_ref, o_ref, st_ref, *, chunk):
    hp = pl.program_id(1)
    c = pl.program_id(2)

    @pl.when(c == 0)
    def _():
        st_ref[...] = jnp.zeros(st_ref.shape, F32)

    cos, sin = cos_ref[...], sin_ref[...]
    q = _rotary128(q_ref[...], cos, sin)
    k = _rotary128(k_ref[...], cos, sin) * RET_DK ** -0.5
    ii = lax.broadcasted_iota(jnp.int32, (chunk, chunk), 0)
    jj = lax.broadcasted_iota(jnp.int32, (chunk, chunk), 1)
    diff = (ii - jj).astype(F32)
    ri = lax.broadcasted_iota(jnp.int32, (chunk, 1), 0).astype(F32)
    for hh in range(2):
        lgh = lg_ref[hp * 2 + hh]
        dmask = jnp.where(diff >= 0, jnp.exp(lgh * jnp.maximum(diff, 0.0)), 0.0)
        qh = q[:, hh * RET_DK:(hh + 1) * RET_DK].astype(BF16)
        kh = k[:, hh * RET_DK:(hh + 1) * RET_DK]
        vh = v_ref[:, hh * RET_DV:(hh + 1) * RET_DV].astype(BF16)
        s0 = st_ref[hh]
        a = _dot_nt(qh, kh.astype(BF16)) * dmask
        inner = _dot(a.astype(BF16), vh)
        cross = _dot(qh, s0.astype(BF16)) * jnp.exp(lgh * (ri + 1.0))
        kdec = (kh * jnp.exp(lgh * (chunk - 1.0 - ri))).astype(BF16)
        st_ref[hh] = jnp.exp(lgh * chunk + jnp.zeros((1, 1), F32)) * s0 + _dot_tn(kdec, vh)
        gh = g_ref[:, hh * RET_DV:(hh + 1) * RET_DV]
        o_ref[:, hh * RET_DV:(hh + 1) * RET_DV] = _groupnorm_gate(inner + cross, gh).astype(BF16)


def _ret_prompt(z, cos_tab, sin_tab, log_decay, n_seq, t):
    chunk = RET_CHUNK
    nc = t // chunk
    hp = RET_HEADS // 2
    qc, kc = OFF_RQ // LANES, OFF_RK // LANES
    vc, gc = OFF_RV // (2 * RET_DV), OFF_RG // (2 * RET_DV)
    return pl.pallas_call(
        functools.partial(_ret_p_body, chunk=chunk),
        grid=(n_seq, hp, nc),
        in_specs=[pl.BlockSpec(memory_space=pltpu.SMEM),
                  pl.BlockSpec((chunk, LANES), lambda b, h, c: (b * nc + c, qc + h)),
                  pl.BlockSpec((chunk, LANES), lambda b, h, c: (b * nc + c, kc + h)),
                  pl.BlockSpec((chunk, 2 * RET_DV), lambda b, h, c: (b * nc + c, vc + h)),
                  pl.BlockSpec((chunk, 2 * RET_DV), lambda b, h, c: (b * nc + c, gc + h)),
                  pl.BlockSpec((chunk, LANES), lambda b, h, c: (c, 0)),
                  pl.BlockSpec((chunk, LANES), lambda b, h, c: (c, 0))],
        out_specs=[pl.BlockSpec((chunk, 2 * RET_DV), lambda b, h, c: (b * nc + c, h)),
                   pl.BlockSpec((None, 2, RET_DK, RET_DV), lambda b, h, c: (b, h, 0, 0))],
        out_shape=[jax.ShapeDtypeStruct((n_seq * t, RET_V_W), BF16),
                   jax.ShapeDtypeStruct((n_seq, RET_HEADS, RET_DK, RET_DV), F32)],
        compiler_params=_params(("arbitrary", "arbitrary", "arbitrary")),
    )(log_decay, z, z, z, z, cos_tab, sin_tab)


def _ret_s_prep_body(q_ref, k_ref, cos_ref, sin_ref, qo_ref, ko_ref):
    cos, sin = cos_ref[0:1, :], sin_ref[0:1, :]
    for j in range(RET_QK_W // LANES):
        sl = slice(j * LANES, (j + 1) * LANES)
        qo_ref[:, sl] = _rotary128(q_ref[:, sl], cos, sin)
        ko_ref[:, sl] = _rotary128(k_ref[:, sl], cos, sin) * RET_DK ** -0.5


def _ret_s_body(lg_ref, q_ref, k_ref, v_ref, g_ref, s0_ref, o_ref, sn_ref):
    gam = jnp.exp(lg_ref[pl.program_id(1)] + jnp.zeros((1, 1, LANES), F32))
    q, k, v, s0 = q_ref[...], k_ref[...], v_ref[...], s0_ref[...]
    o = jnp.sum(q * s0, axis=1, keepdims=True) * gam + jnp.sum(q * k, axis=1, keepdims=True) * v
    sn_ref[...] = gam * s0 + k * v
    o_ref[...] = _groupnorm_gate(o, g_ref[...])


def _ret_sample(z_s, state, cos_tab, sin_tab, log_decay):
    n_seq = z_s.shape[0]
    tab = pl.BlockSpec((SUBLANES, LANES), lambda i: (0, 0))
    q_rot, k_rot = pl.pallas_call(
        _ret_s_prep_body,
        grid=(1,),
        in_specs=[pl.BlockSpec((n_seq, RET_QK_W), lambda i: (0, OFF_RQ // RET_QK_W)),
                  pl.BlockSpec((n_seq, RET_QK_W), lambda i: (0, OFF_RK // RET_QK_W)), tab, tab],
        out_specs=[pl.BlockSpec((n_seq, RET_QK_W), lambda i: (0, 0))] * 2,
        out_shape=[jax.ShapeDtypeStruct((n_seq, RET_QK_W), F32)] * 2,
        compiler_params=_params(("arbitrary",)),
    )(z_s, z_s, cos_tab, sin_tab)
    col = lambda a: a.reshape(n_seq, RET_HEADS, RET_DK, 1)
    row = lambda a: a.reshape(n_seq, RET_HEADS, 1, RET_DV)
    bt = 16
    col_spec = pl.BlockSpec((bt, None, RET_DK, 1), lambda i, h: (i, h, 0, 0))
    row_spec = pl.BlockSpec((bt, None, 1, RET_DV), lambda i, h: (i, h, 0, 0))
    st_spec = pl.BlockSpec((bt, None, RET_DK, RET_DV), lambda i, h: (i, h, 0, 0))
    state_all, layer = state
    st_in_spec = pl.BlockSpec((None, bt, None, RET_DK, RET_DV), lambda i, h: (layer, i, h, 0, 0))
    o, s_new = pl.pallas_call(
        _ret_s_body,
        grid=(n_seq // bt, RET_HEADS),
        in_specs=[pl.BlockSpec(memory_space=pltpu.SMEM), col_spec, col_spec, row_spec, row_spec, st_in_spec],
        out_specs=[row_spec, st_spec],
        out_shape=[jax.ShapeDtypeStruct((n_seq, RET_HEADS, 1, RET_DV), F32),
                   jax.ShapeDtypeStruct((n_seq, RET_HEADS, RET_DK, RET_DV), F32)],
        compiler_params=_params(("arbitrary", "arbitrary")),
    )(log_decay, col(q_rot), col(k_rot), row(z_s[:, OFF_RV:OFF_RV + RET_V_W]),
      row(z_s[:, OFF_RG:OFF_RG + RET_V_W]), state_all)
    return o.reshape(n_seq, RET_V_W), s_new


def _s5_prep_body(are_ref, aim_ref, ldt_ref, bre_ref, bim_ref, pwr_ref, pwi_ref, bbr_ref, bbi_ref):
    ar, ai = are_ref[...], aim_ref[...]
    dt = jnp.exp(ldt_ref[...])
    kk = (lax.broadcasted_iota(jnp.int32, pwr_ref.shape, 1) + 1).astype(F32)
    mag = jnp.exp(ar * dt * kk)
    ang = ai * dt * kk
    pwr_ref[...] = mag * jnp.cos(ang)
    pwi_ref[...] = mag * jnp.sin(ang)
    mag1 = jnp.exp(ar * dt)
    nr = mag1 * jnp.cos(ai * dt) - 1.0
    ni = mag1 * jnp.sin(ai * dt)
    den = ar * ar + ai * ai
    cr = (nr * ar + ni * ai) / den
    ci = (ni * ar - nr * ai) / den
    bre, bim = bre_ref[...], bim_ref[...]
    bbr_ref[...] = cr * bre - ci * bim
    bbi_ref[...] = cr * bim + ci * bre


def _s5_prepare(a_re, a_im, log_dt, b_re, b_im, c_re, c_im, d):
    g, p, c = S5_GROUPS, S5_STATE, S5_GROUP
    g3 = lambda a: a.reshape(g, 1, p)
    ldt = jnp.broadcast_to(log_dt.reshape(g, 1, 1), (g, 1, p))
    pwr, pwi, bbr, bbi = pl.pallas_call(
        _s5_prep_body,
        out_shape=[jax.ShapeDtypeStruct((g, SUBLANES, p), F32)] * 2
        + [jax.ShapeDtypeStruct((g, c, p), F32)] * 2,
    )(g3(a_re), g3(a_im), ldt, b_re.transpose(0, 2, 1), b_im.transpose(0, 2, 1))
    eye = jnp.eye(S5_GT, dtype=F32)
    nt = S5_NT

    def in_tile(bb):
        return jnp.einsum("tgcp,gh->tgchp", bb.reshape(nt, S5_GT, c, p), eye).reshape(nt, S5_GT * c, S5_GT * p)

    def out_tile(cc):
        return jnp.einsum("tgcp,gh->tgphc", cc.reshape(nt, S5_GT, c, p), eye).reshape(nt, S5_GT * p, S5_GT * c)

    b_tile = jnp.concatenate([in_tile(bbr), in_tile(bbi)], axis=2).astype(BF16)
    c_tile = jnp.concatenate([out_tile(c_re), out_tile(-c_im)], axis=1).astype(BF16)
    d_tile = d.reshape(nt, 1, S5_GT * c)

    def pw_tile(pw):
        return pw.reshape(nt, S5_GT, SUBLANES, p).transpose(0, 2, 1, 3).reshape(nt, SUBLANES, S5_GT * p)

    pr, pi = pw_tile(pwr), pw_tile(pwi)
    rows = jnp.arange(SUBLANES)[None, :, None]
    slabs = []
    for shift in (1, 2, 4):
        keep = rows >= shift
        slabs += [jnp.where(keep, pr[:, shift - 1:shift, :], 0.0), jnp.where(keep, pi[:, shift - 1:shift, :], 0.0)]
    scan_c = jnp.stack(slabs + [pr, pi], axis=1)
    lam1 = jnp.stack([pr[:, 0:1, :], pi[:, 0:1, :]], axis=1)
    return b_tile, c_tile, d_tile, scan_c, lam1


def _gelu_tanh(y):
    return 0.5 * y * (1.0 + jnp.tanh(math.sqrt(2.0 / math.pi) * (y + 0.044715 * (y * y * y))))


def _s5_p_body(u_ref, bt_ref, ct_ref, d_ref, sc_ref, z_ref, st_ref, x_ref, carry_ref, *, tc):
    ns = S5_TILE_STATES

    @pl.when(pl.program_id(2) == 0)
    def _():
        carry_ref[...] = jnp.zeros(carry_ref.shape, F32)

    u = u_ref[...]
    x_ref[...] = _dot(u.astype(BF16), bt_ref[...])

    def tile(t, carry):
        cr, ci = carry
        st = pl.multiple_of(t * SUBLANES, SUBLANES)
        xr = x_ref[pl.ds(st, SUBLANES), 0:ns]
        xi = x_ref[pl.ds(st, SUBLANES), ns:2 * ns]
        for s, shift in enumerate((1, 2, 4)):
            ar, ai = sc_ref[2 * s], sc_ref[2 * s + 1]
            sr, si = pltpu.roll(xr, shift, 0), pltpu.roll(xi, shift, 0)
            xr, xi = xr + ar * sr - ai * si, xi + ar * si + ai * sr
        pr, pi = sc_ref[6], sc_ref[7]
        xr, xi = xr + pr * cr - pi * ci, xi + pr * ci + pi * cr
        x_ref[pl.ds(st, SUBLANES), 0:ns] = xr
        x_ref[pl.ds(st, SUBLANES), ns:2 * ns] = xi
        return xr[SUBLANES - 1:SUBLANES, :], xi[SUBLANES - 1:SUBLANES, :]

    cr, ci = lax.fori_loop(0, tc // SUBLANES, tile, (carry_ref[:, 0:ns], carry_ref[:, ns:2 * ns]))
    last = jnp.concatenate([cr, ci], axis=1)
    carry_ref[...] = last
    st_ref[...] = last
    y = _dot(x_ref[...].astype(BF16), ct_ref[...]) + d_ref[...] * u
    z_ref[...] = _gelu_tanh(y)


def _s5_prompt(z, tiles, n_seq, t):
    b_tile, c_tile, d_tile, scan_c, _ = tiles
    tc = 512
    nc = t // tc
    uc = OFF_SU // LANES
    ns2 = 2 * S5_TILE_STATES
    zs, st = pl.pallas_call(
        functools.partial(_s5_p_body, tc=tc),
        grid=(n_seq, S5_NT, nc),
        in_specs=[pl.BlockSpec((tc, LANES), lambda b, g, c: (b * nc + c, uc + g)),
                  pl.BlockSpec((None, LANES, ns2), lambda b, g, c: (g, 0, 0)),
                  pl.BlockSpec((None, ns2, LANES), lambda b, g, c: (g, 0, 0)),
                  pl.BlockSpec((None, 1, LANES), lambda b, g, c: (g, 0, 0)),
                  pl.BlockSpec((None, 8, SUBLANES, S5_TILE_STATES), lambda b, g, c: (g, 0, 0, 0))],
        out_specs=[pl.BlockSpec((tc, LANES), lambda b, g, c: (b * nc + c, g)),
                   pl.BlockSpec((None, None, 1, ns2), lambda b, g, c: (b, g, 0, 0))],
        out_shape=[jax.ShapeDtypeStruct((n_seq * t, S5_W), F32),
                   jax.ShapeDtypeStruct((n_seq, S5_NT, 1, ns2), F32)],
        scratch_shapes=[pltpu.VMEM((tc, ns2), F32), pltpu.VMEM((1, ns2), F32)],
        compiler_params=_params(("arbitrary", "arbitrary", "arbitrary")),
    )(z, b_tile, c_tile, d_tile, scan_c)
    s_re = st[:, :, 0, :S5_TILE_STATES].reshape(n_seq, S5_GROUPS, S5_STATE)
    s_im = st[:, :, 0, S5_TILE_STATES:].reshape(n_seq, S5_GROUPS, S5_STATE)
    return zs, s_re, s_im


def _s5_s_body(u_ref, bt_ref, ct_ref, d_ref, l1_ref, x0_ref, z_ref, xn_ref):
    ns = S5_TILE_STATES
    u = u_ref[...]
    bu = _dot(u.astype(BF16), bt_ref[...])
    lr, li = l1_ref[0], l1_ref[1]
    x0r, x0i = x0_ref[:, 0:ns], x0_ref[:, ns:2 * ns]
    x = jnp.concatenate([bu[:, 0:ns] + lr * x0r - li * x0i, bu[:, ns:2 * ns] + lr * x0i + li * x0r], axis=1)
    xn_ref[...] = x
    z_ref[...] = _gelu_tanh(_dot(x.astype(BF16), ct_ref[...]) + d_ref[...] * u)


def _s5_sample(z_s, tiles, x0_re, x0_im):
    b_tile, c_tile, d_tile, _, lam1 = tiles
    n_seq = z_s.shape[0]
    uc = OFF_SU // LANES
    ns2 = 2 * S5_TILE_STATES

    def to_tiles(a):
        return a.reshape(n_seq, S5_NT, S5_TILE_STATES).transpose(1, 0, 2)

    x0 = jnp.concatenate([to_tiles(x0_re), to_tiles(x0_im)], axis=2)
    zs, xn = pl.pallas_call(
        _s5_s_body,
        grid=(S5_NT,),
        in_specs=[pl.BlockSpec((n_seq, LANES), lambda g: (0, uc + g)),
                  pl.BlockSpec((None, LANES, ns2), lambda g: (g, 0, 0)),
                  pl.BlockSpec((None, ns2, LANES), lambda g: (g, 0, 0)),
                  pl.BlockSpec((None, 1, LANES), lambda g: (g, 0, 0)),
                  pl.BlockSpec((None, 2, 1, S5_TILE_STATES), lambda g: (g, 0, 0, 0)),
                  pl.BlockSpec((None, n_seq, ns2), lambda g: (g, 0, 0))],
        out_specs=[pl.BlockSpec((n_seq, LANES), lambda g: (0, g)),
                   pl.BlockSpec((None, n_seq, ns2), lambda g: (g, 0, 0))],
        out_shape=[jax.ShapeDtypeStruct((n_seq, S5_W), F32),
                   jax.ShapeDtypeStruct((S5_NT, n_seq, ns2), F32)],
        compiler_params=_params(("arbitrary",)),
    )(z_s, b_tile, c_tile, d_tile, lam1, x0)

    def from_tiles(a):
        return a.transpose(1, 0, 2).reshape(n_seq, S5_GROUPS, S5_STATE)

    return zs, from_tiles(xn[:, :, :S5_TILE_STATES]), from_tiles(xn[:, :, S5_TILE_STATES:])


def _glu_body(zf_ref, zc_ref, w_ref, o_ref):
    o_ref[...] = (zc_ref[...] * _sigmoid(_dot(zf_ref[...].astype(BF16), w_ref[...]))).astype(BF16)


def _half_glu(zs, w_glu, tm, tn):
    m, w = zs.shape
    w_bf, layer = w_glu
    return pl.pallas_call(
        _glu_body,
        grid=(m // tm, w // tn),
        in_specs=[pl.BlockSpec((tm, w), lambda i, j: (i, 0)),
                  pl.BlockSpec((tm, tn), lambda i, j: (i, j)),
                  pl.BlockSpec((None, w, tn), lambda i, j: (layer, 0, j))],
        out_specs=pl.BlockSpec((tm, tn), lambda i, j: (i, j)),
        out_shape=jax.ShapeDtypeStruct((m, w), BF16),
        compiler_params=_params(("arbitrary", "arbitrary")),
    )(zs, zs, w_bf)


def _layer(x, mods, lw, mix, kind, tm, tiles_per_seq):
    norm1, norm2, w_in, w_branch, w_out, w_ffn_in, w_ffn_out = lw
    sh1, sc1, g1, sh2, sc2, g2 = mods
    z = _norm_mod_matmul(x, norm1, sc1, sh1, w_in, kind, tm, 1024, tiles_per_seq)
    (o_att, o_ret, o_s5), state = mix(z)
    merged = _branch_merge(o_att, o_ret, o_s5, z, w_branch, tm, 512)
    x = _matmul_residual(merged, w_out, x, g1, kind, tm, 1024, tiles_per_seq)
    act = _norm_mod_ffn_in(x, norm2, sc2, sh2, w_ffn_in, kind, tm, 512, tiles_per_seq)
    tm_out = min(tm, 512)
    x = _matmul_residual(act, w_ffn_out, x, g2, kind, tm_out, 1024, tiles_per_seq * (tm // tm_out))
    return x, z, state


def kernel(x_prompt, x_sample, c_prompt, c_sample, cache_k, cache_v, page_table, state_ret, state_s5_re, state_s5_im, rel_bias, norm1_g, norm2_g, w_ada, b_ada, w_in, s5_a_re, s5_a_im, s5_log_dt, s5_b_re, s5_b_im, s5_c_re, s5_c_im, s5_d, w_glu, w_branch, w_out, w_ffn_in, w_ffn_out, final_g):
    n_seq, t, d = x_prompt.shape
    n_dec = x_sample.shape[0]
    depth = w_in.shape[0]
    past = page_table.shape[1] * PAGE_SIZE
    xp = x_prompt.reshape(n_seq * t, d)
    xs = x_sample.reshape(n_dec, d)
    tm_p = 1024
    tiles_per_seq = t // tm_p

    bias_p = _bias_tables_prompt(rel_bias)
    rel_bias_t = rel_bias.T
    cache_kt = cache_k.transpose(0, 1, 3, 4, 2)
    cache_vt = cache_v.transpose(0, 1, 3, 4, 2)
    cos_p, sin_p = _rope_tables(t, 0, 1)
    cos_s, sin_s = _rope_tables(SUBLANES, past, 0)
    log_decay = _ret_log_decay()
    pad = (-(n_seq + n_dec)) % SUBLANES
    c_all = jnp.concatenate([c_prompt, c_sample, jnp.zeros((pad, d), F32)], axis=0)

    w_in_bf, w_branch_bf, w_out_bf = w_in.astype(BF16), w_branch.astype(BF16), w_out.astype(BF16)
    w_ffn_in_bf, w_ffn_out_bf, w_glu_bf = w_ffn_in.astype(BF16), w_ffn_out.astype(BF16), w_glu.astype(BF16)

    new_p, new_s = [], []
    for l in range(depth):
        mod = _ada(c_all, w_ada, b_ada[l], l)
        mod_p = mod[:n_seq].reshape(n_seq, 6, 1, d)
        mod_s = mod[n_seq:n_seq + n_dec].reshape(n_dec, 6, d)
        mods_p = tuple(mod_p[:, i] for i in range(6))
        mods_s = tuple(mod_s[:, i] for i in range(6))
        lw = (norm1_g[l], norm2_g[l], (w_in_bf, l), (w_branch_bf, l), (w_out_bf, l),
              (w_ffn_in_bf, l), (w_ffn_out_bf, l))
        glu_w = (w_glu_bf, l)
        tiles = _s5_prepare(s5_a_re[l], s5_a_im[l], s5_log_dt[l], s5_b_re[l], s5_b_im[l],
                            s5_c_re[l], s5_c_im[l], s5_d[l])

        def mix_p(z):
            o_att = _moba_prompt(z, bias_p, n_seq, t)
            o_ret, s_ret = _ret_prompt(z, cos_p, sin_p, log_decay, n_seq, t)
            zs5, s_re, s_im = _s5_prompt(z, tiles, n_seq, t)
            return (o_att, o_ret, _half_glu(zs5, glu_w, tm_p, 512)), (s_ret, s_re, s_im)

        def mix_s(z, l=l):
            o_att = _moba_sample(z, cache_kt, cache_vt, l, page_table, rel_bias_t)
            o_ret, s_ret = _ret_sample(z, (state_ret, l), cos_s, sin_s, log_decay)
            zs5, s_re, s_im = _s5_sample(z, tiles, state_s5_re[l], state_s5_im[l])
            return (o_att, o_ret, _half_glu(zs5, glu_w, n_dec, 512)), (s_ret, s_re, s_im)

        xp, zp, st_p = _layer(xp, mods_p, lw, mix_p, "prompt", tm_p, tiles_per_seq)
        xs, zs, st_s = _layer(xs, mods_s, lw, mix_s, "sample", n_dec, 1)
        kv = lambda z, n, tt, off: z[:, off:off + ATT_W].reshape(n, tt, ATT_HEADS, ATT_DH)
        new_p.append((kv(zp, n_seq, t, OFF_AK), kv(zp, n_seq, t, OFF_AV)) + st_p)
        new_s.append((kv(zs, n_dec, 1, OFF_AK), kv(zs, n_dec, 1, OFF_AV)) + st_s)

    y_prompt = _final_norm(xp, final_g, tm_p).reshape(n_seq, t, d)
    y_sample = _final_norm(xs, final_g, n_dec).reshape(n_dec, 1, d)
    outs_p = [jnp.stack(a) for a in zip(*new_p)]
    outs_s = [jnp.stack(a) for a in zip(*new_s)]
    return (y_prompt, y_sample, *outs_p, *outs_s)
```

```python
import functools
import math

import numpy as np
import jax
import jax.numpy as jnp
from jax import lax
from jax.experimental import pallas as pl
from jax.experimental.pallas import tpu as pltpu

F32 = jnp.float32
BF16 = jnp.bfloat16
HIGHEST = lax.Precision.HIGHEST

D_MODEL = 2048
PAGE_SIZE = 128
ATT_HEADS = 16
ATT_DH = 64
ATT_W = ATT_HEADS * ATT_DH
MOBA_BLOCK = 256
MOBA_TOPK = 3
REL_BUCKETS = 32
REL_MAX_DIST = 128
RET_HEADS = 8
RET_DK = 64
RET_DV = 128
RET_QK_W = RET_HEADS * RET_DK
RET_V_W = RET_HEADS * RET_DV
RET_CHUNK = 256
ROPE_BASE = 10000.0
S5_W = 1024
S5_GROUP = 16
S5_GROUPS = S5_W // S5_GROUP
S5_STATE = 64
N_BRANCH = 3
BRANCH_W = 1024
D_FF = ((8 * D_MODEL + 3 * 256 - 1) // (3 * 256)) * 256
NORM_EPS = 1e-6
GN_EPS = 1e-5
D_IN = 3 * ATT_W + 2 * RET_QK_W + 2 * RET_V_W + S5_W + N_BRANCH * D_MODEL

OFF_AQ, OFF_AK, OFF_AV = 0, ATT_W, 2 * ATT_W
OFF_RQ = 3 * ATT_W
OFF_RK = OFF_RQ + RET_QK_W
OFF_RV = OFF_RK + RET_QK_W
OFF_RG = OFF_RV + RET_V_W
OFF_SU = OFF_RG + RET_V_W
OFF_GATE = OFF_SU + S5_W

LANES = 128
SUBLANES = 8
VMEM_LIMIT = 56 * 1024 * 1024
NEG = -1e30
LOG2E = math.log2(math.e)
MOBA_KEY_GROUP = 4
S5_GT = 8
S5_TILE_STATES = S5_GT * S5_STATE
S5_NT = S5_GROUPS // S5_GT


def _params(sem):
    return pltpu.CompilerParams(dimension_semantics=sem, vmem_limit_bytes=VMEM_LIMIT)


def _sigmoid(x):
    return 1.0 / (1.0 + jnp.exp(-x))


def _silu(x):
    return x * _sigmoid(x)


def _dot(a, b):
    return jnp.dot(a, b, preferred_element_type=F32)


def _dot_nt(a, b, precision=None):
    return lax.dot_general(a, b, (((1,), (1,)), ((), ())), precision=precision,
                           preferred_element_type=F32)


def _dot_tn(a, b):
    return lax.dot_general(a, b, (((0,), (0,)), ((), ())), preferred_element_type=F32)


def _mod_spec(kind, tm, tn, tiles_per_seq, col_blocked):
    if kind == "prompt":
        if col_blocked:
            return pl.BlockSpec((None, 1, tn), lambda i, j: (i // tiles_per_seq, 0, j))
        return pl.BlockSpec((None, 1, tn), lambda i, j: (i // tiles_per_seq, 0, 0))
    if col_blocked:
        return pl.BlockSpec((tm, tn), lambda i, j: (i, j))
    return pl.BlockSpec((tm, tn), lambda i, j: (i, 0))


def _ada_body(c_ref, w_ref, b_ref, o_ref):
    a = _silu(c_ref[...]).astype(BF16)
    o_ref[...] = _dot(a, w_ref[...].astype(BF16)) + b_ref[...]


def _ada(c_all, w_all, b, layer):
    m, d = c_all.shape
    n = w_all.shape[2]
    tn = 1024
    return pl.pallas_call(
        _ada_body,
        grid=(n // tn,),
        in_specs=[pl.BlockSpec((m, d), lambda j: (0, 0)),
                  pl.BlockSpec((None, d, tn), lambda j: (layer, 0, j)),
                  pl.BlockSpec((1, tn), lambda j: (0, j))],
        out_specs=pl.BlockSpec((m, tn), lambda j: (0, j)),
        out_shape=jax.ShapeDtypeStruct((m, n), F32),
        compiler_params=_params(("arbitrary",)),
    )(c_all, w_all, b.reshape(1, n))


def _norm_mod(x, g, sc, sh):
    y = x * lax.rsqrt(jnp.mean(x * x, axis=-1, keepdims=True) + NORM_EPS)
    return (y * g) * (1.0 + sc) + sh


def _nmm_body(x_ref, g_ref, sc_ref, sh_ref, w_ref, o_ref, h_ref):
    @pl.when(pl.program_id(1) == 0)
    def _():
        h_ref[...] = _norm_mod(x_ref[...], g_ref[...], sc_ref[...], sh_ref[...]).astype(BF16)

    o_ref[...] = _dot(h_ref[...], w_ref[...])


def _norm_mod_matmul(x, g, sc, sh, w, kind, tm, tn, tiles_per_seq):
    m, d = x.shape
    w_bf, layer = w
    n = w_bf.shape[2]
    mod = _mod_spec(kind, tm, d, tiles_per_seq, False)
    return pl.pallas_call(
        _nmm_body,
        grid=(m // tm, n // tn),
        in_specs=[pl.BlockSpec((tm, d), lambda i, j: (i, 0)),
                  pl.BlockSpec((1, d), lambda i, j: (0, 0)),
                  mod, mod,
                  pl.BlockSpec((None, d, tn), lambda i, j: (layer, 0, j))],
        out_specs=pl.BlockSpec((tm, tn), lambda i, j: (i, j)),
        out_shape=jax.ShapeDtypeStruct((m, n), F32),
        scratch_shapes=[pltpu.VMEM((tm, d), BF16)],
        compiler_params=_params(("arbitrary", "arbitrary")),
    )(x, g.reshape(1, d), sc, sh, w_bf)


def _ffn_in_body(x_ref, g_ref, sc_ref, sh_ref, w1_ref, w2_ref, o_ref, h_ref):
    @pl.when(pl.program_id(1) == 0)
    def _():
        h_ref[...] = _norm_mod(x_ref[...], g_ref[...], sc_ref[...], sh_ref[...]).astype(BF16)

    h = h_ref[...]
    o_ref[...] = (_silu(_dot(h, w1_ref[...])) * _dot(h, w2_ref[...])).astype(BF16)


def _norm_mod_ffn_in(x, g, sc, sh, w, kind, tm, tn, tiles_per_seq):
    m, d = x.shape
    w_bf, layer = w
    nj = D_FF // tn
    mod = _mod_spec(kind, tm, d, tiles_per_seq, False)
    return pl.pallas_call(
        _ffn_in_body,
        grid=(m // tm, nj),
        in_specs=[pl.BlockSpec((tm, d), lambda i, j: (i, 0)),
                  pl.BlockSpec((1, d), lambda i, j: (0, 0)),
                  mod, mod,
                  pl.BlockSpec((None, d, tn), lambda i, j: (layer, 0, j)),
                  pl.BlockSpec((None, d, tn), lambda i, j: (layer, 0, j + nj))],
        out_specs=pl.BlockSpec((tm, tn), lambda i, j: (i, j)),
        out_shape=jax.ShapeDtypeStruct((m, D_FF), BF16),
        scratch_shapes=[pltpu.VMEM((tm, d), BF16)],
        compiler_params=_params(("arbitrary", "arbitrary")),
    )(x, g.reshape(1, d), sc, sh, w_bf, w_bf)


def _mmres_body(a_ref, w_ref, x_ref, g_ref, o_ref):
    o_ref[...] = x_ref[...] + g_ref[...] * _dot(a_ref[...], w_ref[...])


def _matmul_residual(a_bf, w, x, gate, kind, tm, tn, tiles_per_seq):
    m, k = a_bf.shape
    w_bf, layer = w
    n = w_bf.shape[2]
    return pl.pallas_call(
        _mmres_body,
        grid=(m // tm, n // tn),
        in_specs=[pl.BlockSpec((tm, k), lambda i, j: (i, 0)),
                  pl.BlockSpec((None, k, tn), lambda i, j: (layer, 0, j)),
                  pl.BlockSpec((tm, tn), lambda i, j: (i, j)),
                  _mod_spec(kind, tm, tn, tiles_per_seq, True)],
        out_specs=pl.BlockSpec((tm, tn), lambda i, j: (i, j)),
        out_shape=jax.ShapeDtypeStruct((m, n), F32),
        compiler_params=_params(("arbitrary", "arbitrary")),
    )(a_bf, w_bf, x, gate)


def _merge_body(oa_ref, or_ref, os_ref, wb_ref, ga_ref, gr_ref, gs_ref, o_ref):
    acc = None
    for n, (o_r, g_r) in enumerate(((oa_ref, ga_ref), (or_ref, gr_ref), (os_ref, gs_ref))):
        term = _sigmoid(g_r[...]) * _dot(o_r[...].astype(BF16), wb_ref[n])
        acc = term if acc is None else acc + term
    o_ref[...] = acc.astype(BF16)


def _branch_merge(o_att, o_ret, o_s5, z, w, tm, tn):
    m = z.shape[0]
    wb_bf, layer = w
    gate0 = OFF_GATE // tn
    per = D_MODEL // tn
    o_spec = pl.BlockSpec((tm, BRANCH_W), lambda i, j: (i, 0))

    def gate_spec(n):
        return pl.BlockSpec((tm, tn), lambda i, j: (i, gate0 + n * per + j))

    return pl.pallas_call(
        _merge_body,
        grid=(m // tm, per),
        in_specs=[o_spec, o_spec, o_spec,
                  pl.BlockSpec((None, N_BRANCH, BRANCH_W, tn), lambda i, j: (layer, 0, 0, j)),
                  gate_spec(0), gate_spec(1), gate_spec(2)],
        out_specs=pl.BlockSpec((tm, tn), lambda i, j: (i, j)),
        out_shape=jax.ShapeDtypeStruct((m, D_MODEL), BF16),
        compiler_params=_params(("arbitrary", "arbitrary")),
    )(o_att, o_ret, o_s5, wb_bf, z, z, z)


def _fnorm_body(x_ref, g_ref, o_ref):
    x = x_ref[...]
    o_ref[...] = x * lax.rsqrt(jnp.mean(x * x, axis=-1, keepdims=True) + NORM_EPS) * g_ref[...]


def _final_norm(x, g, tm):
    m, d = x.shape
    return pl.pallas_call(
        _fnorm_body,
        grid=(m // tm,),
        in_specs=[pl.BlockSpec((tm, d), lambda i: (i, 0)), pl.BlockSpec((1, d), lambda i: (0, 0))],
        out_specs=pl.BlockSpec((tm, d), lambda i: (i, 0)),
        out_shape=jax.ShapeDtypeStruct((m, d), F32),
        compiler_params=_params(("arbitrary",)),
    )(x, g.reshape(1, d))


def _bucket_np(dist):
    n = np.maximum(dist, 0)
    max_exact = REL_BUCKETS // 2
    nf = np.maximum(n, 1).astype(np.float32)
    large = max_exact + (np.log(nf / max_exact) / math.log(REL_MAX_DIST / max_exact)
                         * (REL_BUCKETS - max_exact)).astype(np.int32)
    large = np.minimum(large, REL_BUCKETS - 1)
    return np.where(n < max_exact, n, large).astype(np.int32)


def _bias_p_body(rb_ref, bk_ref, o_ref):
    h = pl.program_id(0)
    blk = MOBA_BLOCK
    far = rb_ref[REL_BUCKETS - 1, h]
    for s in range(2):
        bk = bk_ref[s]
        acc = jnp.zeros((blk, blk), F32)
        for b in range(REL_BUCKETS):
            acc = jnp.where(bk == b, rb_ref[b, h], acc)
        acc = (acc - far) * LOG2E
        if s == 0:
            rk = lax.broadcasted_iota(jnp.int32, (blk, blk), 0)
            rq = lax.broadcasted_iota(jnp.int32, (blk, blk), 1)
            acc = jnp.where(rk <= rq, acc, 2 * NEG)
        o_ref[s] = acc


def _bias_tables_prompt(rel_bias):
    blk = MOBA_BLOCK
    rk = np.arange(blk)[:, None]
    rq = np.arange(blk)[None, :]
    buckets = np.stack([_bucket_np(rq - rk), _bucket_np(blk + rq - rk)])
    assert int(_bucket_np(np.array([blk + 1]))[0]) == REL_BUCKETS - 1
    return pl.pallas_call(
        _bias_p_body,
        grid=(ATT_HEADS,),
        in_specs=[pl.BlockSpec(memory_space=pltpu.SMEM),
                  pl.BlockSpec((2, blk, blk), lambda h: (0, 0, 0))],
        out_specs=pl.BlockSpec((None, 2, blk, blk), lambda h: (h, 0, 0, 0)),
        out_shape=jax.ShapeDtypeStruct((ATT_HEADS, 2, blk, blk), F32),
        compiler_params=_params(("arbitrary",)),
    )(rel_bias, jnp.asarray(buckets))


def _topk_select(s, allowed, idx, n_cand):
    s = jnp.where(allowed, s, -jnp.inf)
    rank = jnp.zeros(s.shape, jnp.int32)
    for m in range(n_cand):
        sm = s[m:m + 1]
        beats = (sm > s) | ((sm == s) & (idx > m))
        rank = rank + beats.astype(jnp.int32)
    return jnp.where(allowed & (rank < MOBA_TOPK), 1.0, 0.0)


def _moba_p_body(q_ref, k_ref, v_ref, tb_ref, o_ref, kbf_ref, vt_ref, kmean_ref, sel_ref, *, nb):
    i = pl.program_id(2)
    blk = MOBA_BLOCK
    dh = ATT_DH

    @pl.when(i == 0)
    def _():
        kf = k_ref[...]
        for hh in range(2):
            kbf_ref[hh] = kf[:, hh * dh:(hh + 1) * dh].astype(BF16)
        vt_ref[...] = v_ref[...].T.astype(BF16)
        kmean_ref[...] = jnp.mean(kf.reshape(nb, blk, LANES), axis=1)

    q = q_ref[...]
    grp = MOBA_KEY_GROUP
    qs_all, prev_sel, carry0 = [], [], []
    for hh in range(2):
        cs = slice(hh * dh, (hh + 1) * dh)
        qh = q[:, cs]
        s = _dot_nt(kmean_ref[:, cs], qh, precision=HIGHEST)
        nidx = lax.broadcasted_iota(jnp.int32, s.shape, 0)
        sel = _topk_select(s, nidx < i, nidx, nb)
        sel_ref[hh] = jnp.where(nidx < i - 1, sel, 0.0)
        prev_sel.append(jnp.sum(jnp.where(nidx == i - 1, sel, 0.0), axis=0, keepdims=True) > 0.5)
        qs_all.append((qh * (dh ** -0.5 * LOG2E)).astype(BF16))
        carry0 += [jnp.full((1, blk), NEG, F32), jnp.zeros((1, blk), F32), jnp.zeros((dh, blk), F32)]

    def softmax_step(carry, lgs, sels, v_ts):
        m_new = []
        for hh in range(2):
            mh = carry[3 * hh]
            for lg, sl in zip(lgs[hh], sels[hh]):
                cm = jnp.max(lg, axis=0, keepdims=True)
                mh = jnp.maximum(mh, cm if sl is None else jnp.where(sl, cm, NEG))
            m_new.append(mh)
        scale, l_new, p_all = [], [], []
        for hh in range(2):
            a = jnp.exp2(carry[3 * hh] - m_new[hh])
            l = a * carry[3 * hh + 1]
            ps = []
            for lg, sl in zip(lgs[hh], sels[hh]):
                p = jnp.exp2(lg - (m_new[hh] if sl is None else jnp.where(sl, m_new[hh], -NEG)))
                l = l + jnp.sum(p, axis=0, keepdims=True)
                ps.append(p.astype(BF16))
            scale.append(a)
            l_new.append(l)
            p_all.append(jnp.concatenate(ps, axis=0))
        out = []
        for hh in range(2):
            out += [m_new[hh], l_new[hh], scale[hh] * carry[3 * hh + 2] + _dot(v_ts[hh], p_all[hh])]
        return tuple(out)

    def body(j, carry):
        st = pl.multiple_of(j * (grp * blk), grp * blk)
        lg_all = [_dot_nt(kbf_ref[hh, pl.ds(st, grp * blk), :], qs_all[hh]) for hh in range(2)]
        lgs = [[lg_all[hh][g * blk:(g + 1) * blk] for g in range(grp)] for hh in range(2)]
        sels = [[sel_ref[hh, pl.ds(j * grp + g, 1), :] > 0.5 for g in range(grp)] for hh in range(2)]
        v_ts = [vt_ref[hh * dh:(hh + 1) * dh, pl.ds(st, grp * blk)] for hh in range(2)]
        return softmax_step(carry, lgs, sels, v_ts)

    fin = lax.fori_loop(0, (i + grp - 2) // grp, body, tuple(carry0))

    own = pl.multiple_of(i * blk, blk)
    prev = pl.multiple_of(jnp.maximum(i - 1, 0) * blk, blk)
    lgs, v_ts = [], []
    for hh in range(2):
        rows = slice(hh * dh, (hh + 1) * dh)
        lgs.append([_dot_nt(kbf_ref[hh, pl.ds(prev, blk), :], qs_all[hh]) + tb_ref[hh, 1],
                    _dot_nt(kbf_ref[hh, pl.ds(own, blk), :], qs_all[hh]) + tb_ref[hh, 0]])
        v_ts.append(jnp.concatenate([vt_ref[rows, pl.ds(prev, blk)], vt_ref[rows, pl.ds(own, blk)]], axis=1))
    fin = softmax_step(fin, lgs, [[prev_sel[hh], None] for hh in range(2)], v_ts)
    outs = [(fin[3 * hh + 2] / fin[3 * hh + 1]).T for hh in range(2)]
    o_ref[...] = jnp.concatenate(outs, axis=1).astype(BF16)


def _moba_prompt(z, bias_tab, n_seq, t):
    blk = MOBA_BLOCK
    nb = t // blk
    assert nb % MOBA_KEY_GROUP == 0
    hp = ATT_HEADS // 2
    kc, vc = OFF_AK // LANES, OFF_AV // LANES
    return pl.pallas_call(
        functools.partial(_moba_p_body, nb=nb),
        grid=(n_seq, hp, nb),
        in_specs=[pl.BlockSpec((blk, LANES), lambda b, h, i: (b * nb + i, h)),
                  pl.BlockSpec((t, LANES), lambda b, h, i: (b, kc + h)),
                  pl.BlockSpec((t, LANES), lambda b, h, i: (b, vc + h)),
                  pl.BlockSpec((2, 2, blk, blk), lambda b, h, i: (h, 0, 0, 0))],
        out_specs=pl.BlockSpec((blk, LANES), lambda b, h, i: (b * nb + i, h)),
        out_shape=jax.ShapeDtypeStruct((n_seq * t, ATT_W), BF16),
        scratch_shapes=[pltpu.VMEM((2, t, ATT_DH), BF16), pltpu.VMEM((LANES, t), BF16),
                        pltpu.VMEM((nb, LANES), F32), pltpu.VMEM((2, nb, blk), F32)],
        compiler_params=_params(("arbitrary", "arbitrary", "arbitrary")),
    )(z, z, z, bias_tab)


PAGES_PER_STEP = 8
PAGES_PER_BLOCK = MOBA_BLOCK // PAGE_SIZE


def _moba_s_body(pt_ref, *refs, n_steps):
    del pt_ref
    kpages, vpages = refs[:PAGES_PER_STEP], refs[PAGES_PER_STEP:2 * PAGES_PER_STEP]
    q_ref, kn_ref, vn_ref, rbt_ref, bk_ref, o_ref, m_s, l_s, s_s, acc_s = refs[2 * PAGES_PER_STEP:]
    g = pl.program_id(1)
    h, dh, w = ATT_HEADS, ATT_DH, ATT_W
    n_pages = n_steps * PAGES_PER_STEP
    row = lax.broadcasted_iota(jnp.int32, (h, w), 0)
    lane = lax.broadcasted_iota(jnp.int32, (h, w), 1)
    own_head = (lane // dh) == row
    qbd = jnp.where(own_head, q_ref[...] * (dh ** -0.5 * LOG2E), 0.0)
    q_hi = qbd.astype(BF16)
    q_lo = (qbd - q_hi.astype(F32)).astype(BF16)
    q2 = jnp.concatenate([q_hi, q_lo], axis=0)
    far_bias = rbt_ref[:, REL_BUCKETS - 1:REL_BUCKETS] * LOG2E
    bk = bk_ref[...]
    near_bias = jnp.zeros((h, PAGE_SIZE), F32)
    for b in range(REL_BUCKETS):
        near_bias = jnp.where(bk == b, rbt_ref[:, b:b + 1], near_bias)
    is_last = (g == n_steps - 1).astype(F32)
    last_bias = far_bias + is_last * (near_bias * LOG2E - far_bias)

    raws = []
    for j in range(PAGES_PER_STEP):
        raw2 = _dot(q2, kpages[j][...].reshape(w, PAGE_SIZE).astype(BF16))
        raws.append(raw2[:h] + raw2[h:])
    stats = []
    for j, raw in enumerate(raws):
        lg = raw + (last_bias if j == PAGES_PER_STEP - 1 else far_bias)
        mb = jnp.max(lg, axis=1, keepdims=True)
        p = jnp.exp2(lg - mb)
        stats.append((mb, jnp.sum(p, axis=1, keepdims=True), jnp.sum(raw, axis=1, keepdims=True), p.astype(BF16)))
    for j, (mb, lb, sb, p) in enumerate(stats):
        idx = g * PAGES_PER_STEP + j
        m_s[idx] = mb
        l_s[idx] = lb
        s_s[idx] = sb
        acc_s[idx] = _dot_nt(p, vpages[j][...].reshape(w, PAGE_SIZE).astype(BF16))

    @pl.when(g == n_steps - 1)
    def _():
        nblk = n_pages // PAGES_PER_BLOCK
        s = jnp.sum(s_s[...].reshape(nblk, PAGES_PER_BLOCK, h, 1), axis=1)
        nidx = lax.broadcasted_iota(jnp.int32, s.shape, 0)
        sel = _topk_select(s, nidx >= 0, nidx, nblk)
        selp = jnp.broadcast_to(sel[:, None], (nblk, PAGES_PER_BLOCK, h, 1)).reshape(n_pages, h, 1) > 0.5
        ln = jnp.sum(qbd * kn_ref[...], axis=1, keepdims=True) + rbt_ref[:, 0:1] * LOG2E
        mm = jnp.where(selp, m_s[...], NEG)
        mx = jnp.maximum(jnp.max(mm, axis=0), ln)
        wgt = jnp.where(selp, jnp.exp2(mm - mx), 0.0)
        wn = jnp.exp2(ln - mx)
        den = jnp.sum(wgt * l_s[...], axis=0) + wn
        tot = jnp.sum(wgt * acc_s[...], axis=0) + wn * vn_ref[...]
        o_ref[...] = jnp.sum(jnp.where(own_head, tot / den, 0.0), axis=0, keepdims=True)


def _moba_sample(z_s, cache_k, cache_v, layer, page_table, rel_bias_t):
    n_seq, n_pages = page_table.shape
    assert n_pages % PAGES_PER_STEP == 0 and PAGES_PER_STEP % PAGES_PER_BLOCK == 0
    n_steps = n_pages // PAGES_PER_STEP
    assert int(_bucket_np(np.array([PAGE_SIZE + 1]))[0]) == REL_BUCKETS - 1
    near_buckets = _bucket_np(PAGE_SIZE - np.arange(PAGE_SIZE)).reshape(1, PAGE_SIZE)
    rows = lambda off: z_s[:, off:off + ATT_W].reshape(n_seq, 1, ATT_W)

    def page_spec(j):
        return pl.BlockSpec((None, None, ATT_HEADS, ATT_DH, PAGE_SIZE),
                            lambda s, g, pt: (layer, pt[s, g * PAGES_PER_STEP + j], 0, 0, 0))

    row_spec = pl.BlockSpec((None, 1, ATT_W), lambda s, g, pt: (s, 0, 0))
    stat = pltpu.VMEM((n_pages, ATT_HEADS, 1), F32)
    grid_spec = pltpu.PrefetchScalarGridSpec(
        num_scalar_prefetch=1,
        grid=(n_seq, n_steps),
        in_specs=[page_spec(j) for j in range(PAGES_PER_STEP)] * 2 + [
            row_spec, row_spec, row_spec,
            pl.BlockSpec((ATT_HEADS, REL_BUCKETS), lambda s, g, pt: (0, 0)),
            pl.BlockSpec((1, PAGE_SIZE), lambda s, g, pt: (0, 0))],
        out_specs=row_spec,
        scratch_shapes=[stat, stat, stat, pltpu.VMEM((n_pages, ATT_HEADS, ATT_W), F32)],
    )
    out = pl.pallas_call(
        functools.partial(_moba_s_body, n_steps=n_steps),
        grid_spec=grid_spec,
        out_shape=jax.ShapeDtypeStruct((n_seq, 1, ATT_W), F32),
        compiler_params=_params(("arbitrary", "arbitrary")),
    )(page_table, *([cache_k] * PAGES_PER_STEP), *([cache_v] * PAGES_PER_STEP),
      rows(OFF_AQ), rows(OFF_AK), rows(OFF_AV), rel_bias_t, jnp.asarray(near_buckets))
    return out.reshape(n_seq, ATT_W)


def _rope_body(inv_ref, sgn_ref, cos_ref, sin_ref, *, pos0, step, rows):
    r = lax.broadcasted_iota(jnp.int32, (rows, LANES), 0) + pl.program_id(0) * rows
    ang = (pos0 + step * r).astype(F32) * inv_ref[...]
    cos_ref[...] = jnp.cos(ang)
    sin_ref[...] = jnp.sin(ang) * sgn_ref[...]


def _rope_tables(n_rows, pos0, step):
    half = RET_DK // 2
    inv = 1.0 / (ROPE_BASE ** jnp.linspace(0.0, 1.0, half))
    inv_row = jnp.tile(inv, LANES // half).reshape(1, LANES).astype(F32)
    sgn = np.where((np.arange(LANES) % RET_DK) < half, -1.0, 1.0).astype(np.float32).reshape(1, LANES)
    rows = min(n_rows, 512)
    spec = pl.BlockSpec((rows, LANES), lambda i: (i, 0))
    cst = pl.BlockSpec((1, LANES), lambda i: (0, 0))
    return pl.pallas_call(
        functools.partial(_rope_body, pos0=pos0, step=step, rows=rows),
        grid=(n_rows // rows,),
        in_specs=[cst, cst],
        out_specs=[spec, spec],
        out_shape=[jax.ShapeDtypeStruct((n_rows, LANES), F32)] * 2,
        compiler_params=_params(("arbitrary",)),
    )(inv_row, jnp.asarray(sgn))


def _rotary128(x, cos, sin_signed):
    half = RET_DK // 2
    lane = lax.broadcasted_iota(jnp.int32, x.shape, 1)
    partner = jnp.where((lane % RET_DK) < half,
                        pltpu.roll(x, LANES - half, 1), pltpu.roll(x, half, 1))
    return x * cos + partner * sin_signed


def _groupnorm_gate(o, g):
    mu = jnp.mean(o, axis=-1, keepdims=True)
    var = jnp.mean((o - mu) ** 2, axis=-1, keepdims=True)
    return _silu(g) * ((o - mu) * lax.rsqrt(var + GN_EPS))


def _ret_log_decay():
    return jnp.log(1.0 - 2.0 ** (-5.0 - jnp.arange(RET_HEADS, dtype=F32)))


def _ret_p_body(lg_ref, q_ref, k_ref, v_ref, g_ref, cos_ref, sin_ref, o_ref, st_ref, *, chunk):
    hp = pl.program_id(1)
    c = pl.program_id(2)

    @pl.when(c == 0)
    def _():
        st_ref[...] = jnp.zeros(st_ref.shape, F32)

    cos, sin = cos_ref[...], sin_ref[...]
    q = _rotary128(q_ref[...], cos, sin)
    k = _rotary128(k_ref[...], cos, sin) * RET_DK ** -0.5
    ii = lax.broadcasted_iota(jnp.int32, (chunk, chunk), 0)
    jj = lax.broadcasted_iota(jnp.int32, (chunk, chunk), 1)
    diff = (ii - jj).astype(F32)
    ri = lax.broadcasted_iota(jnp.int32, (chunk, 1), 0).astype(F32)
    for hh in range(2):
        lgh = lg_ref[hp * 2 + hh]
        dmask = jnp.where(diff >= 0, jnp.exp(lgh * jnp.maximum(diff, 0.0)), 0.0)
        qh = q[:, hh * RET_DK:(hh + 1) * RET_DK].astype(BF16)
        kh = k[:, hh * RET_DK:(hh + 1) * RET_DK]
        vh = v_ref[:, hh * RET_DV:(hh + 1) * RET_DV].astype(BF16)
        s0 = st_ref[hh]
        a = _dot_nt(qh, kh.astype(BF16)) * dmask
        inner = _dot(a.astype(BF16), vh)
        cross = _dot(qh, s0.astype(BF16)) * jnp.exp(lgh * (ri + 1.0))
        kdec = (kh * jnp.exp(lgh * (chunk - 1.0 - ri))).astype(BF16)
        st_ref[hh] = jnp.exp(lgh * chunk + jnp.zeros((1, 1), F32)) * s0 + _dot_tn(kdec, vh)
        gh = g_ref[:, hh * RET_DV:(hh + 1) * RET_DV]
        o_ref[:, hh * RET_DV:(hh + 1) * RET_DV] = _groupnorm_gate(inner + cross, gh).astype(BF16)


def _ret_prompt(z, cos_tab, sin_tab, log_decay, n_seq, t):
    chunk = RET_CHUNK
    nc = t // chunk
    hp = RET_HEADS // 2
    qc, kc = OFF_RQ // LANES, OFF_RK // LANES
    vc, gc = OFF_RV // (2 * RET_DV), OFF_RG // (2 * RET_DV)
    return pl.pallas_call(
        functools.partial(_ret_p_body, chunk=chunk),
        grid=(n_seq, hp, nc),
        in_specs=[pl.BlockSpec(memory_space=pltpu.SMEM),
                  pl.BlockSpec((chunk, LANES), lambda b, h, c: (b * nc + c, qc + h)),
                  pl.BlockSpec((chunk, LANES), lambda b, h, c: (b * nc + c, kc + h)),
                  pl.BlockSpec((chunk, 2 * RET_DV), lambda b, h, c: (b * nc + c, vc + h)),
                  pl.BlockSpec((chunk, 2 * RET_DV), lambda b, h, c: (b * nc + c, gc + h)),
                  pl.BlockSpec((chunk, LANES), lambda b, h, c: (c, 0)),
                  pl.BlockSpec((chunk, LANES), lambda b, h, c: (c, 0))],
        out_specs=[pl.BlockSpec((chunk, 2 * RET_DV), lambda b, h, c: (b * nc + c, h)),
                   pl.BlockSpec((None, 2, RET_DK, RET_DV), lambda b, h, c: (b, h, 0, 0))],
        out_shape=[jax.ShapeDtypeStruct((n_seq * t, RET_V_W), BF16),
                   jax.ShapeDtypeStruct((n_seq, RET_HEADS, RET_DK, RET_DV), F32)],
        compiler_params=_params(("arbitrary", "arbitrary", "arbitrary")),
    )(log_decay, z, z, z, z, cos_tab, sin_tab)


def _ret_s_prep_body(q_ref, k_ref, cos_ref, sin_ref, qo_ref, ko_ref):
    cos, sin = cos_ref[0:1, :], sin_ref[0:1, :]
    for j in range(RET_QK_W // LANES):
        sl = slice(j * LANES, (j + 1) * LANES)
        qo_ref[:, sl] = _rotary128(q_ref[:, sl], cos, sin)
        ko_ref[:, sl] = _rotary128(k_ref[:, sl], cos, sin) * RET_DK ** -0.5


def _ret_s_body(lg_ref, q_ref, k_ref, v_ref, g_ref, s0_ref, o_ref, sn_ref):
    gam = jnp.exp(lg_ref[pl.program_id(1)] + jnp.zeros((1, 1, LANES), F32))
    q, k, v, s0 = q_ref[...], k_ref[...], v_ref[...], s0_ref[...]
    o = jnp.sum(q * s0, axis=1, keepdims=True) * gam + jnp.sum(q * k, axis=1, keepdims=True) * v
    sn_ref[...] = gam * s0 + k * v
    o_ref[...] = _groupnorm_gate(o, g_ref[...])


def _ret_sample(z_s, state, cos_tab, sin_tab, log_decay):
    n_seq = z_s.shape[0]
    tab = pl.BlockSpec((SUBLANES, LANES), lambda i: (0, 0))
    q_rot, k_rot = pl.pallas_call(
        _ret_s_prep_body,
        grid=(1,),
        in_specs=[pl.BlockSpec((n_seq, RET_QK_W), lambda i: (0, OFF_RQ // RET_QK_W)),
                  pl.BlockSpec((n_seq, RET_QK_W), lambda i: (0, OFF_RK // RET_QK_W)), tab, tab],
        out_specs=[pl.BlockSpec((n_seq, RET_QK_W), lambda i: (0, 0))] * 2,
        out_shape=[jax.ShapeDtypeStruct((n_seq, RET_QK_W), F32)] * 2,
        compiler_params=_params(("arbitrary",)),
    )(z_s, z_s, cos_tab, sin_tab)
    col = lambda a: a.reshape(n_seq, RET_HEADS, RET_DK, 1)
    row = lambda a: a.reshape(n_seq, RET_HEADS, 1, RET_DV)
    bt = 16
    col_spec = pl.BlockSpec((bt, None, RET_DK, 1), lambda i, h: (i, h, 0, 0))
    row_spec = pl.BlockSpec((bt, None, 1, RET_DV), lambda i, h: (i, h, 0, 0))
    st_spec = pl.BlockSpec((bt, None, RET_DK, RET_DV), lambda i, h: (i, h, 0, 0))
    state_all, layer = state
    st_in_spec = pl.BlockSpec((None, bt, None, RET_DK, RET_DV), lambda i, h: (layer, i, h, 0, 0))
    o, s_new = pl.pallas_call(
        _ret_s_body,
        grid=(n_seq // bt, RET_HEADS),
        in_specs=[pl.BlockSpec(memory_space=pltpu.SMEM), col_spec, col_spec, row_spec, row_spec, st_in_spec],
        out_specs=[row_spec, st_spec],
        out_shape=[jax.ShapeDtypeStruct((n_seq, RET_HEADS, 1, RET_DV), F32),
                   jax.ShapeDtypeStruct((n_seq, RET_HEADS, RET_DK, RET_DV), F32)],
        compiler_params=_params(("arbitrary", "arbitrary")),
    )(log_decay, col(q_rot), col(k_rot), row(z_s[:, OFF_RV:OFF_RV + RET_V_W]),
      row(z_s[:, OFF_RG:OFF_RG + RET_V_W]), state_all)
    return o.reshape(n_seq, RET_V_W), s_new


def _s5_prep_body(are_ref, aim_ref, ldt_ref, bre_ref, bim_ref, pwr_ref, pwi_ref, bbr_ref, bbi_ref):
    ar, ai = are_ref[...], aim_ref[...]
    dt = jnp.exp(ldt_ref[...])
    kk = (lax.broadcasted_iota(jnp.int32, pwr_ref.shape, 1) + 1).astype(F32)
    mag = jnp.exp(ar * dt * kk)
    ang = ai * dt * kk
    pwr_ref[...] = mag * jnp.cos(ang)
    pwi_ref[...] = mag * jnp.sin(ang)
    mag1 = jnp.exp(ar * dt)
    nr = mag1 * jnp.cos(ai * dt) - 1.0
    ni = mag1 * jnp.sin(ai * dt)
    den = ar * ar + ai * ai
    cr = (nr * ar + ni * ai) / den
    ci = (ni * ar - nr * ai) / den
    bre, bim = bre_ref[...], bim_ref[...]
    bbr_ref[...] = cr * bre - ci * bim
    bbi_ref[...] = cr * bim + ci * bre


def _s5_prepare(a_re, a_im, log_dt, b_re, b_im, c_re, c_im, d):
    g, p, c = S5_GROUPS, S5_STATE, S5_GROUP
    g3 = lambda a: a.reshape(g, 1, p)
    ldt = jnp.broadcast_to(log_dt.reshape(g, 1, 1), (g, 1, p))
    pwr, pwi, bbr, bbi = pl.pallas_call(
        _s5_prep_body,
        out_shape=[jax.ShapeDtypeStruct((g, SUBLANES, p), F32)] * 2
        + [jax.ShapeDtypeStruct((g, c, p), F32)] * 2,
    )(g3(a_re), g3(a_im), ldt, b_re.transpose(0, 2, 1), b_im.transpose(0, 2, 1))
    eye = jnp.eye(S5_GT, dtype=F32)
    nt = S5_NT

    def in_tile(bb):
        return jnp.einsum("tgcp,gh->tgchp", bb.reshape(nt, S5_GT, c, p), eye).reshape(nt, S5_GT * c, S5_GT * p)

    def out_tile(cc):
        return jnp.einsum("tgcp,gh->tgphc", cc.reshape(nt, S5_GT, c, p), eye).reshape(nt, S5_GT * p, S5_GT * c)

    b_tile = jnp.concatenate([in_tile(bbr), in_tile(bbi)], axis=2).astype(BF16)
    c_tile = jnp.concatenate([out_tile(c_re), out_tile(-c_im)], axis=1).astype(BF16)
    d_tile = d.reshape(nt, 1, S5_GT * c)

    def pw_tile(pw):
        return pw.reshape(nt, S5_GT, SUBLANES, p).transpose(0, 2, 1, 3).reshape(nt, SUBLANES, S5_GT * p)

    pr, pi = pw_tile(pwr), pw_tile(pwi)
    rows = jnp.arange(SUBLANES)[None, :, None]
    slabs = []
    for shift in (1, 2, 4):
        keep = rows >= shift
        slabs += [jnp.where(keep, pr[:, shift - 1:shift, :], 0.0), jnp.where(keep, pi[:, shift - 1:shift, :], 0.0)]
    scan_c = jnp.stack(slabs + [pr, pi], axis=1)
    lam1 = jnp.stack([pr[:, 0:1, :], pi[:, 0:1, :]], axis=1)
    return b_tile, c_tile, d_tile, scan_c, lam1


def _gelu_tanh(y):
    return 0.5 * y * (1.0 + jnp.tanh(math.sqrt(2.0 / math.pi) * (y + 0.044715 * (y * y * y))))


def _s5_p_body(u_ref, bt_ref, ct_ref, d_ref, sc_ref, z_ref, st_ref, x_ref, carry_ref, *, tc):
    ns = S5_TILE_STATES

    @pl.when(pl.program_id(2) == 0)
    def _():
        carry_ref[...] = jnp.zeros(carry_ref.shape, F32)

    u = u_ref[...]
    x_ref[...] = _dot(u.astype(BF16), bt_ref[...])

    def tile(t, carry):
        cr, ci = carry
        st = pl.multiple_of(t * SUBLANES, SUBLANES)
        xr = x_ref[pl.ds(st, SUBLANES), 0:ns]
        xi = x_ref[pl.ds(st, SUBLANES), ns:2 * ns]
        for s, shift in enumerate((1, 2, 4)):
            ar, ai = sc_ref[2 * s], sc_ref[2 * s + 1]
            sr, si = pltpu.roll(xr, shift, 0), pltpu.roll(xi, shift, 0)
            xr, xi = xr + ar * sr - ai * si, xi + ar * si + ai * sr
        pr, pi = sc_ref[6], sc_ref[7]
        xr, xi = xr + pr * cr - pi * ci, xi + pr * ci + pi * cr
        x_ref[pl.ds(st, SUBLANES), 0:ns] = xr
        x_ref[pl.ds(st, SUBLANES), ns:2 * ns] = xi
        return xr[SUBLANES - 1:SUBLANES, :], xi[SUBLANES - 1:SUBLANES, :]

    cr, ci = lax.fori_loop(0, tc // SUBLANES, tile, (carry_ref[:, 0:ns], carry_ref[:, ns:2 * ns]))
    last = jnp.concatenate([cr, ci], axis=1)
    carry_ref[...] = last
    st_ref[...] = last
    y = _dot(x_ref[...].astype(BF16), ct_ref[...]) + d_ref[...] * u
    z_ref[...] = _gelu_tanh(y)


def _s5_prompt(z, tiles, n_seq, t):
    b_tile, c_tile, d_tile, scan_c, _ = tiles
    tc = 1024
    nc = t // tc
    uc = OFF_SU // LANES
    ns2 = 2 * S5_TILE_STATES
    zs, st = pl.pallas_call(
        functools.partial(_s5_p_body, tc=tc),
        grid=(n_seq, S5_NT, nc),
        in_specs=[pl.BlockSpec((tc, LANES), lambda b, g, c: (b * nc + c, uc + g)),
                  pl.BlockSpec((None, LANES, ns2), lambda b, g, c: (g, 0, 0)),
                  pl.BlockSpec((None, ns2, LANES), lambda b, g, c: (g, 0, 0)),
                  pl.BlockSpec((None, 1, LANES), lambda b, g, c: (g, 0, 0)),
                  pl.BlockSpec((None, 8, SUBLANES, S5_TILE_STATES), lambda b, g, c: (g, 0, 0, 0))],
        out_specs=[pl.BlockSpec((tc, LANES), lambda b, g, c: (b * nc + c, g)),
                   pl.BlockSpec((None, None, 1, ns2), lambda b, g, c: (b, g, 0, 0))],
        out_shape=[jax.ShapeDtypeStruct((n_seq * t, S5_W), F32),
                   jax.ShapeDtypeStruct((n_seq, S5_NT, 1, ns2), F32)],
        scratch_shapes=[pltpu.VMEM((tc, ns2), F32), pltpu.VMEM((1, ns2), F32)],
        compiler_params=_params(("arbitrary", "arbitrary", "arbitrary")),
    )(z, b_tile, c_tile, d_tile, scan_c)
    s_re = st[:, :, 0, :S5_TILE_STATES].reshape(n_seq, S5_GROUPS, S5_STATE)
    s_im = st[:, :, 0, S5_TILE_STATES:].reshape(n_seq, S5_GROUPS, S5_STATE)
    return zs, s_re, s_im


def _s5_s_body(u_ref, bt_ref, ct_ref, d_ref, l1_ref, x0_ref, z_ref, xn_ref):
    ns = S5_TILE_STATES
    u = u_ref[...]
    bu = _dot(u.astype(BF16), bt_ref[...])
    lr, li = l1_ref[0], l1_ref[1]
    x0r, x0i = x0_ref[:, 0:ns], x0_ref[:, ns:2 * ns]
    x = jnp.concatenate([bu[:, 0:ns] + lr * x0r - li * x0i, bu[:, ns:2 * ns] + lr * x0i + li * x0r], axis=1)
    xn_ref[...] = x
    z_ref[...] = _gelu_tanh(_dot(x.astype(BF16), ct_ref[...]) + d_ref[...] * u)


def _s5_sample(z_s, tiles, x0_re, x0_im):
    b_tile, c_tile, d_tile, _, lam1 = tiles
    n_seq = z_s.shape[0]
    uc = OFF_SU // LANES
    ns2 = 2 * S5_TILE_STATES

    def to_tiles(a):
        return a.reshape(n_seq, S5_NT, S5_TILE_STATES).transpose(1, 0, 2)

    x0 = jnp.concatenate([to_tiles(x0_re), to_tiles(x0_im)], axis=2)
    zs, xn = pl.pallas_call(
        _s5_s_body,
        grid=(S5_NT,),
        in_specs=[pl.BlockSpec((n_seq, LANES), lambda g: (0, uc + g)),
                  pl.BlockSpec((None, LANES, ns2), lambda g: (g, 0, 0)),
                  pl.BlockSpec((None, ns2, LANES), lambda g: (g, 0, 0)),
                  pl.BlockSpec((None, 1, LANES), lambda g: (g, 0, 0)),
                  pl.BlockSpec((None, 2, 1, S5_TILE_STATES), lambda g: (g, 0, 0, 0)),
                  pl.BlockSpec((None, n_seq, ns2), lambda g: (g, 0, 0))],
        out_specs=[pl.BlockSpec((n_seq, LANES), lambda g: (0, g)),
                   pl.BlockSpec((None, n_seq, ns2), lambda g: (g, 0, 0))],
        out_shape=[jax.ShapeDtypeStruct((n_seq, S5_W), F32),
                   jax.ShapeDtypeStruct((S5_NT, n_seq, ns2), F32)],
        compiler_params=_params(("arbitrary",)),
    )(z_s, b_tile, c_tile, d_tile, lam1, x0)

    def from_tiles(a):
        return a.transpose(1, 0, 2).reshape(n_seq, S5_GROUPS, S5_STATE)

    return zs, from_tiles(xn[:, :, :S5_TILE_STATES]), from_tiles(xn[:, :, S5_TILE_STATES:])


def _glu_body(zf_ref, zc_ref, w_ref, o_ref):
    o_ref[...] = (zc_ref[...] * _sigmoid(_dot(zf_ref[...].astype(BF16), w_ref[...]))).astype(BF16)


def _half_glu(zs, w_glu, tm, tn):
    m, w = zs.shape
    w_bf, layer = w_glu
    return pl.pallas_call(
        _glu_body,
        grid=(m // tm, w // tn),
        in_specs=[pl.BlockSpec((tm, w), lambda i, j: (i, 0)),
                  pl.BlockSpec((tm, tn), lambda i, j: (i, j)),
                  pl.BlockSpec((None, w, tn), lambda i, j: (layer, 0, j))],
        out_specs=pl.BlockSpec((tm, tn), lambda i, j: (i, j)),
        out_shape=jax.ShapeDtypeStruct((m, w), BF16),
        compiler_params=_params(("arbitrary", "arbitrary")),
    )(zs, zs, w_bf)


def _kv_t_body(z_ref, o_ref):
    zt = z_ref[...].T
    for hh in range(2):
        o_ref[hh] = zt[hh * ATT_DH:(hh + 1) * ATT_DH, :]


def _kv_transposed(z, off, n_seq, t):
    tt = 512
    nt = t // tt
    col0 = off // LANES
    return pl.pallas_call(
        _kv_t_body,
        grid=(n_seq, ATT_HEADS // 2, nt),
        in_specs=[pl.BlockSpec((tt, LANES), lambda b, h, i: (b * nt + i, col0 + h))],
        out_specs=pl.BlockSpec((None, 2, ATT_DH, tt), lambda b, h, i: (b, h, 0, i)),
        out_shape=jax.ShapeDtypeStruct((n_seq, ATT_HEADS, ATT_DH, t), F32),
        compiler_params=_params(("arbitrary", "arbitrary", "arbitrary")),
    )(z)


def _layer(x, mods, lw, mix, kind, tm, tiles_per_seq):
    norm1, norm2, w_in, w_branch, w_out, w_ffn_in, w_ffn_out = lw
    sh1, sc1, g1, sh2, sc2, g2 = mods
    z = _norm_mod_matmul(x, norm1, sc1, sh1, w_in, kind, tm, 1024, tiles_per_seq)
    (o_att, o_ret, o_s5), state = mix(z)
    merged = _branch_merge(o_att, o_ret, o_s5, z, w_branch, tm, 512)
    x = _matmul_residual(merged, w_out, x, g1, kind, tm, 1024, tiles_per_seq)
    act = _norm_mod_ffn_in(x, norm2, sc2, sh2, w_ffn_in, kind, tm, 512, tiles_per_seq)
    tm_out = min(tm, 512)
    x = _matmul_residual(act, w_ffn_out, x, g2, kind, tm_out, 1024, tiles_per_seq * (tm // tm_out))
    return x, z, state


def kernel(x_prompt, x_sample, c_prompt, c_sample, cache_k, cache_v, page_table, state_ret, state_s5_re, state_s5_im, rel_bias, norm1_g, norm2_g, w_ada, b_ada, w_in, s5_a_re, s5_a_im, s5_log_dt, s5_b_re, s5_b_im, s5_c_re, s5_c_im, s5_d, w_glu, w_branch, w_out, w_ffn_in, w_ffn_out, final_g):
    n_seq, t, d = x_prompt.shape
    n_dec = x_sample.shape[0]
    depth = w_in.shape[0]
    past = page_table.shape[1] * PAGE_SIZE
    xp = x_prompt.reshape(n_seq * t, d)
    xs = x_sample.reshape(n_dec, d)
    tm_p = 1024
    tiles_per_seq = t // tm_p

    bias_p = _bias_tables_prompt(rel_bias)
    rel_bias_t = rel_bias.T
    cache_kt = cache_k.transpose(0, 1, 3, 4, 2)
    cache_vt = cache_v.transpose(0, 1, 3, 4, 2)
    cos_p, sin_p = _rope_tables(t, 0, 1)
    cos_s, sin_s = _rope_tables(SUBLANES, past, 0)
    log_decay = _ret_log_decay()
    pad = (-(n_seq + n_dec)) % SUBLANES
    c_all = jnp.concatenate([c_prompt, c_sample, jnp.zeros((pad, d), F32)], axis=0)

    w_in_bf, w_branch_bf, w_out_bf = w_in.astype(BF16), w_branch.astype(BF16), w_out.astype(BF16)
    w_ffn_in_bf, w_ffn_out_bf, w_glu_bf = w_ffn_in.astype(BF16), w_ffn_out.astype(BF16), w_glu.astype(BF16)

    new_p, new_s = [], []
    for l in range(depth):
        mod = _ada(c_all, w_ada, b_ada[l], l)
        mod_p = mod[:n_seq].reshape(n_seq, 6, 1, d)
        mod_s = mod[n_seq:n_seq + n_dec].reshape(n_dec, 6, d)
        mods_p = tuple(mod_p[:, i] for i in range(6))
        mods_s = tuple(mod_s[:, i] for i in range(6))
        lw = (norm1_g[l], norm2_g[l], (w_in_bf, l), (w_branch_bf, l), (w_out_bf, l),
              (w_ffn_in_bf, l), (w_ffn_out_bf, l))
        glu_w = (w_glu_bf, l)
        tiles = _s5_prepare(s5_a_re[l], s5_a_im[l], s5_log_dt[l], s5_b_re[l], s5_b_im[l],
                            s5_c_re[l], s5_c_im[l], s5_d[l])

        def mix_p(z):
            o_att = _moba_prompt(z, bias_p, n_seq, t)
            o_ret, s_ret = _ret_prompt(z, cos_p, sin_p, log_decay, n_seq, t)
            zs5, s_re, s_im = _s5_prompt(z, tiles, n_seq, t)
            return (o_att, o_ret, _half_glu(zs5, glu_w, tm_p, 512)), (s_ret, s_re, s_im)

        def mix_s(z, l=l):
            o_att = _moba_sample(z, cache_kt, cache_vt, l, page_table, rel_bias_t)
            o_ret, s_ret = _ret_sample(z, (state_ret, l), cos_s, sin_s, log_decay)
            zs5, s_re, s_im = _s5_sample(z, tiles, state_s5_re[l], state_s5_im[l])
            return (o_att, o_ret, _half_glu(zs5, glu_w, n_dec, 512)), (s_ret, s_re, s_im)

        xp, zp, st_p = _layer(xp, mods_p, lw, mix_p, "prompt", tm_p, tiles_per_seq)
        xs, zs, st_s = _layer(xs, mods_s, lw, mix_s, "sample", n_dec, 1)
        kv = lambda z, n, tt, off: z[:, off:off + ATT_W].reshape(n, tt, ATT_HEADS, ATT_DH)
        new_p.append((_kv_transposed(zp, OFF_AK, n_seq, t), _kv_transposed(zp, OFF_AV, n_seq, t)) + st_p)
        new_s.append((kv(zs, n_dec, 1, OFF_AK), kv(zs, n_dec, 1, OFF_AV)) + st_s)

    y_prompt = _final_norm(xp, final_g, tm_p).reshape(n_seq, t, d)
    y_sample = _final_norm(xs, final_g, n_dec).reshape(n_dec, 1, d)
    outs_p = [jnp.stack(a) for a in zip(*new_p)]
    outs_p[0], outs_p[1] = (a.transpose(0, 1, 4, 2, 3) for a in outs_p[:2])
    outs_s = [jnp.stack(a) for a in zip(*new_s)]
    return (y_prompt, y_sample, *outs_p, *outs_s)
```

```python
import functools
import math

import numpy as np
import jax
import jax.numpy as jnp
from jax import lax
from jax.experimental import pallas as pl
from jax.experimental.pallas import tpu as pltpu

F32 = jnp.float32
BF16 = jnp.bfloat16
HIGHEST = lax.Precision.HIGHEST

D_MODEL = 2048
PAGE_SIZE = 128
ATT_HEADS = 16
ATT_DH = 64
ATT_W = ATT_HEADS * ATT_DH
MOBA_BLOCK = 256
MOBA_TOPK = 3
REL_BUCKETS = 32
REL_MAX_DIST = 128
RET_HEADS = 8
RET_DK = 64
RET_DV = 128
RET_QK_W = RET_HEADS * RET_DK
RET_V_W = RET_HEADS * RET_DV
RET_CHUNK = 256
ROPE_BASE = 10000.0
S5_W = 1024
S5_GROUP = 16
S5_GROUPS = S5_W // S5_GROUP
S5_STATE = 64
N_BRANCH = 3
BRANCH_W = 1024
D_FF = ((8 * D_MODEL + 3 * 256 - 1) // (3 * 256)) * 256
NORM_EPS = 1e-6
GN_EPS = 1e-5
D_IN = 3 * ATT_W + 2 * RET_QK_W + 2 * RET_V_W + S5_W + N_BRANCH * D_MODEL

OFF_AQ, OFF_AK, OFF_AV = 0, ATT_W, 2 * ATT_W
OFF_RQ = 3 * ATT_W
OFF_RK = OFF_RQ + RET_QK_W
OFF_RV = OFF_RK + RET_QK_W
OFF_RG = OFF_RV + RET_V_W
OFF_SU = OFF_RG + RET_V_W
OFF_GATE = OFF_SU + S5_W

LANES = 128
SUBLANES = 8
VMEM_LIMIT = 56 * 1024 * 1024
NEG = -1e30
LOG2E = math.log2(math.e)
MOBA_KEY_GROUP = 4
S5_GT = 8
S5_TILE_STATES = S5_GT * S5_STATE
S5_NT = S5_GROUPS // S5_GT


def _params(sem):
    return pltpu.CompilerParams(dimension_semantics=sem, vmem_limit_bytes=VMEM_LIMIT)


def _sigmoid(x):
    return 1.0 / (1.0 + jnp.exp(-x))


def _silu(x):
    return x * _sigmoid(x)


def _dot(a, b):
    return jnp.dot(a, b, preferred_element_type=F32)


def _dot_nt(a, b, precision=None):
    return lax.dot_general(a, b, (((1,), (1,)), ((), ())), precision=precision,
                           preferred_element_type=F32)


def _dot_tn(a, b):
    return lax.dot_general(a, b, (((0,), (0,)), ((), ())), preferred_element_type=F32)


def _mod_spec(kind, tm, tn, tiles_per_seq, col_blocked):
    if kind == "prompt":
        if col_blocked:
            return pl.BlockSpec((None, 1, tn), lambda i, j: (i // tiles_per_seq, 0, j))
        return pl.BlockSpec((None, 1, tn), lambda i, j: (i // tiles_per_seq, 0, 0))
    if col_blocked:
        return pl.BlockSpec((tm, tn), lambda i, j: (i, j))
    return pl.BlockSpec((tm, tn), lambda i, j: (i, 0))


def _ada_body(c_ref, w_ref, b_ref, o_ref):
    a = _silu(c_ref[...]).astype(BF16)
    o_ref[...] = _dot(a, w_ref[...].astype(BF16)) + b_ref[...]


def _ada(c_all, w_all, b, layer):
    m, d = c_all.shape
    n = w_all.shape[2]
    tn = 1024
    return pl.pallas_call(
        _ada_body,
        grid=(n // tn,),
        in_specs=[pl.BlockSpec((m, d), lambda j: (0, 0)),
                  pl.BlockSpec((None, d, tn), lambda j: (layer, 0, j)),
                  pl.BlockSpec((1, tn), lambda j: (0, j))],
        out_specs=pl.BlockSpec((m, tn), lambda j: (0, j)),
        out_shape=jax.ShapeDtypeStruct((m, n), F32),
        compiler_params=_params(("arbitrary",)),
    )(c_all, w_all, b.reshape(1, n))


def _norm_mod(x, g, sc, sh):
    y = x * lax.rsqrt(jnp.mean(x * x, axis=-1, keepdims=True) + NORM_EPS)
    return (y * g) * (1.0 + sc) + sh


def _nmm_body(x_ref, g_ref, sc_ref, sh_ref, w_ref, o_ref, h_ref):
    @pl.when(pl.program_id(1) == 0)
    def _():
        h_ref[...] = _norm_mod(x_ref[...], g_ref[...], sc_ref[...], sh_ref[...]).astype(BF16)

    o_ref[...] = _dot(h_ref[...], w_ref[...])


def _norm_mod_matmul(x, g, sc, sh, w, kind, tm, tn, tiles_per_seq):
    m, d = x.shape
    w_bf, layer = w
    n = w_bf.shape[2]
    mod = _mod_spec(kind, tm, d, tiles_per_seq, False)
    return pl.pallas_call(
        _nmm_body,
        grid=(m // tm, n // tn),
        in_specs=[pl.BlockSpec((tm, d), lambda i, j: (i, 0)),
                  pl.BlockSpec((1, d), lambda i, j: (0, 0)),
                  mod, mod,
                  pl.BlockSpec((None, d, tn), lambda i, j: (layer, 0, j))],
        out_specs=pl.BlockSpec((tm, tn), lambda i, j: (i, j)),
        out_shape=jax.ShapeDtypeStruct((m, n), F32),
        scratch_shapes=[pltpu.VMEM((tm, d), BF16)],
        compiler_params=_params(("arbitrary", "arbitrary")),
    )(x, g.reshape(1, d), sc, sh, w_bf)


def _ffn_in_body(x_ref, g_ref, sc_ref, sh_ref, w1_ref, w2_ref, o_ref, h_ref):
    @pl.when(pl.program_id(1) == 0)
    def _():
        h_ref[...] = _norm_mod(x_ref[...], g_ref[...], sc_ref[...], sh_ref[...]).astype(BF16)

    h = h_ref[...]
    o_ref[...] = (_silu(_dot(h, w1_ref[...])) * _dot(h, w2_ref[...])).astype(BF16)


def _norm_mod_ffn_in(x, g, sc, sh, w, kind, tm, tn, tiles_per_seq):
    m, d = x.shape
    w_bf, layer = w
    nj = D_FF // tn
    mod = _mod_spec(kind, tm, d, tiles_per_seq, False)
    return pl.pallas_call(
        _ffn_in_body,
        grid=(m // tm, nj),
        in_specs=[pl.BlockSpec((tm, d), lambda i, j: (i, 0)),
                  pl.BlockSpec((1, d), lambda i, j: (0, 0)),
                  mod, mod,
                  pl.BlockSpec((None, d, tn), lambda i, j: (layer, 0, j)),
                  pl.BlockSpec((None, d, tn), lambda i, j: (layer, 0, j + nj))],
        out_specs=pl.BlockSpec((tm, tn), lambda i, j: (i, j)),
        out_shape=jax.ShapeDtypeStruct((m, D_FF), BF16),
        scratch_shapes=[pltpu.VMEM((tm, d), BF16)],
        compiler_params=_params(("arbitrary", "arbitrary")),
    )(x, g.reshape(1, d), sc, sh, w_bf, w_bf)


def _mmres_body(a_ref, w_ref, x_ref, g_ref, o_ref):
    o_ref[...] = x_ref[...] + g_ref[...] * _dot(a_ref[...], w_ref[...])


def _matmul_residual(a_bf, w, x, gate, kind, tm, tn, tiles_per_seq):
    m, k = a_bf.shape
    w_bf, layer = w
    n = w_bf.shape[2]
    return pl.pallas_call(
        _mmres_body,
        grid=(m // tm, n // tn),
        in_specs=[pl.BlockSpec((tm, k), lambda i, j: (i, 0)),
                  pl.BlockSpec((None, k, tn), lambda i, j: (layer, 0, j)),
                  pl.BlockSpec((tm, tn), lambda i, j: (i, j)),
                  _mod_spec(kind, tm, tn, tiles_per_seq, True)],
        out_specs=pl.BlockSpec((tm, tn), lambda i, j: (i, j)),
        out_shape=jax.ShapeDtypeStruct((m, n), F32),
        compiler_params=_params(("arbitrary", "arbitrary")),
    )(a_bf, w_bf, x, gate)


def _merge_body(oa_ref, or_ref, os_ref, wb_ref, ga_ref, gr_ref, gs_ref, o_ref):
    acc = None
    for n, (o_r, g_r) in enumerate(((oa_ref, ga_ref), (or_ref, gr_ref), (os_ref, gs_ref))):
        term = _sigmoid(g_r[...]) * _dot(o_r[...].astype(BF16), wb_ref[n])
        acc = term if acc is None else acc + term
    o_ref[...] = acc.astype(BF16)


def _branch_merge(o_att, o_ret, o_s5, z, w, tm, tn):
    m = z.shape[0]
    wb_bf, layer = w
    gate0 = OFF_GATE // tn
    per = D_MODEL // tn
    o_spec = pl.BlockSpec((tm, BRANCH_W), lambda i, j: (i, 0))

    def gate_spec(n):
        return pl.BlockSpec((tm, tn), lambda i, j: (i, gate0 + n * per + j))

    return pl.pallas_call(
        _merge_body,
        grid=(m // tm, per),
        in_specs=[o_spec, o_spec, o_spec,
                  pl.BlockSpec((None, N_BRANCH, BRANCH_W, tn), lambda i, j: (layer, 0, 0, j)),
                  gate_spec(0), gate_spec(1), gate_spec(2)],
        out_specs=pl.BlockSpec((tm, tn), lambda i, j: (i, j)),
        out_shape=jax.ShapeDtypeStruct((m, D_MODEL), BF16),
        compiler_params=_params(("arbitrary", "arbitrary")),
    )(o_att, o_ret, o_s5, wb_bf, z, z, z)


def _fnorm_body(x_ref, g_ref, o_ref):
    x = x_ref[...]
    o_ref[...] = x * lax.rsqrt(jnp.mean(x * x, axis=-1, keepdims=True) + NORM_EPS) * g_ref[...]


def _final_norm(x, g, tm):
    m, d = x.shape
    return pl.pallas_call(
        _fnorm_body,
        grid=(m // tm,),
        in_specs=[pl.BlockSpec((tm, d), lambda i: (i, 0)), pl.BlockSpec((1, d), lambda i: (0, 0))],
        out_specs=pl.BlockSpec((tm, d), lambda i: (i, 0)),
        out_shape=jax.ShapeDtypeStruct((m, d), F32),
        compiler_params=_params(("arbitrary",)),
    )(x, g.reshape(1, d))


def _bucket_np(dist):
    n = np.maximum(dist, 0)
    max_exact = REL_BUCKETS // 2
    nf = np.maximum(n, 1).astype(np.float32)
    large = max_exact + (np.log(nf / max_exact) / math.log(REL_MAX_DIST / max_exact)
                         * (REL_BUCKETS - max_exact)).astype(np.int32)
    large = np.minimum(large, REL_BUCKETS - 1)
    return np.where(n < max_exact, n, large).astype(np.int32)


def _bias_p_body(rb_ref, bk_ref, o_ref):
    h = pl.program_id(0)
    blk = MOBA_BLOCK
    far = rb_ref[REL_BUCKETS - 1, h]
    for s in range(2):
        bk = bk_ref[s]
        acc = jnp.zeros((blk, blk), F32)
        for b in range(REL_BUCKETS):
            acc = jnp.where(bk == b, rb_ref[b, h], acc)
        acc = (acc - far) * LOG2E
        if s == 0:
            rk = lax.broadcasted_iota(jnp.int32, (blk, blk), 0)
            rq = lax.broadcasted_iota(jnp.int32, (blk, blk), 1)
            acc = jnp.where(rk <= rq, acc, 2 * NEG)
        o_ref[s] = acc


def _bias_tables_prompt(rel_bias):
    blk = MOBA_BLOCK
    rk = np.arange(blk)[:, None]
    rq = np.arange(blk)[None, :]
    buckets = np.stack([_bucket_np(rq - rk), _bucket_np(blk + rq - rk)])
    assert int(_bucket_np(np.array([blk + 1]))[0]) == REL_BUCKETS - 1
    return pl.pallas_call(
        _bias_p_body,
        grid=(ATT_HEADS,),
        in_specs=[pl.BlockSpec(memory_space=pltpu.SMEM),
                  pl.BlockSpec((2, blk, blk), lambda h: (0, 0, 0))],
        out_specs=pl.BlockSpec((None, 2, blk, blk), lambda h: (h, 0, 0, 0)),
        out_shape=jax.ShapeDtypeStruct((ATT_HEADS, 2, blk, blk), F32),
        compiler_params=_params(("arbitrary",)),
    )(rel_bias, jnp.asarray(buckets))


def _topk_select(s, allowed, idx, n_cand):
    s = jnp.where(allowed, s, -jnp.inf)
    rank = jnp.zeros(s.shape, jnp.int32)
    for m in range(n_cand):
        sm = s[m:m + 1]
        beats = (sm > s) | ((sm == s) & (idx > m))
        rank = rank + beats.astype(jnp.int32)
    return jnp.where(allowed & (rank < MOBA_TOPK), 1.0, 0.0)


def _moba_p_body(q_ref, k_ref, v_ref, tb_ref, o_ref, kbf_ref, vt_ref, kmean_ref, sel_ref, *, nb):
    i = pl.program_id(2)
    blk = MOBA_BLOCK
    dh = ATT_DH

    @pl.when(i == 0)
    def _():
        kf = k_ref[...]
        for hh in range(2):
            kbf_ref[hh] = kf[:, hh * dh:(hh + 1) * dh].astype(BF16)
        vt_ref[...] = v_ref[...].T.astype(BF16)
        kmean_ref[...] = jnp.mean(kf.reshape(nb, blk, LANES), axis=1)

    q = q_ref[...]
    grp = MOBA_KEY_GROUP
    qs_all, prev_sel, carry0 = [], [], []
    for hh in range(2):
        cs = slice(hh * dh, (hh + 1) * dh)
        qh = q[:, cs]
        s = _dot_nt(kmean_ref[:, cs], qh, precision=HIGHEST)
        nidx = lax.broadcasted_iota(jnp.int32, s.shape, 0)
        sel = _topk_select(s, nidx < i, nidx, nb)
        sel_ref[hh] = jnp.where(nidx < i - 1, sel, 0.0)
        prev_sel.append(jnp.sum(jnp.where(nidx == i - 1, sel, 0.0), axis=0, keepdims=True) > 0.5)
        qs_all.append((qh * (dh ** -0.5 * LOG2E)).astype(BF16))
        carry0 += [jnp.full((1, blk), NEG, F32), jnp.zeros((1, blk), F32), jnp.zeros((dh, blk), F32)]

    def softmax_step(carry, lgs, sels, v_ts):
        m_new = []
        for hh in range(2):
            mh = carry[3 * hh]
            for lg, sl in zip(lgs[hh], sels[hh]):
                cm = jnp.max(lg, axis=0, keepdims=True)
                mh = jnp.maximum(mh, cm if sl is None else jnp.where(sl, cm, NEG))
            m_new.append(mh)
        scale, l_new, p_all = [], [], []
        for hh in range(2):
            a = jnp.exp2(carry[3 * hh] - m_new[hh])
            l = a * carry[3 * hh + 1]
            ps = []
            for lg, sl in zip(lgs[hh], sels[hh]):
                p = jnp.exp2(lg - (m_new[hh] if sl is None else jnp.where(sl, m_new[hh], -NEG)))
                l = l + jnp.sum(p, axis=0, keepdims=True)
                ps.append(p.astype(BF16))
            scale.append(a)
            l_new.append(l)
            p_all.append(jnp.concatenate(ps, axis=0))
        out = []
        for hh in range(2):
            out += [m_new[hh], l_new[hh], scale[hh] * carry[3 * hh + 2] + _dot(v_ts[hh], p_all[hh])]
        return tuple(out)

    def body(j, carry):
        st = pl.multiple_of(j * (grp * blk), grp * blk)
        lg_all = [_dot_nt(kbf_ref[hh, pl.ds(st, grp * blk), :], qs_all[hh]) for hh in range(2)]
        lgs = [[lg_all[hh][g * blk:(g + 1) * blk] for g in range(grp)] for hh in range(2)]
        sels = [[sel_ref[hh, pl.ds(j * grp + g, 1), :] > 0.5 for g in range(grp)] for hh in range(2)]
        v_ts = [vt_ref[hh * dh:(hh + 1) * dh, pl.ds(st, grp * blk)] for hh in range(2)]
        return softmax_step(carry, lgs, sels, v_ts)

    fin = lax.fori_loop(0, (i + grp - 2) // grp, body, tuple(carry0))

    own = pl.multiple_of(i * blk, blk)
    prev = pl.multiple_of(jnp.maximum(i - 1, 0) * blk, blk)
    lgs, v_ts = [], []
    for hh in range(2):
        rows = slice(hh * dh, (hh + 1) * dh)
        lgs.append([_dot_nt(kbf_ref[hh, pl.ds(prev, blk), :], qs_all[hh]) + tb_ref[hh, 1],
                    _dot_nt(kbf_ref[hh, pl.ds(own, blk), :], qs_all[hh]) + tb_ref[hh, 0]])
        v_ts.append(jnp.concatenate([vt_ref[rows, pl.ds(prev, blk)], vt_ref[rows, pl.ds(own, blk)]], axis=1))
    fin = softmax_step(fin, lgs, [[prev_sel[hh], None] for hh in range(2)], v_ts)
    outs = [(fin[3 * hh + 2] / fin[3 * hh + 1]).T for hh in range(2)]
    o_ref[...] = jnp.concatenate(outs, axis=1).astype(BF16)


def _moba_prompt(z, bias_tab, n_seq, t):
    blk = MOBA_BLOCK
    nb = t // blk
    assert nb % MOBA_KEY_GROUP == 0
    hp = ATT_HEADS // 2
    kc, vc = OFF_AK // LANES, OFF_AV // LANES
    return pl.pallas_call(
        functools.partial(_moba_p_body, nb=nb),
        grid=(n_seq, hp, nb),
        in_specs=[pl.BlockSpec((blk, LANES), lambda b, h, i: (b * nb + i, h)),
                  pl.BlockSpec((t, LANES), lambda b, h, i: (b, kc + h)),
                  pl.BlockSpec((t, LANES), lambda b, h, i: (b, vc + h)),
                  pl.BlockSpec((2, 2, blk, blk), lambda b, h, i: (h, 0, 0, 0))],
        out_specs=pl.BlockSpec((blk, LANES), lambda b, h, i: (b * nb + i, h)),
        out_shape=jax.ShapeDtypeStruct((n_seq * t, ATT_W), BF16),
        scratch_shapes=[pltpu.VMEM((2, t, ATT_DH), BF16), pltpu.VMEM((LANES, t), BF16),
                        pltpu.VMEM((nb, LANES), F32), pltpu.VMEM((2, nb, blk), F32)],
        compiler_params=_params(("arbitrary", "arbitrary", "arbitrary")),
    )(z, z, z, bias_tab)


PAGES_PER_STEP = 8
PAGES_PER_BLOCK = MOBA_BLOCK // PAGE_SIZE


def _moba_s_body(pt_ref, *refs, n_steps):
    del pt_ref
    kpages, vpages = refs[:PAGES_PER_STEP], refs[PAGES_PER_STEP:2 * PAGES_PER_STEP]
    q_ref, kn_ref, vn_ref, rbt_ref, bk_ref, o_ref, m_s, l_s, s_s, acc_s = refs[2 * PAGES_PER_STEP:]
    g = pl.program_id(1)
    h, dh, w = ATT_HEADS, ATT_DH, ATT_W
    n_pages = n_steps * PAGES_PER_STEP
    row = lax.broadcasted_iota(jnp.int32, (h, w), 0)
    lane = lax.broadcasted_iota(jnp.int32, (h, w), 1)
    own_head = (lane // dh) == row
    qbd = jnp.where(own_head, q_ref[...] * (dh ** -0.5 * LOG2E), 0.0)
    q_hi = qbd.astype(BF16)
    q_lo = (qbd - q_hi.astype(F32)).astype(BF16)
    q2 = jnp.concatenate([q_hi, q_lo], axis=0)
    far_bias = rbt_ref[:, REL_BUCKETS - 1:REL_BUCKETS] * LOG2E
    bk = bk_ref[...]
    near_bias = jnp.zeros((h, PAGE_SIZE), F32)
    for b in range(REL_BUCKETS):
        near_bias = jnp.where(bk == b, rbt_ref[:, b:b + 1], near_bias)
    is_last = (g == n_steps - 1).astype(F32)
    last_bias = far_bias + is_last * (near_bias * LOG2E - far_bias)

    raws = []
    for j in range(PAGES_PER_STEP):
        raw2 = _dot(q2, kpages[j][...].reshape(w, PAGE_SIZE).astype(BF16))
        raws.append(raw2[:h] + raw2[h:])
    stats = []
    for j, raw in enumerate(raws):
        lg = raw + (last_bias if j == PAGES_PER_STEP - 1 else far_bias)
        mb = jnp.max(lg, axis=1, keepdims=True)
        p = jnp.exp2(lg - mb)
        stats.append((mb, jnp.sum(p, axis=1, keepdims=True), jnp.sum(raw, axis=1, keepdims=True), p.astype(BF16)))
    for j, (mb, lb, sb, p) in enumerate(stats):
        idx = g * PAGES_PER_STEP + j
        m_s[idx] = mb
        l_s[idx] = lb
        s_s[idx] = sb
        acc_s[idx] = _dot_nt(p, vpages[j][...].reshape(w, PAGE_SIZE).astype(BF16))

    @pl.when(g == n_steps - 1)
    def _():
        nblk = n_pages // PAGES_PER_BLOCK
        s = jnp.sum(s_s[...].reshape(nblk, PAGES_PER_BLOCK, h, 1), axis=1)
        nidx = lax.broadcasted_iota(jnp.int32, s.shape, 0)
        sel = _topk_select(s, nidx >= 0, nidx, nblk)
        selp = jnp.broadcast_to(sel[:, None], (nblk, PAGES_PER_BLOCK, h, 1)).reshape(n_pages, h, 1) > 0.5
        ln = jnp.sum(qbd * kn_ref[...], axis=1, keepdims=True) + rbt_ref[:, 0:1] * LOG2E
        mm = jnp.where(selp, m_s[...], NEG)
        mx = jnp.maximum(jnp.max(mm, axis=0), ln)
        wgt = jnp.where(selp, jnp.exp2(mm - mx), 0.0)
        wn = jnp.exp2(ln - mx)
        den = jnp.sum(wgt * l_s[...], axis=0) + wn
        tot = jnp.sum(wgt * acc_s[...], axis=0) + wn * vn_ref[...]
        o_ref[...] = jnp.sum(jnp.where(own_head, tot / den, 0.0), axis=0, keepdims=True)


def _moba_sample(z_s, cache_k, cache_v, layer, page_table, rel_bias_t):
    n_seq, n_pages = page_table.shape
    assert n_pages % PAGES_PER_STEP == 0 and PAGES_PER_STEP % PAGES_PER_BLOCK == 0
    n_steps = n_pages // PAGES_PER_STEP
    assert int(_bucket_np(np.array([PAGE_SIZE + 1]))[0]) == REL_BUCKETS - 1
    near_buckets = _bucket_np(PAGE_SIZE - np.arange(PAGE_SIZE)).reshape(1, PAGE_SIZE)
    rows = lambda off: z_s[:, off:off + ATT_W].reshape(n_seq, 1, ATT_W)

    def page_spec(j):
        return pl.BlockSpec((None, None, ATT_HEADS, ATT_DH, PAGE_SIZE),
                            lambda s, g, pt: (layer, pt[s, g * PAGES_PER_STEP + j], 0, 0, 0))

    row_spec = pl.BlockSpec((None, 1, ATT_W), lambda s, g, pt: (s, 0, 0))
    stat = pltpu.VMEM((n_pages, ATT_HEADS, 1), F32)
    grid_spec = pltpu.PrefetchScalarGridSpec(
        num_scalar_prefetch=1,
        grid=(n_seq, n_steps),
        in_specs=[page_spec(j) for j in range(PAGES_PER_STEP)] * 2 + [
            row_spec, row_spec, row_spec,
            pl.BlockSpec((ATT_HEADS, REL_BUCKETS), lambda s, g, pt: (0, 0)),
            pl.BlockSpec((1, PAGE_SIZE), lambda s, g, pt: (0, 0))],
        out_specs=row_spec,
        scratch_shapes=[stat, stat, stat, pltpu.VMEM((n_pages, ATT_HEADS, ATT_W), F32)],
    )
    out = pl.pallas_call(
        functools.partial(_moba_s_body, n_steps=n_steps),
        grid_spec=grid_spec,
        out_shape=jax.ShapeDtypeStruct((n_seq, 1, ATT_W), F32),
        compiler_params=_params(("arbitrary", "arbitrary")),
    )(page_table, *([cache_k] * PAGES_PER_STEP), *([cache_v] * PAGES_PER_STEP),
      rows(OFF_AQ), rows(OFF_AK), rows(OFF_AV), rel_bias_t, jnp.asarray(near_buckets))
    return out.reshape(n_seq, ATT_W)


def _rope_body(inv_ref, sgn_ref, cos_ref, sin_ref, *, pos0, step, rows):
    r = lax.broadcasted_iota(jnp.int32, (rows, LANES), 0) + pl.program_id(0) * rows
    ang = (pos0 + step * r).astype(F32) * inv_ref[...]
    cos_ref[...] = jnp.cos(ang)
    sin_ref[...] = jnp.sin(ang) * sgn_ref[...]


def _rope_tables(n_rows, pos0, step):
    half = RET_DK // 2
    inv = 1.0 / (ROPE_BASE ** jnp.linspace(0.0, 1.0, half))
    inv_row = jnp.tile(inv, LANES // half).reshape(1, LANES).astype(F32)
    sgn = np.where((np.arange(LANES) % RET_DK) < half, -1.0, 1.0).astype(np.float32).reshape(1, LANES)
    rows = min(n_rows, 512)
    spec = pl.BlockSpec((rows, LANES), lambda i: (i, 0))
    cst = pl.BlockSpec((1, LANES), lambda i: (0, 0))
    return pl.pallas_call(
        functools.partial(_rope_body, pos0=pos0, step=step, rows=rows),
        grid=(n_rows // rows,),
        in_specs=[cst, cst],
        out_specs=[spec, spec],
        out_shape=[jax.ShapeDtypeStruct((n_rows, LANES), F32)] * 2,
        compiler_params=_params(("arbitrary",)),
    )(inv_row, jnp.asarray(sgn))


def _rotary128(x, cos, sin_signed):
    half = RET_DK // 2
    lane = lax.broadcasted_iota(jnp.int32, x.shape, 1)
    partner = jnp.where((lane % RET_DK) < half,
                        pltpu.roll(x, LANES - half, 1), pltpu.roll(x, half, 1))
    return x * cos + partner * sin_signed


def _groupnorm_gate(o, g):
    mu = jnp.mean(o, axis=-1, keepdims=True)
    var = jnp.mean((o - mu) ** 2, axis=-1, keepdims=True)
    return _silu(g) * ((o - mu) * lax.rsqrt(var + GN_EPS))


def _ret_log_decay():
    return jnp.log(1.0 - 2.0 ** (-5.0 - jnp.arange(RET_HEADS, dtype=F32)))


def _ret_p_body(lg_ref, q_ref, k_ref, v_ref, g_ref, cos_ref, sin_ref, o_ref, st_ref, *, chunk):
    hp = pl.program_id(1)
    c = pl.program_id(2)

    @pl.when(c == 0)
    def _():
        st_ref[...] = jnp.zeros(st_ref.shape, F32)

    cos, sin = cos_ref[...], sin_ref[...]
    q = _rotary128(q_ref[...], cos, sin)
    k = _rotary128(k_ref[...], cos, sin) * RET_DK ** -0.5
    ii = lax.broadcasted_iota(jnp.int32, (chunk, chunk), 0)
    jj = lax.broadcasted_iota(jnp.int32, (chunk, chunk), 1)
    diff = (ii - jj).astype(F32)
    ri = lax.broadcasted_iota(jnp.int32, (chunk, 1), 0).astype(F32)
    for hh in range(2):
        lgh = lg_ref[hp * 2 + hh]
        dmask = jnp.where(diff >= 0, jnp.exp(lgh * jnp.maximum(diff, 0.0)), 0.0)
        qh = q[:, hh * RET_DK:(hh + 1) * RET_DK].astype(BF16)
        kh = k[:, hh * RET_DK:(hh + 1) * RET_DK]
        vh = v_ref[:, hh * RET_DV:(hh + 1) * RET_DV].astype(BF16)
        s0 = st_ref[hh]
        a = _dot_nt(qh, kh.astype(BF16)) * dmask
        inner = _dot(a.astype(BF16), vh)
        cross = _dot(qh, s0.astype(BF16)) * jnp.exp(lgh * (ri + 1.0))
        kdec = (kh * jnp.exp(lgh * (chunk - 1.0 - ri))).astype(BF16)
        st_ref[hh] = jnp.exp(lgh * chunk + jnp.zeros((1, 1), F32)) * s0 + _dot_tn(kdec, vh)
        gh = g_ref[:, hh * RET_DV:(hh + 1) * RET_DV]
        o_ref[:, hh * RET_DV:(hh + 1) * RET_DV] = _groupnorm_gate(inner + cross, gh).astype(BF16)


def _ret_prompt(z, cos_tab, sin_tab, log_decay, n_seq, t):
    chunk = RET_CHUNK
    nc = t // chunk
    hp = RET_HEADS // 2
    qc, kc = OFF_RQ // LANES, OFF_RK // LANES
    vc, gc = OFF_RV // (2 * RET_DV), OFF_RG // (2 * RET_DV)
    return pl.pallas_call(
        functools.partial(_ret_p_body, chunk=chunk),
        grid=(n_seq, hp, nc),
        in_specs=[pl.BlockSpec(memory_space=pltpu.SMEM),
                  pl.BlockSpec((chunk, LANES), lambda b, h, c: (b * nc + c, qc + h)),
                  pl.BlockSpec((chunk, LANES), lambda b, h, c: (b * nc + c, kc + h)),
                  pl.BlockSpec((chunk, 2 * RET_DV), lambda b, h, c: (b * nc + c, vc + h)),
                  pl.BlockSpec((chunk, 2 * RET_DV), lambda b, h, c: (b * nc + c, gc + h)),
                  pl.BlockSpec((chunk, LANES), lambda b, h, c: (c, 0)),
                  pl.BlockSpec((chunk, LANES), lambda b, h, c: (c, 0))],
        out_specs=[pl.BlockSpec((chunk, 2 * RET_DV), lambda b, h, c: (b * nc + c, h)),
                   pl.BlockSpec((None, 2, RET_DK, RET_DV), lambda b, h, c: (b, h, 0, 0))],
        out_shape=[jax.ShapeDtypeStruct((n_seq * t, RET_V_W), BF16),
                   jax.ShapeDtypeStruct((n_seq, RET_HEADS, RET_DK, RET_DV), F32)],
        compiler_params=_params(("arbitrary", "arbitrary", "arbitrary")),
    )(log_decay, z, z, z, z, cos_tab, sin_tab)


def _ret_s_prep_body(q_ref, k_ref, cos_ref, sin_ref, qo_ref, ko_ref):
    cos, sin = cos_ref[0:1, :], sin_ref[0:1, :]
    for j in range(RET_QK_W // LANES):
        sl = slice(j * LANES, (j + 1) * LANES)
        qo_ref[:, sl] = _rotary128(q_ref[:, sl], cos, sin)
        ko_ref[:, sl] = _rotary128(k_ref[:, sl], cos, sin) * RET_DK ** -0.5


def _ret_s_body(lg_ref, q_ref, k_ref, v_ref, g_ref, s0_ref, o_ref, sn_ref):
    gam = jnp.exp(lg_ref[pl.program_id(1)] + jnp.zeros((1, 1, LANES), F32))
    q, k, v, s0 = q_ref[...], k_ref[...], v_ref[...], s0_ref[...]
    o = jnp.sum(q * s0, axis=1, keepdims=True) * gam + jnp.sum(q * k, axis=1, keepdims=True) * v
    sn_ref[...] = gam * s0 + k * v
    o_ref[...] = _groupnorm_gate(o, g_ref[...])


def _ret_sample(z_s, state, cos_tab, sin_tab, log_decay):
    n_seq = z_s.shape[0]
    tab = pl.BlockSpec((SUBLANES, LANES), lambda i: (0, 0))
    q_rot, k_rot = pl.pallas_call(
        _ret_s_prep_body,
        grid=(1,),
        in_specs=[pl.BlockSpec((n_seq, RET_QK_W), lambda i: (0, OFF_RQ // RET_QK_W)),
                  pl.BlockSpec((n_seq, RET_QK_W), lambda i: (0, OFF_RK // RET_QK_W)), tab, tab],
        out_specs=[pl.BlockSpec((n_seq, RET_QK_W), lambda i: (0, 0))] * 2,
        out_shape=[jax.ShapeDtypeStruct((n_seq, RET_QK_W), F32)] * 2,
        compiler_params=_params(("arbitrary",)),
    )(z_s, z_s, cos_tab, sin_tab)
    col = lambda a: a.reshape(n_seq, RET_HEADS, RET_DK, 1)
    row = lambda a: a.reshape(n_seq, RET_HEADS, 1, RET_DV)
    bt = 16
    col_spec = pl.BlockSpec((bt, None, RET_DK, 1), lambda i, h: (i, h, 0, 0))
    row_spec = pl.BlockSpec((bt, None, 1, RET_DV), lambda i, h: (i, h, 0, 0))
    st_spec = pl.BlockSpec((bt, None, RET_DK, RET_DV), lambda i, h: (i, h, 0, 0))
    state_all, layer = state
    st_in_spec = pl.BlockSpec((None, bt, None, RET_DK, RET_DV), lambda i, h: (layer, i, h, 0, 0))
    o, s_new = pl.pallas_call(
        _ret_s_body,
        grid=(n_seq // bt, RET_HEADS),
        in_specs=[pl.BlockSpec(memory_space=pltpu.SMEM), col_spec, col_spec, row_spec, row_spec, st_in_spec],
        out_specs=[row_spec, st_spec],
        out_shape=[jax.ShapeDtypeStruct((n_seq, RET_HEADS, 1, RET_DV), F32),
                   jax.ShapeDtypeStruct((n_seq, RET_HEADS, RET_DK, RET_DV), F32)],
        compiler_params=_params(("arbitrary", "arbitrary")),
    )(log_decay, col(q_rot), col(k_rot), row(z_s[:, OFF_RV:OFF_RV + RET_V_W]),
      row(z_s[:, OFF_RG:OFF_RG + RET_V_W]), state_all)
    return o.reshape(n_seq, RET_V_W), s_new


def _s5_prep_body(are_ref, aim_ref, ldt_ref, bre_ref, bim_ref, pwr_ref, pwi_ref, bbr_ref, bbi_ref):
    ar, ai = are_ref[...], aim_ref[...]
    dt = jnp.exp(ldt_ref[...])
    kk = (lax.broadcasted_iota(jnp.int32, pwr_ref.shape, 1) + 1).astype(F32)
    mag = jnp.exp(ar * dt * kk)
    ang = ai * dt * kk
    pwr_ref[...] = mag * jnp.cos(ang)
    pwi_ref[...] = mag * jnp.sin(ang)
    mag1 = jnp.exp(ar * dt)
    nr = mag1 * jnp.cos(ai * dt) - 1.0
    ni = mag1 * jnp.sin(ai * dt)
    den = ar * ar + ai * ai
    cr = (nr * ar + ni * ai) / den
    ci = (ni * ar - nr * ai) / den
    bre, bim = bre_ref[...], bim_ref[...]
    bbr_ref[...] = cr * bre - ci * bim
    bbi_ref[...] = cr * bim + ci * bre


def _s5_prepare(a_re, a_im, log_dt, b_re, b_im, c_re, c_im, d):
    g, p, c = S5_GROUPS, S5_STATE, S5_GROUP
    g3 = lambda a: a.reshape(g, 1, p)
    ldt = jnp.broadcast_to(log_dt.reshape(g, 1, 1), (g, 1, p))
    pwr, pwi, bbr, bbi = pl.pallas_call(
        _s5_prep_body,
        out_shape=[jax.ShapeDtypeStruct((g, SUBLANES, p), F32)] * 2
        + [jax.ShapeDtypeStruct((g, c, p), F32)] * 2,
    )(g3(a_re), g3(a_im), ldt, b_re.transpose(0, 2, 1), b_im.transpose(0, 2, 1))
    eye = jnp.eye(S5_GT, dtype=F32)
    nt = S5_NT

    def in_tile(bb):
        return jnp.einsum("tgcp,gh->tgchp", bb.reshape(nt, S5_GT, c, p), eye).reshape(nt, S5_GT * c, S5_GT * p)

    def out_tile(cc):
        return jnp.einsum("tgcp,gh->tgphc", cc.reshape(nt, S5_GT, c, p), eye).reshape(nt, S5_GT * p, S5_GT * c)

    b_tile = jnp.concatenate([in_tile(bbr), in_tile(bbi)], axis=2).astype(BF16)
    c_tile = jnp.concatenate([out_tile(c_re), out_tile(-c_im)], axis=1).astype(BF16)
    d_tile = d.reshape(nt, 1, S5_GT * c)

    def pw_tile(pw):
        return pw.reshape(nt, S5_GT, SUBLANES, p).transpose(0, 2, 1, 3).reshape(nt, SUBLANES, S5_GT * p)

    pr, pi = pw_tile(pwr), pw_tile(pwi)
    rows = jnp.arange(SUBLANES)[None, :, None]
    slabs = []
    for shift in (1, 2, 4):
        keep = rows >= shift
        slabs += [jnp.where(keep, pr[:, shift - 1:shift, :], 0.0), jnp.where(keep, pi[:, shift - 1:shift, :], 0.0)]
    scan_c = jnp.stack(slabs + [pr, pi], axis=1)
    lam1 = jnp.stack([pr[:, 0:1, :], pi[:, 0:1, :]], axis=1)
    return b_tile, c_tile, d_tile, scan_c, lam1


def _gelu_tanh(y):
    return 0.5 * y * (1.0 + jnp.tanh(math.sqrt(2.0 / math.pi) * (y + 0.044715 * (y * y * y))))


def _s5_p_body(u_ref, bt_ref, ct_ref, d_ref, sc_ref, z_ref, st_ref, x_ref, carry_ref, *, tc):
    ns = S5_TILE_STATES

    @pl.when(pl.program_id(2) == 0)
    def _():
        carry_ref[...] = jnp.zeros(carry_ref.shape, F32)

    u = u_ref[...]
    x_ref[...] = _dot(u.astype(BF16), bt_ref[...])

    def tile(t, carry):
        cr, ci = carry
        st = pl.multiple_of(t * SUBLANES, SUBLANES)
        xr = x_ref[pl.ds(st, SUBLANES), 0:ns]
        xi = x_ref[pl.ds(st, SUBLANES), ns:2 * ns]
        for s, shift in enumerate((1, 2, 4)):
            ar, ai = sc_ref[2 * s], sc_ref[2 * s + 1]
            sr, si = pltpu.roll(xr, shift, 0), pltpu.roll(xi, shift, 0)
            xr, xi = xr + ar * sr - ai * si, xi + ar * si + ai * sr
        pr, pi = sc_ref[6], sc_ref[7]
        xr, xi = xr + pr * cr - pi * ci, xi + pr * ci + pi * cr
        x_ref[pl.ds(st, SUBLANES), 0:ns] = xr
        x_ref[pl.ds(st, SUBLANES), ns:2 * ns] = xi
        return xr[SUBLANES - 1:SUBLANES, :], xi[SUBLANES - 1:SUBLANES, :]

    cr, ci = lax.fori_loop(0, tc // SUBLANES, tile, (carry_ref[:, 0:ns], carry_ref[:, ns:2 * ns]))
    last = jnp.concatenate([cr, ci], axis=1)
    carry_ref[...] = last
    st_ref[...] = last
    y = _dot(x_ref[...].astype(BF16), ct_ref[...]) + d_ref[...] * u
    z_ref[...] = _gelu_tanh(y)


def _s5_prompt(z, tiles, n_seq, t):
    b_tile, c_tile, d_tile, scan_c, _ = tiles
    tc = 1024
    nc = t // tc
    uc = OFF_SU // LANES
    ns2 = 2 * S5_TILE_STATES
    zs, st = pl.pallas_call(
        functools.partial(_s5_p_body, tc=tc),
        grid=(n_seq, S5_NT, nc),
        in_specs=[pl.BlockSpec((tc, LANES), lambda b, g, c: (b * nc + c, uc + g)),
                  pl.BlockSpec((None, LANES, ns2), lambda b, g, c: (g, 0, 0)),
                  pl.BlockSpec((None, ns2, LANES), lambda b, g, c: (g, 0, 0)),
                  pl.BlockSpec((None, 1, LANES), lambda b, g, c: (g, 0, 0)),
                  pl.BlockSpec((None, 8, SUBLANES, S5_TILE_STATES), lambda b, g, c: (g, 0, 0, 0))],
        out_specs=[pl.BlockSpec((tc, LANES), lambda b, g, c: (b * nc + c, g)),
                   pl.BlockSpec((None, None, 1, ns2), lambda b, g, c: (b, g, 0, 0))],
        out_shape=[jax.ShapeDtypeStruct((n_seq * t, S5_W), F32),
                   jax.ShapeDtypeStruct((n_seq, S5_NT, 1, ns2), F32)],
        scratch_shapes=[pltpu.VMEM((tc, ns2), F32), pltpu.VMEM((1, ns2), F32)],
        compiler_params=_params(("arbitrary", "arbitrary", "arbitrary")),
    )(z, b_tile, c_tile, d_tile, scan_c)
    s_re = st[:, :, 0, :S5_TILE_STATES].reshape(n_seq, S5_GROUPS, S5_STATE)
    s_im = st[:, :, 0, S5_TILE_STATES:].reshape(n_seq, S5_GROUPS, S5_STATE)
    return zs, s_re, s_im


def _s5_s_body(u_ref, bt_ref, ct_ref, d_ref, l1_ref, x0_ref, z_ref, xn_ref):
    ns = S5_TILE_STATES
    u = u_ref[...]
    bu = _dot(u.astype(BF16), bt_ref[...])
    lr, li = l1_ref[0], l1_ref[1]
    x0r, x0i = x0_ref[:, 0:ns], x0_ref[:, ns:2 * ns]
    x = jnp.concatenate([bu[:, 0:ns] + lr * x0r - li * x0i, bu[:, ns:2 * ns] + lr * x0i + li * x0r], axis=1)
    xn_ref[...] = x
    z_ref[...] = _gelu_tanh(_dot(x.astype(BF16), ct_ref[...]) + d_ref[...] * u)


def _s5_sample(z_s, tiles, x0_re, x0_im):
    b_tile, c_tile, d_tile, _, lam1 = tiles
    n_seq = z_s.shape[0]
    uc = OFF_SU // LANES
    ns2 = 2 * S5_TILE_STATES

    def to_tiles(a):
        return a.reshape(n_seq, S5_NT, S5_TILE_STATES).transpose(1, 0, 2)

    x0 = jnp.concatenate([to_tiles(x0_re), to_tiles(x0_im)], axis=2)
    zs, xn = pl.pallas_call(
        _s5_s_body,
        grid=(S5_NT,),
        in_specs=[pl.BlockSpec((n_seq, LANES), lambda g: (0, uc + g)),
                  pl.BlockSpec((None, LANES, ns2), lambda g: (g, 0, 0)),
                  pl.BlockSpec((None, ns2, LANES), lambda g: (g, 0, 0)),
                  pl.BlockSpec((None, 1, LANES), lambda g: (g, 0, 0)),
                  pl.BlockSpec((None, 2, 1, S5_TILE_STATES), lambda g: (g, 0, 0, 0)),
                  pl.BlockSpec((None, n_seq, ns2), lambda g: (g, 0, 0))],
        out_specs=[pl.BlockSpec((n_seq, LANES), lambda g: (0, g)),
                   pl.BlockSpec((None, n_seq, ns2), lambda g: (g, 0, 0))],
        out_shape=[jax.ShapeDtypeStruct((n_seq, S5_W), F32),
                   jax.ShapeDtypeStruct((S5_NT, n_seq, ns2), F32)],
        compiler_params=_params(("arbitrary",)),
    )(z_s, b_tile, c_tile, d_tile, lam1, x0)

    def from_tiles(a):
        return a.transpose(1, 0, 2).reshape(n_seq, S5_GROUPS, S5_STATE)

    return zs, from_tiles(xn[:, :, :S5_TILE_STATES]), from_tiles(xn[:, :, S5_TILE_STATES:])


def _glu_body(zf_ref, zc_ref, w_ref, o_ref):
    o_ref[...] = (zc_ref[...] * _sigmoid(_dot(zf_ref[...].astype(BF16), w_ref[...]))).astype(BF16)


def _half_glu(zs, w_glu, tm, tn):
    m, w = zs.shape
    w_bf, layer = w_glu
    return pl.pallas_call(
        _glu_body,
        grid=(m // tm, w // tn),
        in_specs=[pl.BlockSpec((tm, w), lambda i, j: (i, 0)),
                  pl.BlockSpec((tm, tn), lambda i, j: (i, j)),
                  pl.BlockSpec((None, w, tn), lambda i, j: (layer, 0, j))],
        out_specs=pl.BlockSpec((tm, tn), lambda i, j: (i, j)),
        out_shape=jax.ShapeDtypeStruct((m, w), BF16),
        compiler_params=_params(("arbitrary", "arbitrary")),
    )(zs, zs, w_bf)


def _kv_t_body(z_ref, o_ref):
    zt = z_ref[...].T
    for hh in range(2):
        o_ref[hh] = zt[hh * ATT_DH:(hh + 1) * ATT_DH, :]


def _kv_transposed(z, off, n_seq, t):
    tt = min(t, 2048)
    nt = t // tt
    col0 = off // LANES
    return pl.pallas_call(
        _kv_t_body,
        grid=(n_seq, ATT_HEADS // 2, nt),
        in_specs=[pl.BlockSpec((tt, LANES), lambda b, h, i: (b * nt + i, col0 + h))],
        out_specs=pl.BlockSpec((None, 2, ATT_DH, tt), lambda b, h, i: (b, h, 0, i)),
        out_shape=jax.ShapeDtypeStruct((n_seq, ATT_HEADS, ATT_DH, t), F32),
        compiler_params=_params(("arbitrary", "arbitrary", "arbitrary")),
    )(z)


def _layer(x, mods, lw, mix, kind, tm, tiles_per_seq):
    norm1, norm2, w_in, w_branch, w_out, w_ffn_in, w_ffn_out = lw
    sh1, sc1, g1, sh2, sc2, g2 = mods
    z = _norm_mod_matmul(x, norm1, sc1, sh1, w_in, kind, tm, 1024, tiles_per_seq)
    (o_att, o_ret, o_s5), state = mix(z)
    merged = _branch_merge(o_att, o_ret, o_s5, z, w_branch, tm, 512)
    x = _matmul_residual(merged, w_out, x, g1, kind, tm, 1024, tiles_per_seq)
    act = _norm_mod_ffn_in(x, norm2, sc2, sh2, w_ffn_in, kind, tm, 512, tiles_per_seq)
    tm_out = min(tm, 512)
    x = _matmul_residual(act, w_ffn_out, x, g2, kind, tm_out, 1024, tiles_per_seq * (tm // tm_out))
    return x, z, state


def kernel(x_prompt, x_sample, c_prompt, c_sample, cache_k, cache_v, page_table, state_ret, state_s5_re, state_s5_im, rel_bias, norm1_g, norm2_g, w_ada, b_ada, w_in, s5_a_re, s5_a_im, s5_log_dt, s5_b_re, s5_b_im, s5_c_re, s5_c_im, s5_d, w_glu, w_branch, w_out, w_ffn_in, w_ffn_out, final_g):
    n_seq, t, d = x_prompt.shape
    n_dec = x_sample.shape[0]
    depth = w_in.shape[0]
    past = page_table.shape[1] * PAGE_SIZE
    xp = x_prompt.reshape(n_seq * t, d)
    xs = x_sample.reshape(n_dec, d)
    tm_p = 1024
    tiles_per_seq = t // tm_p

    bias_p = _bias_tables_prompt(rel_bias)
    rel_bias_t = rel_bias.T
    cache_kt = cache_k.transpose(0, 1, 3, 4, 2)
    cache_vt = cache_v.transpose(0, 1, 3, 4, 2)
    cos_p, sin_p = _rope_tables(t, 0, 1)
    cos_s, sin_s = _rope_tables(SUBLANES, past, 0)
    log_decay = _ret_log_decay()
    pad = (-(n_seq + n_dec)) % SUBLANES
    c_all = jnp.concatenate([c_prompt, c_sample, jnp.zeros((pad, d), F32)], axis=0)

    w_in_bf, w_branch_bf, w_out_bf = w_in.astype(BF16), w_branch.astype(BF16), w_out.astype(BF16)
    w_ffn_in_bf, w_ffn_out_bf, w_glu_bf = w_ffn_in.astype(BF16), w_ffn_out.astype(BF16), w_glu.astype(BF16)

    new_p, new_s = [], []
    for l in range(depth):
        mod = _ada(c_all, w_ada, b_ada[l], l)
        mod_p = mod[:n_seq].reshape(n_seq, 6, 1, d)
        mod_s = mod[n_seq:n_seq + n_dec].reshape(n_dec, 6, d)
        mods_p = tuple(mod_p[:, i] for i in range(6))
        mods_s = tuple(mod_s[:, i] for i in range(6))
        lw = (norm1_g[l], norm2_g[l], (w_in_bf, l), (w_branch_bf, l), (w_out_bf, l),
              (w_ffn_in_bf, l), (w_ffn_out_bf, l))
        glu_w = (w_glu_bf, l)
        tiles = _s5_prepare(s5_a_re[l], s5_a_im[l], s5_log_dt[l], s5_b_re[l], s5_b_im[l],
                            s5_c_re[l], s5_c_im[l], s5_d[l])

        def mix_p(z):
            o_att = _moba_prompt(z, bias_p, n_seq, t)
            o_ret, s_ret = _ret_prompt(z, cos_p, sin_p, log_decay, n_seq, t)
            zs5, s_re, s_im = _s5_prompt(z, tiles, n_seq, t)
            return (o_att, o_ret, _half_glu(zs5, glu_w, tm_p, 512)), (s_ret, s_re, s_im)

        def mix_s(z, l=l):
            o_att = _moba_sample(z, cache_kt, cache_vt, l, page_table, rel_bias_t)
            o_ret, s_ret = _ret_sample(z, (state_ret, l), cos_s, sin_s, log_decay)
            zs5, s_re, s_im = _s5_sample(z, tiles, state_s5_re[l], state_s5_im[l])
            return (o_att, o_ret, _half_glu(zs5, glu_w, n_dec, 512)), (s_ret, s_re, s_im)

        xp, zp, st_p = _layer(xp, mods_p, lw, mix_p, "prompt", tm_p, tiles_per_seq)
        xs, zs, st_s = _layer(xs, mods_s, lw, mix_s, "sample", n_dec, 1)
        kv = lambda z, n, tt, off: z[:, off:off + ATT_W].reshape(n, tt, ATT_HEADS, ATT_DH)
        new_p.append((_kv_transposed(zp, OFF_AK, n_seq, t), _kv_transposed(zp, OFF_AV, n_seq, t)) + st_p)
        new_s.append((kv(zs, n_dec, 1, OFF_AK), kv(zs, n_dec, 1, OFF_AV)) + st_s)

    y_prompt = _final_norm(xp, final_g, tm_p).reshape(n_seq, t, d)
    y_sample = _final_norm(xs, final_g, n_dec).reshape(n_dec, 1, d)
    outs_p = [jnp.stack(a) for a in zip(*new_p)]
    outs_p[0], outs_p[1] = (a.transpose(0, 1, 4, 2, 3) for a in outs_p[:2])
    outs_s = [jnp.stack(a) for a in zip(*new_s)]
    return (y_prompt, y_sample, *outs_p, *outs_s)
```

```python
import functools
import math

import numpy as np
import jax
import jax.numpy as jnp
from jax import lax
from jax.experimental import pallas as pl
from jax.experimental.pallas import tpu as pltpu

F32 = jnp.float32
BF16 = jnp.bfloat16
HIGHEST = lax.Precision.HIGHEST

D_MODEL = 2048
PAGE_SIZE = 128
ATT_HEADS = 16
ATT_DH = 64
ATT_W = ATT_HEADS * ATT_DH
MOBA_BLOCK = 256
MOBA_TOPK = 3
REL_BUCKETS = 32
REL_MAX_DIST = 128
RET_HEADS = 8
RET_DK = 64
RET_DV = 128
RET_QK_W = RET_HEADS * RET_DK
RET_V_W = RET_HEADS * RET_DV
RET_CHUNK = 256
ROPE_BASE = 10000.0
S5_W = 1024
S5_GROUP = 16
S5_GROUPS = S5_W // S5_GROUP
S5_STATE = 64
N_BRANCH = 3
BRANCH_W = 1024
D_FF = ((8 * D_MODEL + 3 * 256 - 1) // (3 * 256)) * 256
NORM_EPS = 1e-6
GN_EPS = 1e-5
D_IN = 3 * ATT_W + 2 * RET_QK_W + 2 * RET_V_W + S5_W + N_BRANCH * D_MODEL

OFF_AQ, OFF_AK, OFF_AV = 0, ATT_W, 2 * ATT_W
OFF_RQ = 3 * ATT_W
OFF_RK = OFF_RQ + RET_QK_W
OFF_RV = OFF_RK + RET_QK_W
OFF_RG = OFF_RV + RET_V_W
OFF_SU = OFF_RG + RET_V_W
OFF_GATE = OFF_SU + S5_W

LANES = 128
SUBLANES = 8
VMEM_LIMIT = 56 * 1024 * 1024
NEG = -1e30
LOG2E = math.log2(math.e)
MOBA_KEY_GROUP = 4
S5_GT = 8
S5_TILE_STATES = S5_GT * S5_STATE
S5_NT = S5_GROUPS // S5_GT


def _params(sem):
    return pltpu.CompilerParams(dimension_semantics=sem, vmem_limit_bytes=VMEM_LIMIT)


def _sigmoid(x):
    return 1.0 / (1.0 + jnp.exp(-x))


def _silu(x):
    return x * _sigmoid(x)


def _dot(a, b):
    return jnp.dot(a, b, preferred_element_type=F32)


def _dot_nt(a, b, precision=None):
    return lax.dot_general(a, b, (((1,), (1,)), ((), ())), precision=precision,
                           preferred_element_type=F32)


def _dot_tn(a, b):
    return lax.dot_general(a, b, (((0,), (0,)), ((), ())), preferred_element_type=F32)


def _mod_spec(kind, tm, tn, tiles_per_seq, col_blocked):
    if kind == "prompt":
        if col_blocked:
            return pl.BlockSpec((None, 1, tn), lambda i, j: (i // tiles_per_seq, 0, j))
        return pl.BlockSpec((None, 1, tn), lambda i, j: (i // tiles_per_seq, 0, 0))
    if col_blocked:
        return pl.BlockSpec((tm, tn), lambda i, j: (i, j))
    return pl.BlockSpec((tm, tn), lambda i, j: (i, 0))


def _ada_body(c_ref, w_ref, b_ref, o_ref):
    a = _silu(c_ref[...]).astype(BF16)
    o_ref[...] = _dot(a, w_ref[...].astype(BF16)) + b_ref[...]


def _ada(c_all, w_all, b, layer):
    m, d = c_all.shape
    n = w_all.shape[2]
    tn = 1024
    return pl.pallas_call(
        _ada_body,
        grid=(n // tn,),
        in_specs=[pl.BlockSpec((m, d), lambda j: (0, 0)),
                  pl.BlockSpec((None, d, tn), lambda j: (layer, 0, j)),
                  pl.BlockSpec((1, tn), lambda j: (0, j))],
        out_specs=pl.BlockSpec((m, tn), lambda j: (0, j)),
        out_shape=jax.ShapeDtypeStruct((m, n), F32),
        compiler_params=_params(("arbitrary",)),
    )(c_all, w_all, b.reshape(1, n))


def _norm_mod(x, g, sc, sh):
    y = x * lax.rsqrt(jnp.mean(x * x, axis=-1, keepdims=True) + NORM_EPS)
    return (y * g) * (1.0 + sc) + sh


def _nmm_body(x_ref, g_ref, sc_ref, sh_ref, w_ref, o_ref, h_ref):
    @pl.when(pl.program_id(1) == 0)
    def _():
        h_ref[...] = _norm_mod(x_ref[...], g_ref[...], sc_ref[...], sh_ref[...]).astype(BF16)

    o_ref[...] = _dot(h_ref[...], w_ref[...])


def _norm_mod_matmul(x, g, sc, sh, w, kind, tm, tn, tiles_per_seq):
    m, d = x.shape
    w_bf, layer = w
    n = w_bf.shape[2]
    mod = _mod_spec(kind, tm, d, tiles_per_seq, False)
    return pl.pallas_call(
        _nmm_body,
        grid=(m // tm, n // tn),
        in_specs=[pl.BlockSpec((tm, d), lambda i, j: (i, 0)),
                  pl.BlockSpec((1, d), lambda i, j: (0, 0)),
                  mod, mod,
                  pl.BlockSpec((None, d, tn), lambda i, j: (layer, 0, j))],
        out_specs=pl.BlockSpec((tm, tn), lambda i, j: (i, j)),
        out_shape=jax.ShapeDtypeStruct((m, n), F32),
        scratch_shapes=[pltpu.VMEM((tm, d), BF16)],
        compiler_params=_params(("arbitrary", "arbitrary")),
    )(x, g.reshape(1, d), sc, sh, w_bf)


def _ffn_in_body(x_ref, g_ref, sc_ref, sh_ref, w1_ref, w2_ref, o_ref, h_ref):
    @pl.when(pl.program_id(1) == 0)
    def _():
        h_ref[...] = _norm_mod(x_ref[...], g_ref[...], sc_ref[...], sh_ref[...]).astype(BF16)

    h = h_ref[...]
    o_ref[...] = (_silu(_dot(h, w1_ref[...])) * _dot(h, w2_ref[...])).astype(BF16)


def _norm_mod_ffn_in(x, g, sc, sh, w, kind, tm, tn, tiles_per_seq):
    m, d = x.shape
    w_bf, layer = w
    nj = D_FF // tn
    mod = _mod_spec(kind, tm, d, tiles_per_seq, False)
    return pl.pallas_call(
        _ffn_in_body,
        grid=(m // tm, nj),
        in_specs=[pl.BlockSpec((tm, d), lambda i, j: (i, 0)),
                  pl.BlockSpec((1, d), lambda i, j: (0, 0)),
                  mod, mod,
                  pl.BlockSpec((None, d, tn), lambda i, j: (layer, 0, j)),
                  pl.BlockSpec((None, d, tn), lambda i, j: (layer, 0, j + nj))],
        out_specs=pl.BlockSpec((tm, tn), lambda i, j: (i, j)),
        out_shape=jax.ShapeDtypeStruct((m, D_FF), BF16),
        scratch_shapes=[pltpu.VMEM((tm, d), BF16)],
        compiler_params=_params(("arbitrary", "arbitrary")),
    )(x, g.reshape(1, d), sc, sh, w_bf, w_bf)


def _mmres_body(a_ref, w_ref, x_ref, g_ref, o_ref):
    o_ref[...] = x_ref[...] + g_ref[...] * _dot(a_ref[...], w_ref[...])


def _matmul_residual(a_bf, w, x, gate, kind, tm, tn, tiles_per_seq):
    m, k = a_bf.shape
    w_bf, layer = w
    n = w_bf.shape[2]
    return pl.pallas_call(
        _mmres_body,
        grid=(m // tm, n // tn),
        in_specs=[pl.BlockSpec((tm, k), lambda i, j: (i, 0)),
                  pl.BlockSpec((None, k, tn), lambda i, j: (layer, 0, j)),
                  pl.BlockSpec((tm, tn), lambda i, j: (i, j)),
                  _mod_spec(kind, tm, tn, tiles_per_seq, True)],
        out_specs=pl.BlockSpec((tm, tn), lambda i, j: (i, j)),
        out_shape=jax.ShapeDtypeStruct((m, n), F32),
        compiler_params=_params(("arbitrary", "arbitrary")),
    )(a_bf, w_bf, x, gate)


def _merge_body(oa_ref, or_ref, os_ref, wb_ref, ga_ref, gr_ref, gs_ref, o_ref):
    acc = None
    for n, (o_r, g_r) in enumerate(((oa_ref, ga_ref), (or_ref, gr_ref), (os_ref, gs_ref))):
        term = _sigmoid(g_r[...]) * _dot(o_r[...].astype(BF16), wb_ref[n])
        acc = term if acc is None else acc + term
    o_ref[...] = acc.astype(BF16)


def _branch_merge(o_att, o_ret, o_s5, z, w, tm, tn):
    m = z.shape[0]
    wb_bf, layer = w
    gate0 = OFF_GATE // tn
    per = D_MODEL // tn
    o_spec = pl.BlockSpec((tm, BRANCH_W), lambda i, j: (i, 0))

    def gate_spec(n):
        return pl.BlockSpec((tm, tn), lambda i, j: (i, gate0 + n * per + j))

    return pl.pallas_call(
        _merge_body,
        grid=(m // tm, per),
        in_specs=[o_spec, o_spec, o_spec,
                  pl.BlockSpec((None, N_BRANCH, BRANCH_W, tn), lambda i, j: (layer, 0, 0, j)),
                  gate_spec(0), gate_spec(1), gate_spec(2)],
        out_specs=pl.BlockSpec((tm, tn), lambda i, j: (i, j)),
        out_shape=jax.ShapeDtypeStruct((m, D_MODEL), BF16),
        compiler_params=_params(("arbitrary", "arbitrary")),
    )(o_att, o_ret, o_s5, wb_bf, z, z, z)


def _fnorm_body(x_ref, g_ref, o_ref):
    x = x_ref[...]
    o_ref[...] = x * lax.rsqrt(jnp.mean(x * x, axis=-1, keepdims=True) + NORM_EPS) * g_ref[...]


def _final_norm(x, g, tm):
    m, d = x.shape
    return pl.pallas_call(
        _fnorm_body,
        grid=(m // tm,),
        in_specs=[pl.BlockSpec((tm, d), lambda i: (i, 0)), pl.BlockSpec((1, d), lambda i: (0, 0))],
        out_specs=pl.BlockSpec((tm, d), lambda i: (i, 0)),
        out_shape=jax.ShapeDtypeStruct((m, d), F32),
        compiler_params=_params(("arbitrary",)),
    )(x, g.reshape(1, d))


def _bucket_np(dist):
    n = np.maximum(dist, 0)
    max_exact = REL_BUCKETS // 2
    nf = np.maximum(n, 1).astype(np.float32)
    large = max_exact + (np.log(nf / max_exact) / math.log(REL_MAX_DIST / max_exact)
                         * (REL_BUCKETS - max_exact)).astype(np.int32)
    large = np.minimum(large, REL_BUCKETS - 1)
    return np.where(n < max_exact, n, large).astype(np.int32)


def _bias_p_body(rb_ref, bk_ref, o_ref):
    h = pl.program_id(0)
    blk = MOBA_BLOCK
    far = rb_ref[REL_BUCKETS - 1, h]
    for s in range(2):
        bk = bk_ref[s]
        acc = jnp.zeros((blk, blk), F32)
        for b in range(REL_BUCKETS):
            acc = jnp.where(bk == b, rb_ref[b, h], acc)
        acc = (acc - far) * LOG2E
        if s == 0:
            rk = lax.broadcasted_iota(jnp.int32, (blk, blk), 0)
            rq = lax.broadcasted_iota(jnp.int32, (blk, blk), 1)
            acc = jnp.where(rk <= rq, acc, 2 * NEG)
        o_ref[s] = acc


def _bias_tables_prompt(rel_bias):
    blk = MOBA_BLOCK
    rk = np.arange(blk)[:, None]
    rq = np.arange(blk)[None, :]
    buckets = np.stack([_bucket_np(rq - rk), _bucket_np(blk + rq - rk)])
    assert int(_bucket_np(np.array([blk + 1]))[0]) == REL_BUCKETS - 1
    return pl.pallas_call(
        _bias_p_body,
        grid=(ATT_HEADS,),
        in_specs=[pl.BlockSpec(memory_space=pltpu.SMEM),
                  pl.BlockSpec((2, blk, blk), lambda h: (0, 0, 0))],
        out_specs=pl.BlockSpec((None, 2, blk, blk), lambda h: (h, 0, 0, 0)),
        out_shape=jax.ShapeDtypeStruct((ATT_HEADS, 2, blk, blk), F32),
        compiler_params=_params(("arbitrary",)),
    )(rel_bias, jnp.asarray(buckets))


def _topk_select(s, allowed, idx, n_cand):
    s = jnp.where(allowed, s, -jnp.inf)
    rank = jnp.zeros(s.shape, jnp.int32)
    for m in range(n_cand):
        sm = s[m:m + 1]
        beats = (sm > s) | ((sm == s) & (idx > m))
        rank = rank + beats.astype(jnp.int32)
    return jnp.where(allowed & (rank < MOBA_TOPK), 1.0, 0.0)


def _moba_p_body(q_ref, k_ref, v_ref, tb_ref, o_ref, kbf_ref, vt_ref, kmean_ref, sel_ref, *, nb):
    i = pl.program_id(2)
    blk = MOBA_BLOCK
    dh = ATT_DH

    @pl.when(i == 0)
    def _():
        kf = k_ref[...]
        for hh in range(2):
            kbf_ref[hh] = kf[:, hh * dh:(hh + 1) * dh].astype(BF16)
        vt_ref[...] = v_ref[...].T.astype(BF16)
        kmean_ref[...] = jnp.mean(kf.reshape(nb, blk, LANES), axis=1)

    q = q_ref[...]
    grp = MOBA_KEY_GROUP
    qs_all, prev_sel, carry0 = [], [], []
    for hh in range(2):
        cs = slice(hh * dh, (hh + 1) * dh)
        qh = q[:, cs]
        s = _dot_nt(kmean_ref[:, cs], qh, precision=HIGHEST)
        nidx = lax.broadcasted_iota(jnp.int32, s.shape, 0)
        sel = _topk_select(s, nidx < i, nidx, nb)
        sel_ref[hh] = jnp.where(nidx < i - 1, sel, 0.0)
        prev_sel.append(jnp.sum(jnp.where(nidx == i - 1, sel, 0.0), axis=0, keepdims=True) > 0.5)
        qs_all.append((qh * (dh ** -0.5 * LOG2E)).astype(BF16))
        carry0 += [jnp.full((1, blk), NEG, F32), jnp.zeros((1, blk), F32), jnp.zeros((dh, blk), F32)]

    def softmax_step(carry, lgs, sels, v_ts):
        parts = [[], []]
        for g in range(len(lgs[0])):
            for hh in range(2):
                lg, sl = lgs[hh][g], sels[hh][g]
                cm = jnp.max(lg, axis=0, keepdims=True)
                p = jnp.exp2(lg - (cm if sl is None else jnp.where(sl, cm, -NEG)))
                parts[hh].append((cm if sl is None else jnp.where(sl, cm, NEG),
                                  jnp.sum(p, axis=0, keepdims=True),
                                  _dot(v_ts[hh][:, g * blk:(g + 1) * blk], p.astype(BF16))))
        out = []
        for hh in range(2):
            m, l, acc = carry[3 * hh:3 * hh + 3]
            m_new = m
            for mg, _, _ in parts[hh]:
                m_new = jnp.maximum(m_new, mg)
            a = jnp.exp2(m - m_new)
            l, acc = a * l, a * acc
            for mg, lg_sum, acc_g in parts[hh]:
                wgt = jnp.exp2(mg - m_new)
                l, acc = l + wgt * lg_sum, acc + wgt * acc_g
            out += [m_new, l, acc]
        return tuple(out)

    def body(j, carry):
        st = pl.multiple_of(j * (grp * blk), grp * blk)
        lg_all = [_dot_nt(kbf_ref[hh, pl.ds(st, grp * blk), :], qs_all[hh]) for hh in range(2)]
        lgs = [[lg_all[hh][g * blk:(g + 1) * blk] for g in range(grp)] for hh in range(2)]
        sels = [[sel_ref[hh, pl.ds(j * grp + g, 1), :] > 0.5 for g in range(grp)] for hh in range(2)]
        v_ts = [vt_ref[hh * dh:(hh + 1) * dh, pl.ds(st, grp * blk)] for hh in range(2)]
        return softmax_step(carry, lgs, sels, v_ts)

    fin = lax.fori_loop(0, (i + grp - 2) // grp, body, tuple(carry0))

    own = pl.multiple_of(i * blk, blk)
    prev = pl.multiple_of(jnp.maximum(i - 1, 0) * blk, blk)
    lgs, v_ts = [], []
    for hh in range(2):
        rows = slice(hh * dh, (hh + 1) * dh)
        lgs.append([_dot_nt(kbf_ref[hh, pl.ds(prev, blk), :], qs_all[hh]) + tb_ref[hh, 1],
                    _dot_nt(kbf_ref[hh, pl.ds(own, blk), :], qs_all[hh]) + tb_ref[hh, 0]])
        v_ts.append(jnp.concatenate([vt_ref[rows, pl.ds(prev, blk)], vt_ref[rows, pl.ds(own, blk)]], axis=1))
    fin = softmax_step(fin, lgs, [[prev_sel[hh], None] for hh in range(2)], v_ts)
    outs = [(fin[3 * hh + 2] / fin[3 * hh + 1]).T for hh in range(2)]
    o_ref[...] = jnp.concatenate(outs, axis=1).astype(BF16)


def _moba_prompt(z, bias_tab, n_seq, t):
    blk = MOBA_BLOCK
    nb = t // blk
    assert nb % MOBA_KEY_GROUP == 0
    hp = ATT_HEADS // 2
    kc, vc = OFF_AK // LANES, OFF_AV // LANES
    return pl.pallas_call(
        functools.partial(_moba_p_body, nb=nb),
        grid=(n_seq, hp, nb),
        in_specs=[pl.BlockSpec((blk, LANES), lambda b, h, i: (b * nb + i, h)),
                  pl.BlockSpec((t, LANES), lambda b, h, i: (b, kc + h)),
                  pl.BlockSpec((t, LANES), lambda b, h, i: (b, vc + h)),
                  pl.BlockSpec((2, 2, blk, blk), lambda b, h, i: (h, 0, 0, 0))],
        out_specs=pl.BlockSpec((blk, LANES), lambda b, h, i: (b * nb + i, h)),
        out_shape=jax.ShapeDtypeStruct((n_seq * t, ATT_W), BF16),
        scratch_shapes=[pltpu.VMEM((2, t, ATT_DH), BF16), pltpu.VMEM((LANES, t), BF16),
                        pltpu.VMEM((nb, LANES), F32), pltpu.VMEM((2, nb, blk), F32)],
        compiler_params=_params(("arbitrary", "arbitrary", "arbitrary")),
    )(z, z, z, bias_tab)


PAGES_PER_STEP = 8
PAGES_PER_BLOCK = MOBA_BLOCK // PAGE_SIZE


def _moba_s_body(pt_ref, *refs, n_steps):
    del pt_ref
    kpages, vpages = refs[:PAGES_PER_STEP], refs[PAGES_PER_STEP:2 * PAGES_PER_STEP]
    q_ref, kn_ref, vn_ref, rbt_ref, bk_ref, o_ref, m_s, l_s, s_s, acc_s = refs[2 * PAGES_PER_STEP:]
    g = pl.program_id(1)
    h, dh, w = ATT_HEADS, ATT_DH, ATT_W
    n_pages = n_steps * PAGES_PER_STEP
    row = lax.broadcasted_iota(jnp.int32, (h, w), 0)
    lane = lax.broadcasted_iota(jnp.int32, (h, w), 1)
    own_head = (lane // dh) == row
    qbd = jnp.where(own_head, q_ref[...] * (dh ** -0.5 * LOG2E), 0.0)
    q_hi = qbd.astype(BF16)
    q_lo = (qbd - q_hi.astype(F32)).astype(BF16)
    q2 = jnp.concatenate([q_hi, q_lo], axis=0)
    far_bias = rbt_ref[:, REL_BUCKETS - 1:REL_BUCKETS] * LOG2E
    bk = bk_ref[...]
    near_bias = jnp.zeros((h, PAGE_SIZE), F32)
    for b in range(REL_BUCKETS):
        near_bias = jnp.where(bk == b, rbt_ref[:, b:b + 1], near_bias)
    is_last = (g == n_steps - 1).astype(F32)
    last_bias = far_bias + is_last * (near_bias * LOG2E - far_bias)

    raws = []
    for j in range(PAGES_PER_STEP):
        raw2 = _dot(q2, kpages[j][...].reshape(w, PAGE_SIZE).astype(BF16))
        raws.append(raw2[:h] + raw2[h:])
    stats = []
    for j, raw in enumerate(raws):
        lg = raw + (last_bias if j == PAGES_PER_STEP - 1 else far_bias)
        mb = jnp.max(lg, axis=1, keepdims=True)
        p = jnp.exp2(lg - mb)
        stats.append((mb, jnp.sum(p, axis=1, keepdims=True), jnp.sum(raw, axis=1, keepdims=True), p.astype(BF16)))
    for j, (mb, lb, sb, p) in enumerate(stats):
        idx = g * PAGES_PER_STEP + j
        m_s[idx] = mb
        l_s[idx] = lb
        s_s[idx] = sb
        acc_s[idx] = _dot_nt(p, vpages[j][...].reshape(w, PAGE_SIZE).astype(BF16))

    @pl.when(g == n_steps - 1)
    def _():
        nblk = n_pages // PAGES_PER_BLOCK
        s = jnp.sum(s_s[...].reshape(nblk, PAGES_PER_BLOCK, h, 1), axis=1)
        nidx = lax.broadcasted_iota(jnp.int32, s.shape, 0)
        sel = _topk_select(s, nidx >= 0, nidx, nblk)
        selp = jnp.broadcast_to(sel[:, None], (nblk, PAGES_PER_BLOCK, h, 1)).reshape(n_pages, h, 1) > 0.5
        ln = jnp.sum(qbd * kn_ref[...], axis=1, keepdims=True) + rbt_ref[:, 0:1] * LOG2E
        mm = jnp.where(selp, m_s[...], NEG)
        mx = jnp.maximum(jnp.max(mm, axis=0), ln)
        wgt = jnp.where(selp, jnp.exp2(mm - mx), 0.0)
        wn = jnp.exp2(ln - mx)
        den = jnp.sum(wgt * l_s[...], axis=0) + wn
        tot = jnp.sum(wgt * acc_s[...], axis=0) + wn * vn_ref[...]
        o_ref[...] = jnp.sum(jnp.where(own_head, tot / den, 0.0), axis=0, keepdims=True)


def _moba_sample(z_s, cache_k, cache_v, layer, page_table, rel_bias_t):
    n_seq, n_pages = page_table.shape
    assert n_pages % PAGES_PER_STEP == 0 and PAGES_PER_STEP % PAGES_PER_BLOCK == 0
    n_steps = n_pages // PAGES_PER_STEP
    assert int(_bucket_np(np.array([PAGE_SIZE + 1]))[0]) == REL_BUCKETS - 1
    near_buckets = _bucket_np(PAGE_SIZE - np.arange(PAGE_SIZE)).reshape(1, PAGE_SIZE)
    rows = lambda off: z_s[:, off:off + ATT_W].reshape(n_seq, 1, ATT_W)

    def page_spec(j):
        return pl.BlockSpec((None, None, ATT_HEADS, ATT_DH, PAGE_SIZE),
                            lambda s, g, pt: (layer, pt[s, g * PAGES_PER_STEP + j], 0, 0, 0))

    row_spec = pl.BlockSpec((None, 1, ATT_W), lambda s, g, pt: (s, 0, 0))
    stat = pltpu.VMEM((n_pages, ATT_HEADS, 1), F32)
    grid_spec = pltpu.PrefetchScalarGridSpec(
        num_scalar_prefetch=1,
        grid=(n_seq, n_steps),
        in_specs=[page_spec(j) for j in range(PAGES_PER_STEP)] * 2 + [
            row_spec, row_spec, row_spec,
            pl.BlockSpec((ATT_HEADS, REL_BUCKETS), lambda s, g, pt: (0, 0)),
            pl.BlockSpec((1, PAGE_SIZE), lambda s, g, pt: (0, 0))],
        out_specs=row_spec,
        scratch_shapes=[stat, stat, stat, pltpu.VMEM((n_pages, ATT_HEADS, ATT_W), F32)],
    )
    out = pl.pallas_call(
        functools.partial(_moba_s_body, n_steps=n_steps),
        grid_spec=grid_spec,
        out_shape=jax.ShapeDtypeStruct((n_seq, 1, ATT_W), F32),
        compiler_params=_params(("arbitrary", "arbitrary")),
    )(page_table, *([cache_k] * PAGES_PER_STEP), *([cache_v] * PAGES_PER_STEP),
      rows(OFF_AQ), rows(OFF_AK), rows(OFF_AV), rel_bias_t, jnp.asarray(near_buckets))
    return out.reshape(n_seq, ATT_W)


def _rope_body(inv_ref, sgn_ref, cos_ref, sin_ref, *, pos0, step, rows):
    r = lax.broadcasted_iota(jnp.int32, (rows, LANES), 0) + pl.program_id(0) * rows
    ang = (pos0 + step * r).astype(F32) * inv_ref[...]
    cos_ref[...] = jnp.cos(ang)
    sin_ref[...] = jnp.sin(ang) * sgn_ref[...]


def _rope_tables(n_rows, pos0, step):
    half = RET_DK // 2
    inv = 1.0 / (ROPE_BASE ** jnp.linspace(0.0, 1.0, half))
    inv_row = jnp.tile(inv, LANES // half).reshape(1, LANES).astype(F32)
    sgn = np.where((np.arange(LANES) % RET_DK) < half, -1.0, 1.0).astype(np.float32).reshape(1, LANES)
    rows = min(n_rows, 512)
    spec = pl.BlockSpec((rows, LANES), lambda i: (i, 0))
    cst = pl.BlockSpec((1, LANES), lambda i: (0, 0))
    return pl.pallas_call(
        functools.partial(_rope_body, pos0=pos0, step=step, rows=rows),
        grid=(n_rows // rows,),
        in_specs=[cst, cst],
        out_specs=[spec, spec],
        out_shape=[jax.ShapeDtypeStruct((n_rows, LANES), F32)] * 2,
        compiler_params=_params(("arbitrary",)),
    )(inv_row, jnp.asarray(sgn))


def _rotary128(x, cos, sin_signed):
    half = RET_DK // 2
    lane = lax.broadcasted_iota(jnp.int32, x.shape, 1)
    partner = jnp.where((lane % RET_DK) < half,
                        pltpu.roll(x, LANES - half, 1), pltpu.roll(x, half, 1))
    return x * cos + partner * sin_signed


def _groupnorm_gate(o, g):
    mu = jnp.mean(o, axis=-1, keepdims=True)
    var = jnp.mean((o - mu) ** 2, axis=-1, keepdims=True)
    return _silu(g) * ((o - mu) * lax.rsqrt(var + GN_EPS))


def _ret_log_decay():
    return jnp.log(1.0 - 2.0 ** (-5.0 - jnp.arange(RET_HEADS, dtype=F32)))


def _ret_p_body(lg_ref, q_ref, k_ref, v_ref, g_ref, cos_ref, sin_ref, o_ref, st_ref, *, chunk):
    hp = pl.program_id(1)
    c = pl.program_id(2)

    @pl.when(c == 0)
    def _():
        st_ref[...] = jnp.zeros(st_ref.shape, F32)

    cos, sin = cos_ref[...], sin_ref[...]
    q = _rotary128(q_ref[...], cos, sin)
    k = _rotary128(k_ref[...], cos, sin) * RET_DK ** -0.5
    ii = lax.broadcasted_iota(jnp.int32, (chunk, chunk), 0)
    jj = lax.broadcasted_iota(jnp.int32, (chunk, chunk), 1)
    diff = (ii - jj).astype(F32)
    ri = lax.broadcasted_iota(jnp.int32, (chunk, 1), 0).astype(F32)
    for hh in range(2):
        lgh = lg_ref[hp * 2 + hh]
        dmask = jnp.where(diff >= 0, jnp.exp(lgh * jnp.maximum(diff, 0.0)), 0.0)
        qh = q[:, hh * RET_DK:(hh + 1) * RET_DK].astype(BF16)
        kh = k[:, hh * RET_DK:(hh + 1) * RET_DK]
        vh = v_ref[:, hh * RET_DV:(hh + 1) * RET_DV].astype(BF16)
        s0 = st_ref[hh]
        a = _dot_nt(qh, kh.astype(BF16)) * dmask
        inner = _dot(a.astype(BF16), vh)
        cross = _dot(qh, s0.astype(BF16)) * jnp.exp(lgh * (ri + 1.0))
        kdec = (kh * jnp.exp(lgh * (chunk - 1.0 - ri))).astype(BF16)
        st_ref[hh] = jnp.exp(lgh * chunk + jnp.zeros((1, 1), F32)) * s0 + _dot_tn(kdec, vh)
        gh = g_ref[:, hh * RET_DV:(hh + 1) * RET_DV]
        o_ref[:, hh * RET_DV:(hh + 1) * RET_DV] = _groupnorm_gate(inner + cross, gh).astype(BF16)


def _ret_prompt(z, cos_tab, sin_tab, log_decay, n_seq, t):
    chunk = RET_CHUNK
    nc = t // chunk
    hp = RET_HEADS // 2
    qc, kc = OFF_RQ // LANES, OFF_RK // LANES
    vc, gc = OFF_RV // (2 * RET_DV), OFF_RG // (2 * RET_DV)
    return pl.pallas_call(
        functools.partial(_ret_p_body, chunk=chunk),
        grid=(n_seq, hp, nc),
        in_specs=[pl.BlockSpec(memory_space=pltpu.SMEM),
                  pl.BlockSpec((chunk, LANES), lambda b, h, c: (b * nc + c, qc + h)),
                  pl.BlockSpec((chunk, LANES), lambda b, h, c: (b * nc + c, kc + h)),
                  pl.BlockSpec((chunk, 2 * RET_DV), lambda b, h, c: (b * nc + c, vc + h)),
                  pl.BlockSpec((chunk, 2 * RET_DV), lambda b, h, c: (b * nc + c, gc + h)),
                  pl.BlockSpec((chunk, LANES), lambda b, h, c: (c, 0)),
                  pl.BlockSpec((chunk, LANES), lambda b, h, c: (c, 0))],
        out_specs=[pl.BlockSpec((chunk, 2 * RET_DV), lambda b, h, c: (b * nc + c, h)),
                   pl.BlockSpec((None, 2, RET_DK, RET_DV), lambda b, h, c: (b, h, 0, 0))],
        out_shape=[jax.ShapeDtypeStruct((n_seq * t, RET_V_W), BF16),
                   jax.ShapeDtypeStruct((n_seq, RET_HEADS, RET_DK, RET_DV), F32)],
        compiler_params=_params(("arbitrary", "arbitrary", "arbitrary")),
    )(log_decay, z, z, z, z, cos_tab, sin_tab)


def _ret_s_prep_body(q_ref, k_ref, cos_ref, sin_ref, qo_ref, ko_ref):
    cos, sin = cos_ref[0:1, :], sin_ref[0:1, :]
    for j in range(RET_QK_W // LANES):
        sl = slice(j * LANES, (j + 1) * LANES)
        qo_ref[:, sl] = _rotary128(q_ref[:, sl], cos, sin)
        ko_ref[:, sl] = _rotary128(k_ref[:, sl], cos, sin) * RET_DK ** -0.5


def _ret_s_body(lg_ref, q_ref, k_ref, v_ref, g_ref, s0_ref, o_ref, sn_ref):
    gam = jnp.exp(lg_ref[pl.program_id(1)] + jnp.zeros((1, 1, LANES), F32))
    q, k, v, s0 = q_ref[...], k_ref[...], v_ref[...], s0_ref[...]
    o = jnp.sum(q * s0, axis=1, keepdims=True) * gam + jnp.sum(q * k, axis=1, keepdims=True) * v
    sn_ref[...] = gam * s0 + k * v
    o_ref[...] = _groupnorm_gate(o, g_ref[...])


def _ret_sample(z_s, state, cos_tab, sin_tab, log_decay):
    n_seq = z_s.shape[0]
    tab = pl.BlockSpec((SUBLANES, LANES), lambda i: (0, 0))
    q_rot, k_rot = pl.pallas_call(
        _ret_s_prep_body,
        grid=(1,),
        in_specs=[pl.BlockSpec((n_seq, RET_QK_W), lambda i: (0, OFF_RQ // RET_QK_W)),
                  pl.BlockSpec((n_seq, RET_QK_W), lambda i: (0, OFF_RK // RET_QK_W)), tab, tab],
        out_specs=[pl.BlockSpec((n_seq, RET_QK_W), lambda i: (0, 0))] * 2,
        out_shape=[jax.ShapeDtypeStruct((n_seq, RET_QK_W), F32)] * 2,
        compiler_params=_params(("arbitrary",)),
    )(z_s, z_s, cos_tab, sin_tab)
    col = lambda a: a.reshape(n_seq, RET_HEADS, RET_DK, 1)
    row = lambda a: a.reshape(n_seq, RET_HEADS, 1, RET_DV)
    bt = 16
    col_spec = pl.BlockSpec((bt, None, RET_DK, 1), lambda i, h: (i, h, 0, 0))
    row_spec = pl.BlockSpec((bt, None, 1, RET_DV), lambda i, h: (i, h, 0, 0))
    st_spec = pl.BlockSpec((bt, None, RET_DK, RET_DV), lambda i, h: (i, h, 0, 0))
    state_all, layer = state
    st_in_spec = pl.BlockSpec((None, bt, None, RET_DK, RET_DV), lambda i, h: (layer, i, h, 0, 0))
    o, s_new = pl.pallas_call(
        _ret_s_body,
        grid=(n_seq // bt, RET_HEADS),
        in_specs=[pl.BlockSpec(memory_space=pltpu.SMEM), col_spec, col_spec, row_spec, row_spec, st_in_spec],
        out_specs=[row_spec, st_spec],
        out_shape=[jax.ShapeDtypeStruct((n_seq, RET_HEADS, 1, RET_DV), F32),
                   jax.ShapeDtypeStruct((n_seq, RET_HEADS, RET_DK, RET_DV), F32)],
        compiler_params=_params(("arbitrary", "arbitrary")),
    )(log_decay, col(q_rot), col(k_rot), row(z_s[:, OFF_RV:OFF_RV + RET_V_W]),
      row(z_s[:, OFF_RG:OFF_RG + RET_V_W]), state_all)
    return o.reshape(n_seq, RET_V_W), s_new


def _s5_prep_body(are_ref, aim_ref, ldt_ref, bre_ref, bim_ref, pwr_ref, pwi_ref, bbr_ref, bbi_ref):
    ar, ai = are_ref[...], aim_ref[...]
    dt = jnp.exp(ldt_ref[...])
    kk = (lax.broadcasted_iota(jnp.int32, pwr_ref.shape, 1) + 1).astype(F32)
    mag = jnp.exp(ar * dt * kk)
    ang = ai * dt * kk
    pwr_ref[...] = mag * jnp.cos(ang)
    pwi_ref[...] = mag * jnp.sin(ang)
    mag1 = jnp.exp(ar * dt)
    nr = mag1 * jnp.cos(ai * dt) - 1.0
    ni = mag1 * jnp.sin(ai * dt)
    den = ar * ar + ai * ai
    cr = (nr * ar + ni * ai) / den
    ci = (ni * ar - nr * ai) / den
    bre, bim = bre_ref[...], bim_ref[...]
    bbr_ref[...] = cr * bre - ci * bim
    bbi_ref[...] = cr * bim + ci * bre


def _s5_prepare(a_re, a_im, log_dt, b_re, b_im, c_re, c_im, d):
    g, p, c = S5_GROUPS, S5_STATE, S5_GROUP
    g3 = lambda a: a.reshape(g, 1, p)
    ldt = jnp.broadcast_to(log_dt.reshape(g, 1, 1), (g, 1, p))
    pwr, pwi, bbr, bbi = pl.pallas_call(
        _s5_prep_body,
        out_shape=[jax.ShapeDtypeStruct((g, SUBLANES, p), F32)] * 2
        + [jax.ShapeDtypeStruct((g, c, p), F32)] * 2,
    )(g3(a_re), g3(a_im), ldt, b_re.transpose(0, 2, 1), b_im.transpose(0, 2, 1))
    eye = jnp.eye(S5_GT, dtype=F32)
    nt = S5_NT

    def in_tile(bb):
        return jnp.einsum("tgcp,gh->tgchp", bb.reshape(nt, S5_GT, c, p), eye).reshape(nt, S5_GT * c, S5_GT * p)

    def out_tile(cc):
        return jnp.einsum("tgcp,gh->tgphc", cc.reshape(nt, S5_GT, c, p), eye).reshape(nt, S5_GT * p, S5_GT * c)

    b_tile = jnp.concatenate([in_tile(bbr), in_tile(bbi)], axis=2).astype(BF16)
    c_tile = jnp.concatenate([out_tile(c_re), out_tile(-c_im)], axis=1).astype(BF16)
    d_tile = d.reshape(nt, 1, S5_GT * c)

    def pw_tile(pw):
        return pw.reshape(nt, S5_GT, SUBLANES, p).transpose(0, 2, 1, 3).reshape(nt, SUBLANES, S5_GT * p)

    pr, pi = pw_tile(pwr), pw_tile(pwi)
    rows = jnp.arange(SUBLANES)[None, :, None]
    slabs = []
    for shift in (1, 2, 4):
        keep = rows >= shift
        slabs += [jnp.where(keep, pr[:, shift - 1:shift, :], 0.0), jnp.where(keep, pi[:, shift - 1:shift, :], 0.0)]
    scan_c = jnp.stack(slabs + [pr, pi], axis=1)
    lam1 = jnp.stack([pr[:, 0:1, :], pi[:, 0:1, :]], axis=1)
    return b_tile, c_tile, d_tile, scan_c, lam1


def _gelu_tanh(y):
    return 0.5 * y * (1.0 + jnp.tanh(math.sqrt(2.0 / math.pi) * (y + 0.044715 * (y * y * y))))


def _s5_p_body(u_ref, bt_ref, ct_ref, d_ref, sc_ref, z_ref, st_ref, x_ref, carry_ref, *, tc):
    ns = S5_TILE_STATES

    @pl.when(pl.program_id(2) == 0)
    def _():
        carry_ref[...] = jnp.zeros(carry_ref.shape, F32)

    u = u_ref[...]
    x_ref[...] = _dot(u.astype(BF16), bt_ref[...])

    def tile(t, carry):
        cr, ci = carry
        st = pl.multiple_of(t * SUBLANES, SUBLANES)
        xr = x_ref[pl.ds(st, SUBLANES), 0:ns]
        xi = x_ref[pl.ds(st, SUBLANES), ns:2 * ns]
        for s, shift in enumerate((1, 2, 4)):
            ar, ai = sc_ref[2 * s], sc_ref[2 * s + 1]
            sr, si = pltpu.roll(xr, shift, 0), pltpu.roll(xi, shift, 0)
            xr, xi = xr + ar * sr - ai * si, xi + ar * si + ai * sr
        pr, pi = sc_ref[6], sc_ref[7]
        xr, xi = xr + pr * cr - pi * ci, xi + pr * ci + pi * cr
        x_ref[pl.ds(st, SUBLANES), 0:ns] = xr
        x_ref[pl.ds(st, SUBLANES), ns:2 * ns] = xi
        return xr[SUBLANES - 1:SUBLANES, :], xi[SUBLANES - 1:SUBLANES, :]

    cr, ci = lax.fori_loop(0, tc // SUBLANES, tile, (carry_ref[:, 0:ns], carry_ref[:, ns:2 * ns]))
    last = jnp.concatenate([cr, ci], axis=1)
    carry_ref[...] = last
    st_ref[...] = last
    y = _dot(x_ref[...].astype(BF16), ct_ref[...]) + d_ref[...] * u
    z_ref[...] = _gelu_tanh(y)


def _s5_prompt(z, tiles, n_seq, t):
    b_tile, c_tile, d_tile, scan_c, _ = tiles
    tc = 1024
    nc = t // tc
    uc = OFF_SU // LANES
    ns2 = 2 * S5_TILE_STATES
    zs, st = pl.pallas_call(
        functools.partial(_s5_p_body, tc=tc),
        grid=(n_seq, S5_NT, nc),
        in_specs=[pl.BlockSpec((tc, LANES), lambda b, g, c: (b * nc + c, uc + g)),
                  pl.BlockSpec((None, LANES, ns2), lambda b, g, c: (g, 0, 0)),
                  pl.BlockSpec((None, ns2, LANES), lambda b, g, c: (g, 0, 0)),
                  pl.BlockSpec((None, 1, LANES), lambda b, g, c: (g, 0, 0)),
                  pl.BlockSpec((None, 8, SUBLANES, S5_TILE_STATES), lambda b, g, c: (g, 0, 0, 0))],
        out_specs=[pl.BlockSpec((tc, LANES), lambda b, g, c: (b * nc + c, g)),
                   pl.BlockSpec((None, None, 1, ns2), lambda b, g, c: (b, g, 0, 0))],
        out_shape=[jax.ShapeDtypeStruct((n_seq * t, S5_W), F32),
                   jax.ShapeDtypeStruct((n_seq, S5_NT, 1, ns2), F32)],
        scratch_shapes=[pltpu.VMEM((tc, ns2), F32), pltpu.VMEM((1, ns2), F32)],
        compiler_params=_params(("arbitrary", "arbitrary", "arbitrary")),
    )(z, b_tile, c_tile, d_tile, scan_c)
    s_re = st[:, :, 0, :S5_TILE_STATES].reshape(n_seq, S5_GROUPS, S5_STATE)
    s_im = st[:, :, 0, S5_TILE_STATES:].reshape(n_seq, S5_GROUPS, S5_STATE)
    return zs, s_re, s_im


def _s5_s_body(u_ref, bt_ref, ct_ref, d_ref, l1_ref, x0_ref, z_ref, xn_ref):
    ns = S5_TILE_STATES
    u = u_ref[...]
    bu = _dot(u.astype(BF16), bt_ref[...])
    lr, li = l1_ref[0], l1_ref[1]
    x0r, x0i = x0_ref[:, 0:ns], x0_ref[:, ns:2 * ns]
    x = jnp.concatenate([bu[:, 0:ns] + lr * x0r - li * x0i, bu[:, ns:2 * ns] + lr * x0i + li * x0r], axis=1)
    xn_ref[...] = x
    z_ref[...] = _gelu_tanh(_dot(x.astype(BF16), ct_ref[...]) + d_ref[...] * u)


def _s5_sample(z_s, tiles, x0_re, x0_im):
    b_tile, c_tile, d_tile, _, lam1 = tiles
    n_seq = z_s.shape[0]
    uc = OFF_SU // LANES
    ns2 = 2 * S5_TILE_STATES

    def to_tiles(a):
        return a.reshape(n_seq, S5_NT, S5_TILE_STATES).transpose(1, 0, 2)

    x0 = jnp.concatenate([to_tiles(x0_re), to_tiles(x0_im)], axis=2)
    zs, xn = pl.pallas_call(
        _s5_s_body,
        grid=(S5_NT,),
        in_specs=[pl.BlockSpec((n_seq, LANES), lambda g: (0, uc + g)),
                  pl.BlockSpec((None, LANES, ns2), lambda g: (g, 0, 0)),
                  pl.BlockSpec((None, ns2, LANES), lambda g: (g, 0, 0)),
                  pl.BlockSpec((None, 1, LANES), lambda g: (g, 0, 0)),
                  pl.BlockSpec((None, 2, 1, S5_TILE_STATES), lambda g: (g, 0, 0, 0)),
                  pl.BlockSpec((None, n_seq, ns2), lambda g: (g, 0, 0))],
        out_specs=[pl.BlockSpec((n_seq, LANES), lambda g: (0, g)),
                   pl.BlockSpec((None, n_seq, ns2), lambda g: (g, 0, 0))],
        out_shape=[jax.ShapeDtypeStruct((n_seq, S5_W), F32),
                   jax.ShapeDtypeStruct((S5_NT, n_seq, ns2), F32)],
        compiler_params=_params(("arbitrary",)),
    )(z_s, b_tile, c_tile, d_tile, lam1, x0)

    def from_tiles(a):
        return a.transpose(1, 0, 2).reshape(n_seq, S5_GROUPS, S5_STATE)

    return zs, from_tiles(xn[:, :, :S5_TILE_STATES]), from_tiles(xn[:, :, S5_TILE_STATES:])


def _glu_body(zf_ref, zc_ref, w_ref, o_ref):
    o_ref[...] = (zc_ref[...] * _sigmoid(_dot(zf_ref[...].astype(BF16), w_ref[...]))).astype(BF16)


def _half_glu(zs, w_glu, tm, tn):
    m, w = zs.shape
    w_bf, layer = w_glu
    return pl.pallas_call(
        _glu_body,
        grid=(m // tm, w // tn),
        in_specs=[pl.BlockSpec((tm, w), lambda i, j: (i, 0)),
                  pl.BlockSpec((tm, tn), lambda i, j: (i, j)),
                  pl.BlockSpec((None, w, tn), lambda i, j: (layer, 0, j))],
        out_specs=pl.BlockSpec((tm, tn), lambda i, j: (i, j)),
        out_shape=jax.ShapeDtypeStruct((m, w), BF16),
        compiler_params=_params(("arbitrary", "arbitrary")),
    )(zs, zs, w_bf)


def _kv_t_body(z_ref, o_ref):
    zt = z_ref[...].T
    for hh in range(2):
        o_ref[hh] = zt[hh * ATT_DH:(hh + 1) * ATT_DH, :]


def _kv_transposed(z, off, n_seq, t):
    tt = min(t, 2048)
    nt = t // tt
    col0 = off // LANES
    return pl.pallas_call(
        _kv_t_body,
        grid=(n_seq, ATT_HEADS // 2, nt),
        in_specs=[pl.BlockSpec((tt, LANES), lambda b, h, i: (b * nt + i, col0 + h))],
        out_specs=pl.BlockSpec((None, 2, ATT_DH, tt), lambda b, h, i: (b, h, 0, i)),
        out_shape=jax.ShapeDtypeStruct((n_seq, ATT_HEADS, ATT_DH, t), F32),
        compiler_params=_params(("arbitrary", "arbitrary", "arbitrary")),
    )(z)


def _layer(x, mods, lw, mix, kind, tm, tiles_per_seq):
    norm1, norm2, w_in, w_branch, w_out, w_ffn_in, w_ffn_out = lw
    sh1, sc1, g1, sh2, sc2, g2 = mods
    z = _norm_mod_matmul(x, norm1, sc1, sh1, w_in, kind, tm, 1024, tiles_per_seq)
    (o_att, o_ret, o_s5), state = mix(z)
    merged = _branch_merge(o_att, o_ret, o_s5, z, w_branch, tm, 512)
    x = _matmul_residual(merged, w_out, x, g1, kind, tm, 1024, tiles_per_seq)
    act = _norm_mod_ffn_in(x, norm2, sc2, sh2, w_ffn_in, kind, tm, 512, tiles_per_seq)
    tm_out = min(tm, 512)
    x = _matmul_residual(act, w_ffn_out, x, g2, kind, tm_out, 1024, tiles_per_seq * (tm // tm_out))
    return x, z, state


def kernel(x_prompt, x_sample, c_prompt, c_sample, cache_k, cache_v, page_table, state_ret, state_s5_re, state_s5_im, rel_bias, norm1_g, norm2_g, w_ada, b_ada, w_in, s5_a_re, s5_a_im, s5_log_dt, s5_b_re, s5_b_im, s5_c_re, s5_c_im, s5_d, w_glu, w_branch, w_out, w_ffn_in, w_ffn_out, final_g):
    n_seq, t, d = x_prompt.shape
    n_dec = x_sample.shape[0]
    depth = w_in.shape[0]
    past = page_table.shape[1] * PAGE_SIZE
    xp = x_prompt.reshape(n_seq * t, d)
    xs = x_sample.reshape(n_dec, d)
    tm_p = 1024
    tiles_per_seq = t // tm_p

    bias_p = _bias_tables_prompt(rel_bias)
    rel_bias_t = rel_bias.T
    cache_kt = cache_k.transpose(0, 1, 3, 4, 2)
    cache_vt = cache_v.transpose(0, 1, 3, 4, 2)
    cos_p, sin_p = _rope_tables(t, 0, 1)
    cos_s, sin_s = _rope_tables(SUBLANES, past, 0)
    log_decay = _ret_log_decay()
    pad = (-(n_seq + n_dec)) % SUBLANES
    c_all = jnp.concatenate([c_prompt, c_sample, jnp.zeros((pad, d), F32)], axis=0)

    w_in_bf, w_branch_bf, w_out_bf = w_in.astype(BF16), w_branch.astype(BF16), w_out.astype(BF16)
    w_ffn_in_bf, w_ffn_out_bf, w_glu_bf = w_ffn_in.astype(BF16), w_ffn_out.astype(BF16), w_glu.astype(BF16)

    new_p, new_s = [], []
    for l in range(depth):
        mod = _ada(c_all, w_ada, b_ada[l], l)
        mod_p = mod[:n_seq].reshape(n_seq, 6, 1, d)
        mod_s = mod[n_seq:n_seq + n_dec].reshape(n_dec, 6, d)
        mods_p = tuple(mod_p[:, i] for i in range(6))
        mods_s = tuple(mod_s[:, i] for i in range(6))
        lw = (norm1_g[l], norm2_g[l], (w_in_bf, l), (w_branch_bf, l), (w_out_bf, l),
              (w_ffn_in_bf, l), (w_ffn_out_bf, l))
        glu_w = (w_glu_bf, l)
        tiles = _s5_prepare(s5_a_re[l], s5_a_im[l], s5_log_dt[l], s5_b_re[l], s5_b_im[l],
                            s5_c_re[l], s5_c_im[l], s5_d[l])

        def mix_p(z):
            o_att = _moba_prompt(z, bias_p, n_seq, t)
            o_ret, s_ret = _ret_prompt(z, cos_p, sin_p, log_decay, n_seq, t)
            zs5, s_re, s_im = _s5_prompt(z, tiles, n_seq, t)
            return (o_att, o_ret, _half_glu(zs5, glu_w, tm_p, 512)), (s_ret, s_re, s_im)

        def mix_s(z, l=l):
            o_att = _moba_sample(z, cache_kt, cache_vt, l, page_table, rel_bias_t)
            o_ret, s_ret = _ret_sample(z, (state_ret, l), cos_s, sin_s, log_decay)
            zs5, s_re, s_im = _s5_sample(z, tiles, state_s5_re[l], state_s5_im[l])
            return (o_att, o_ret, _half_glu(zs5, glu_w, n_dec, 512)), (s_ret, s_re, s_im)

        xp, zp, st_p = _layer(xp, mods_p, lw, mix_p, "prompt", tm_p, tiles_per_seq)
        xs, zs, st_s = _layer(xs, mods_s, lw, mix_s, "sample", n_dec, 1)
        kv = lambda z, n, tt, off: z[:, off:off + ATT_W].reshape(n, tt, ATT_HEADS, ATT_DH)
        new_p.append((_kv_transposed(zp, OFF_AK, n_seq, t), _kv_transposed(zp, OFF_AV, n_seq, t)) + st_p)
        new_s.append((kv(zs, n_dec, 1, OFF_AK), kv(zs, n_dec, 1, OFF_AV)) + st_s)

    y_prompt = _final_norm(xp, final_g, tm_p).reshape(n_seq, t, d)
    y_sample = _final_norm(xs, final_g, n_dec).reshape(n_dec, 1, d)
    outs_p = [jnp.stack(a) for a in zip(*new_p)]
    outs_p[0], outs_p[1] = (a.transpose(0, 1, 4, 2, 3) for a in outs_p[:2])
    outs_s = [jnp.stack(a) for a in zip(*new_s)]
    return (y_prompt, y_sample, *outs_p, *outs_s)
```
